```python
import jax
import jax.numpy as jnp
from jax import lax
import numpy as np

D_MODEL = 2048
BATCH = 4
SEQ = 2048
DEPTH = 4
DEC_BATCH = 8
DEC_SEQ = 1
PAST_LEN = 16384
PAGE_SIZE = 128

HG_DK = 128
HG_DV = 128
HG_HEADS = D_MODEL // 256
HG_WIDTH = HG_HEADS * HG_DK
HG_CHUNK = 64
ATT_GROUPS = ((128, 1), (512, 4), (2048, 16))
N_GROUPS = len(ATT_GROUPS)
ATT_HPG = 4
ATT_DH = 128
ATT_HEADS = N_GROUPS * ATT_HPG
ATT_WIDTH = ATT_HEADS * ATT_DH
ATT_OUT = ATT_HPG * ATT_DH
ATT_SCALE = ATT_DH ** -0.5
SPLIT_WIDTHS = (HG_WIDTH, HG_WIDTH, HG_WIDTH, HG_WIDTH, ATT_WIDTH, ATT_WIDTH, ATT_WIDTH, D_MODEL, D_MODEL)
N_IN = sum(SPLIT_WIDTHS)
SPLIT_POINTS = tuple(sum(SPLIT_WIDTHS[:j + 1]) for j in range(len(SPLIT_WIDTHS) - 1))
D_FF = ((8 * D_MODEL // 3 + 255) // 256) * 256
N_EXPERTS = 8
TOP_K = 2
D_FF_EXPERT = 7 * D_MODEL // 2
MOE_BLOCK = 128
ALPHA = (2 * DEPTH) ** 0.25
BETA = (8 * DEPTH) ** -0.25
LN_EPS = 1e-5
RMS_EPS = 1e-6

kernel_name = 'hgrn2_dilated_swa_deepnorm_moe_step'


def layer_norm(x, g, b):
    xf = x.astype(jnp.float32)
    mu = jnp.mean(xf, axis=-1, keepdims=True)
    var = jnp.mean(jnp.square(xf - mu), axis=-1, keepdims=True)
    return ((xf - mu) * lax.rsqrt(var + LN_EPS) * g + b).astype(x.dtype)


def alibi_slopes():
    h = jnp.arange(1, ATT_HEADS + 1, dtype=jnp.float32)
    return jnp.exp2(-8.0 * h / ATT_HEADS).reshape(N_GROUPS, ATT_HPG)


def hgrn_gates(hq, hf, hi, lb):
    B, T, _ = hq.shape
    hf = hf.astype(jnp.float32)
    logf = jnp.log(lb + (1.0 - lb) * jax.nn.sigmoid(hf))
    k = (1.0 - lb) * jax.nn.sigmoid(-hf)
    heads = lambda a: a.astype(jnp.float32).reshape(B, T, HG_HEADS, -1).transpose(0, 2, 1, 3)
    return heads(hq), heads(k), heads(hi), heads(logf)


def hgrn_chunk(S, xs):
    q, k, v, g = xs
    C = q.shape[2]
    b = jnp.cumsum(g, axis=2)
    causal = jnp.tril(jnp.ones((C, C), dtype=bool))
    diff = b[:, :, :, None, :] - b[:, :, None, :, :]
    decay = jnp.exp(jnp.where(causal[:, :, None], diff, -jnp.inf))
    A = jnp.einsum('bhtk,bhsk,bhtsk->bhts', q, k, decay)
    o = jnp.einsum('bhtk,bhkv->bhtv', q * jnp.exp(b), S) + jnp.einsum('bhts,bhsv->bhtv', A, v)
    b_end = b[:, :, -1]
    S = jnp.exp(b_end)[..., None] * S + jnp.einsum('bhsk,bhsv->bhkv', k * jnp.exp(b_end[:, :, None] - b), v)
    return S, o


def hgrn_recurrence(q, k, v, logf, S0):
    B, H, T, _ = q.shape
    C = min(HG_CHUNK, T)
    nC = -(-T // C)
    pad = nC * C - T
    chunks = lambda a: jnp.pad(a, ((0, 0), (0, 0), (0, pad), (0, 0))).reshape(B, H, nC, C, a.shape[-1]).transpose(2, 0, 1, 3, 4)
    S, o = lax.scan(hgrn_chunk, S0, (chunks(q), chunks(k), chunks(v), chunks(logf)))
    o = o.transpose(1, 2, 0, 3, 4).reshape(B, H, nC * C, HG_DV)[:, :, :T]
    return o.transpose(0, 2, 1, 3), S


def band_attention(q, k, v, slopes, dil, band):
    N, L, H, Dh = q.shape
    nb = -(-L // band)
    pad = nb * band - L
    qb = jnp.pad(q, ((0, 0), (0, pad), (0, 0), (0, 0))).reshape(N, nb, band, H, Dh)
    kp = jnp.pad(k, ((0, 0), (band, pad), (0, 0), (0, 0))).reshape(N, nb + 1, band, H, Dh)
    vp = jnp.pad(v, ((0, 0), (band, pad), (0, 0), (0, 0))).reshape(N, nb + 1, band, H, Dh)
    kw = jnp.concatenate([kp[:, :-1], kp[:, 1:]], axis=2)
    vw = jnp.concatenate([vp[:, :-1], vp[:, 1:]], axis=2)
    dist = (jnp.arange(band)[:, None] + band) - jnp.arange(2 * band)[None, :]
    kabs = jnp.arange(nb)[:, None] * band - band + jnp.arange(2 * band)[None, :]
    mask = ((dist >= 0) & (dist <= band))[None] & (kabs >= 0)[:, None, :]
    penalty = slopes[:, None, None] * (dil * dist).astype(jnp.float32)[None]
    s = jnp.einsum('nbqhd,nbkhd->nbhqk', qb, kw) * ATT_SCALE - penalty
    s = jnp.where(mask[None, :, None], s, -jnp.inf)
    m = jnp.max(s, axis=-1, keepdims=True)
    p = jnp.exp(s - m)
    l = jnp.sum(p, axis=-1)
    o = jnp.einsum('nbhqk,nbkhd->nbqhd', p, vw) / l.transpose(0, 1, 3, 2)[..., None]
    lse = (m[..., 0] + jnp.log(l)).transpose(0, 1, 3, 2)
    o = o.reshape(N, nb * band, H, Dh)[:, :L]
    lse = lse.reshape(N, nb * band, H)[:, :L]
    return o, lse


def dilated_prompt(q, k, v, slopes, dil, band):
    B, T, H, Dh = q.shape
    L = T // dil
    to_sub = lambda a: a.reshape(B, L, dil, H, Dh).transpose(0, 2, 1, 3, 4).reshape(B * dil, L, H, Dh)
    o, lse = band_attention(to_sub(q), to_sub(k), to_sub(v), slopes, dil, band)
    o = o.reshape(B, dil, L, H, Dh).transpose(0, 2, 1, 3, 4).reshape(B, T, H, Dh)
    lse = lse.reshape(B, dil, L, H).transpose(0, 2, 1, 3).reshape(B, T, H)
    return o, lse


def dilated_decode(q, k_new, v_new, kv_cache, slopes, dil, window):
    B, T = q.shape[:2]
    Lw = kv_cache.shape[1]
    kk = jnp.concatenate([kv_cache[:, :, 0], k_new], axis=1)
    vv = jnp.concatenate([kv_cache[:, :, 1], v_new], axis=1)
    steps = jnp.arange(window // dil + 1)
    idx = Lw + jnp.arange(T)[:, None] - dil * steps[None, :]
    valid = idx >= 0
    idx = jnp.maximum(idx, 0)
    kg = kk[:, idx]
    vg = vv[:, idx]
    penalty = slopes[:, None, None] * (dil * steps).astype(jnp.float32)[None, None, :]
    s = jnp.einsum('bthd,btkhd->bhtk', q, kg) * ATT_SCALE - penalty
    s = jnp.where(valid[None, None], s, -jnp.inf)
    m = jnp.max(s, axis=-1, keepdims=True)
    p = jnp.exp(s - m)
    l = jnp.sum(p, axis=-1)
    o = jnp.einsum('bhtk,btkhd->bthd', p, vg) / l.transpose(0, 2, 1)[..., None]
    lse = (m[..., 0] + jnp.log(l)).transpose(0, 2, 1)
    return o, lse


def combine_groups(outs, lses):
    w = jax.nn.softmax(jnp.stack(lses, axis=0), axis=0)
    return jnp.sum(w[..., None] * jnp.stack(outs, axis=0), axis=0)


def mixer_output(o_h, hg, o_att, ga, gb, g_norm, w_a, w_b, w_out):
    B, T, _ = hg.shape
    dt = hg.dtype
    o_h = o_h * lax.rsqrt(jnp.mean(jnp.square(o_h), axis=-1, keepdims=True) + RMS_EPS) * g_norm
    o_h = o_h.reshape(B, T, HG_WIDTH) * jax.nn.silu(hg.astype(jnp.float32))
    y_a = o_h.astype(dt) @ w_a
    y_b = o_att.reshape(B, T, ATT_OUT).astype(dt) @ w_b
    merged = jax.nn.sigmoid(ga) * y_a + jax.nn.sigmoid(gb) * y_b
    return merged @ w_out


def mixer_prompt(x, w_in, lb, g_norm, w_a, w_b, w_out, slopes):
    B, T, _ = x.shape
    hq, hf, hi, hg, aq, ak, av, ga, gb = jnp.split(x @ w_in, SPLIT_POINTS, axis=-1)
    q, k, v, logf = hgrn_gates(hq, hf, hi, lb)
    S0 = jnp.zeros((B, HG_HEADS, HG_DK, HG_DV), jnp.float32)
    o_h, S = hgrn_recurrence(q, k, v, logf, S0)
    aq, ak, av = [a.astype(jnp.float32).reshape(B, T, N_GROUPS, ATT_HPG, ATT_DH) for a in (aq, ak, av)]
    outs, lses, kv_new = [], [], []
    for g, (win, dil) in enumerate(ATT_GROUPS):
        o, lse = dilated_prompt(aq[:, :, g], ak[:, :, g], av[:, :, g], slopes[g], dil, win // dil)
        outs.append(o)
        lses.append(lse)
        keep = min(win, T)
        kv_new.append(jnp.stack([ak[:, T - keep:, g], av[:, T - keep:, g]], axis=2).astype(x.dtype))
    o_att = combine_groups(outs, lses)
    y = mixer_output(o_h, hg, o_att, ga, gb, g_norm, w_a, w_b, w_out)
    return y, S.astype(x.dtype), kv_new


def mixer_sample(x, kv_caches, S0, w_in, lb, g_norm, w_a, w_b, w_out, slopes):
    B, T, _ = x.shape
    hq, hf, hi, hg, aq, ak, av, ga, gb = jnp.split(x @ w_in, SPLIT_POINTS, axis=-1)
    q, k, v, logf = hgrn_gates(hq, hf, hi, lb)
    o_h, S = hgrn_recurrence(q, k, v, logf, S0.astype(jnp.float32))
    aq, ak, av = [a.astype(jnp.float32).reshape(B, T, N_GROUPS, ATT_HPG, ATT_DH) for a in (aq, ak, av)]
    outs, lses, kv_new = [], [], []
    for g, (win, dil) in enumerate(ATT_GROUPS):
        o, lse = dilated_decode(aq[:, :, g], ak[:, :, g], av[:, :, g], kv_caches[g].astype(jnp.float32), slopes[g], dil, win)
        outs.append(o)
        lses.append(lse)
        kv_new.append(jnp.stack([ak[:, :, g], av[:, :, g]], axis=2).astype(x.dtype))
    o_att = combine_groups(outs, lses)
    y = mixer_output(o_h, hg, o_att, ga, gb, g_norm, w_a, w_b, w_out)
    return y, S.astype(x.dtype), kv_new


def swiglu(x, wg, wu, wd):
    return (jax.nn.silu(x @ wg) * (x @ wu)) @ wd


def moe_swiglu(x, w_router, b_router, w_gate, w_up, w_down):
    N, D = x.shape
    logits = (x @ w_router).astype(jnp.float32) + b_router.astype(jnp.float32)
    top_logit, top_e = lax.top_k(logits, TOP_K)
    gate = jax.nn.softmax(top_logit, axis=-1)
    A = N * TOP_K
    blk = min(MOE_BLOCK, A)
    n_blocks = -(-A // blk) + N_EXPERTS
    flat_e = top_e.reshape(A)
    order = jnp.argsort(flat_e)
    sorted_e = flat_e[order]
    counts = jnp.bincount(flat_e, length=N_EXPERTS)
    padded = (counts + blk - 1) // blk * blk
    start = jnp.cumsum(counts) - counts
    pstart = jnp.cumsum(padded) - padded
    dest = pstart[sorted_e] + jnp.arange(A) - start[sorted_e]
    rows = jnp.zeros((n_blocks * blk,), jnp.int32).at[dest].set((order // TOP_K).astype(jnp.int32))
    block_e = jnp.searchsorted(jnp.cumsum(padded), jnp.arange(n_blocks) * blk, side='right')
    block_e = jnp.minimum(block_e, N_EXPERTS - 1)
    xb = x[rows].reshape(n_blocks, blk, D)

    def expert_block(args):
        xblk, e = args
        return (jax.nn.silu(xblk @ w_gate[e]) * (xblk @ w_up[e])) @ w_down[e]

    yb = lax.map(expert_block, (xb, block_e)).reshape(n_blocks * blk, D)
    y_assign = jnp.zeros((A, D), x.dtype).at[order].set(yb[dest])
    return jnp.einsum('nkd,nk->nd', y_assign.reshape(N, TOP_K, D), gate.astype(x.dtype))


def channel_mixer(i, x, ffn_w_gate, ffn_w_up, ffn_w_down, moe_w_router, moe_b_router, moe_w_gate, moe_w_up, moe_w_down):
    j = i // 2
    if i % 2 == 0:
        return swiglu(x, ffn_w_gate[j], ffn_w_up[j], ffn_w_down[j])
    B, T, D = x.shape
    y = moe_swiglu(x.reshape(B * T, D), moe_w_router[j], moe_b_router[j], moe_w_gate[j], moe_w_up[j], moe_w_down[j])
    return y.reshape(B, T, D)


def setup_inputs(seed: int = 0) -> dict:
    key = jax.random.key(seed)
    keys = iter(jax.random.split(key, 32))

    def nrm(shape, scale):
        return jax.random.normal(next(keys), shape, jnp.float32) * scale

    n_dense = (DEPTH + 1) // 2
    n_moe = DEPTH // 2
    d_in = D_MODEL ** -0.5
    col_scale = jnp.concatenate([
        jnp.ones((2 * HG_WIDTH,), jnp.float32), jnp.full((HG_WIDTH,), BETA, jnp.float32),
        jnp.ones((HG_WIDTH + 2 * ATT_WIDTH,), jnp.float32), jnp.full((ATT_WIDTH,), BETA, jnp.float32),
        jnp.ones((2 * D_MODEL,), jnp.float32)])
    return {
        'x_prompt': nrm((BATCH, SEQ, D_MODEL), 1.0),
        'x_sample': nrm((DEC_BATCH, DEC_SEQ, D_MODEL), 1.0),
        'cache_kv_w128': nrm((DEPTH, DEC_BATCH, min(ATT_GROUPS[0][0], PAST_LEN), 2, ATT_HPG, ATT_DH), 1.0),
        'cache_kv_w512': nrm((DEPTH, DEC_BATCH, min(ATT_GROUPS[1][0], PAST_LEN), 2, ATT_HPG, ATT_DH), 1.0),
        'cache_kv_w2048': nrm((DEPTH, DEC_BATCH, min(ATT_GROUPS[2][0], PAST_LEN), 2, ATT_HPG, ATT_DH), 1.0),
        'state_hgrn': nrm((DEPTH, DEC_BATCH, HG_HEADS, HG_DK, HG_DV), 0.5),
        'w_in': nrm((DEPTH, D_MODEL, N_IN), d_in) * col_scale,
        'lb_param': nrm((DEPTH, HG_WIDTH), 0.5),
        'hgrn_norm': 1.0 + nrm((DEPTH, HG_DV), 0.02),
        'w_branch_a': nrm((DEPTH, HG_WIDTH, D_MODEL), HG_WIDTH ** -0.5 * BETA),
        'w_branch_b': nrm((DEPTH, ATT_OUT, D_MODEL), ATT_OUT ** -0.5 * BETA),
        'w_out': nrm((DEPTH, D_MODEL, D_MODEL), d_in * BETA),
        'ln1_g': 1.0 + nrm((DEPTH, D_MODEL), 0.02),
        'ln1_b': nrm((DEPTH, D_MODEL), 0.02),
        'ln2_g': 1.0 + nrm((DEPTH, D_MODEL), 0.02),
        'ln2_b': nrm((DEPTH, D_MODEL), 0.02),
        'ffn_w_gate': nrm((n_dense, D_MODEL, D_FF), d_in * BETA),
        'ffn_w_up': nrm((n_dense, D_MODEL, D_FF), d_in * BETA),
        'ffn_w_down': nrm((n_dense, D_FF, D_MODEL), D_FF ** -0.5 * BETA),
        'moe_w_router': nrm((n_moe, D_MODEL, N_EXPERTS), d_in),
        'moe_b_router': nrm((n_moe, N_EXPERTS), 0.01),
        'moe_w_gate': nrm((n_moe, N_EXPERTS, D_MODEL, D_FF_EXPERT), d_in * BETA),
        'moe_w_up': nrm((n_moe, N_EXPERTS, D_MODEL, D_FF_EXPERT), d_in * BETA),
        'moe_w_down': nrm((n_moe, N_EXPERTS, D_FF_EXPERT, D_MODEL), D_FF_EXPERT ** -0.5 * BETA),
    }


def reference(x_prompt, x_sample, cache_kv_w128, cache_kv_w512, cache_kv_w2048, state_hgrn,
              w_in, lb_param, hgrn_norm, w_branch_a, w_branch_b, w_out,
              ln1_g, ln1_b, ln2_g, ln2_b, ffn_w_gate, ffn_w_up, ffn_w_down,
              moe_w_router, moe_b_router, moe_w_gate, moe_w_up, moe_w_down):
    slopes = alibi_slopes()
    lb_cum = jnp.cumsum(jax.nn.softmax(lb_param.astype(jnp.float32), axis=0), axis=0)
    lb_all = lb_cum - lb_cum[0]
    caches = (cache_kv_w128, cache_kv_w512, cache_kv_w2048)
    xp, xs = x_prompt, x_sample
    kv_p = [[] for _ in ATT_GROUPS]
    kv_s = [[] for _ in ATT_GROUPS]
    hg_p, hg_s = [], []
    for i in range(DEPTH):
        mix_w = (w_in[i], lb_all[i], hgrn_norm[i], w_branch_a[i], w_branch_b[i], w_out[i], slopes)
        yp, Sp, kvp = mixer_prompt(xp, *mix_w)
        ys, Ss, kvs = mixer_sample(xs, tuple(c[i] for c in caches), state_hgrn[i], *mix_w)
        for g in range(N_GROUPS):
            kv_p[g].append(kvp[g])
            kv_s[g].append(kvs[g])
        hg_p.append(Sp)
        hg_s.append(Ss)
        xp = layer_norm(ALPHA * xp + yp, ln1_g[i], ln1_b[i])
        xs = layer_norm(ALPHA * xs + ys, ln1_g[i], ln1_b[i])
        ffn_w = (ffn_w_gate, ffn_w_up, ffn_w_down, moe_w_router, moe_b_router, moe_w_gate, moe_w_up, moe_w_down)
        xp = layer_norm(ALPHA * xp + channel_mixer(i, xp, *ffn_w), ln2_g[i], ln2_b[i])
        xs = layer_norm(ALPHA * xs + channel_mixer(i, xs, *ffn_w), ln2_g[i], ln2_b[i])
    return (xp, xs,
            jnp.stack(kv_p[0], axis=0), jnp.stack(kv_s[0], axis=0),
            jnp.stack(kv_p[1], axis=0), jnp.stack(kv_s[1], axis=0),
            jnp.stack(kv_p[2], axis=0), jnp.stack(kv_s[2], axis=0),
            jnp.stack(hg_p, axis=0), jnp.stack(hg_s, axis=0))
```

```python
import functools

import numpy as np
import jax
import jax.numpy as jnp
from jax import lax
from jax.experimental import pallas as pl
from jax.experimental.pallas import tpu as pltpu

F32 = jnp.float32
BF16 = jnp.bfloat16

D_MODEL = 2048
DEPTH = 4
HG_HEADS = 8
HG_D = 128
HG_WIDTH = HG_HEADS * HG_D
ATT_GROUPS = ((128, 1), (512, 4), (2048, 16))
N_GROUPS = 3
ATT_HPG = 4
ATT_DH = 128
ATT_HEADS = N_GROUPS * ATT_HPG
ATT_GW = ATT_HPG * ATT_DH
ATT_SCALE = ATT_DH ** -0.5
BAND = 128
HQ_OFF, HF_OFF, HI_OFF, HGATE_OFF = 0, 1024, 2048, 3072
AQ_OFF, AK_OFF, AV_OFF = 4096, 5632, 7168
GA_OFF, GB_OFF = 8704, 10752
N_IN = 12800
N_EXPERTS = 8
TOP_K = 2
ALPHA = (2 * DEPTH) ** 0.25
LN_EPS = 1e-5
RMS_EPS = 1e-6
NEG = -1e30
LANES = 128

VMEM_LIMIT = 56 * 1024 * 1024


def _params(sem):
    return pltpu.CompilerParams(dimension_semantics=sem, vmem_limit_bytes=VMEM_LIMIT)


def _dot(a, b):
    return jnp.dot(a, b, preferred_element_type=F32)


def _dot_nt(a, b):
    return lax.dot_general(a, b, (((1,), (1,)), ((), ())), preferred_element_type=F32)


def _dot_tn(a, b):
    return lax.dot_general(a, b, (((0,), (0,)), ((), ())), preferred_element_type=F32)


def _sigmoid(x):
    return 1.0 / (1.0 + jnp.exp(-x))


def _layer_norm(y, g, b):
    mu = jnp.mean(y, axis=-1, keepdims=True)
    d = y - mu
    var = jnp.mean(d * d, axis=-1, keepdims=True)
    return d * lax.rsqrt(var + LN_EPS) * g + b


def _mm_body(x_ref, w_ref, o_ref, wb_ref):
    @pl.when(pl.program_id(1) == 0)
    def _():
        wb_ref[...] = w_ref[...].astype(BF16)

    o_ref[...] = _dot(x_ref[...], wb_ref[...]).astype(o_ref.dtype)


def matmul(x, w, layer, tm, tn, out_dtype):
    M, K = x.shape
    N = w.shape[-1]
    return pl.pallas_call(
        _mm_body,
        grid=(N // tn, M // tm),
        in_specs=[pl.BlockSpec((tm, K), lambda j, i: (i, 0)),
                  pl.BlockSpec((None, K, tn), lambda j, i: (layer, 0, j))],
        out_specs=pl.BlockSpec((tm, tn), lambda j, i: (i, j)),
        out_shape=jax.ShapeDtypeStruct((M, N), out_dtype),
        scratch_shapes=[pltpu.VMEM((K, tn), BF16)],
        compiler_params=_params(("arbitrary", "arbitrary")),
        name="mm",
    )(x, w)


def _merge_body(oh_ref, oa_ref, ga_ref, gb_ref, wa_ref, wb_ref, o_ref, wab_ref, wbb_ref):
    @pl.when(pl.program_id(1) == 0)
    def _():
        wab_ref[...] = wa_ref[...].astype(BF16)
        wbb_ref[...] = wb_ref[...].astype(BF16)

    ya = _dot(oh_ref[...], wab_ref[...])
    yb = _dot(oa_ref[...], wbb_ref[...])
    o_ref[...] = (_sigmoid(ga_ref[...]) * ya + _sigmoid(gb_ref[...]) * yb).astype(o_ref.dtype)


def branch_merge(oh, oa, z, w_a, w_b, layer, tm):
    M = oh.shape[0]
    tn = 512
    ga0, gb0 = GA_OFF // tn, GB_OFF // tn
    return pl.pallas_call(
        _merge_body,
        grid=(D_MODEL // tn, M // tm),
        in_specs=[pl.BlockSpec((tm, HG_WIDTH), lambda j, i: (i, 0)),
                  pl.BlockSpec((tm, ATT_GW), lambda j, i: (i, 0)),
                  pl.BlockSpec((tm, tn), lambda j, i: (i, ga0 + j)),
                  pl.BlockSpec((tm, tn), lambda j, i: (i, gb0 + j)),
                  pl.BlockSpec((None, HG_WIDTH, tn), lambda j, i: (layer, 0, j)),
                  pl.BlockSpec((None, ATT_GW, tn), lambda j, i: (layer, 0, j))],
        out_specs=pl.BlockSpec((tm, tn), lambda j, i: (i, j)),
        out_shape=jax.ShapeDtypeStruct((M, D_MODEL), BF16),
        scratch_shapes=[pltpu.VMEM((HG_WIDTH, tn), BF16), pltpu.VMEM((ATT_GW, tn), BF16)],
        compiler_params=_params(("arbitrary", "arbitrary")),
        name="branch_merge",
    )(oh, oa, z, z, w_a, w_b)


def _mm_ln_body(a_ref, w_ref, x_ref, g_ref, b_ref, of_ref, ob_ref, acc_ref, *, nk):
    k = pl.program_id(1)

    @pl.when(k == 0)
    def _():
        acc_ref[...] = jnp.zeros_like(acc_ref)

    acc_ref[...] += _dot(a_ref[...], w_ref[...].astype(BF16))

    @pl.when(k == nk - 1)
    def _():
        out = _layer_norm(ALPHA * x_ref[...] + acc_ref[...], g_ref[...], b_ref[...])
        of_ref[...] = out
        ob_ref[...] = out.astype(BF16)


def matmul_ln(a, w, layer, x, g, b, ln_layer, tm, tk):
    M, K = a.shape
    nk = K // tk
    g3 = g.reshape(DEPTH, 1, D_MODEL)
    b3 = b.reshape(DEPTH, 1, D_MODEL)
    return pl.pallas_call(
        functools.partial(_mm_ln_body, nk=nk),
        grid=(M // tm, nk),
        in_specs=[pl.BlockSpec((tm, tk), lambda i, k: (i, k)),
                  pl.BlockSpec((None, tk, D_MODEL), lambda i, k: (layer, k, 0)),
                  pl.BlockSpec((tm, D_MODEL), lambda i, k: (i, 0)),
                  pl.BlockSpec((None, 1, D_MODEL), lambda i, k: (ln_layer, 0, 0)),
                  pl.BlockSpec((None, 1, D_MODEL), lambda i, k: (ln_layer, 0, 0))],
        out_specs=[pl.BlockSpec((tm, D_MODEL), lambda i, k: (i, 0)),
                   pl.BlockSpec((tm, D_MODEL), lambda i, k: (i, 0))],
        out_shape=[jax.ShapeDtypeStruct((M, D_MODEL), F32),
                   jax.ShapeDtypeStruct((M, D_MODEL), BF16)],
        scratch_shapes=[pltpu.VMEM((tm, D_MODEL), F32)],
        compiler_params=_params(("arbitrary", "arbitrary")),
        name="mm_ln",
    )(a, w, x, g3, b3)


def _add_ln_body(x_ref, y0_ref, y1_ref, g_ref, b_ref, of_ref, ob_ref):
    out = _layer_norm(ALPHA * x_ref[...] + (y0_ref[...] + y1_ref[...]), g_ref[...], b_ref[...])
    of_ref[...] = out
    ob_ref[...] = out.astype(BF16)


def add_ln(x, y0, y1, g, b, ln_layer, tm):
    M = x.shape[0]
    g3 = g.reshape(DEPTH, 1, D_MODEL)
    b3 = b.reshape(DEPTH, 1, D_MODEL)
    row = pl.BlockSpec((tm, D_MODEL), lambda i: (i, 0))
    vec = pl.BlockSpec((None, 1, D_MODEL), lambda i: (ln_layer, 0, 0))
    return pl.pallas_call(
        _add_ln_body,
        grid=(M // tm,),
        in_specs=[row, row, row, vec, vec],
        out_specs=[row, row],
        out_shape=[jax.ShapeDtypeStruct((M, D_MODEL), F32),
                   jax.ShapeDtypeStruct((M, D_MODEL), BF16)],
        compiler_params=_params(("arbitrary",)),
        name="add_ln",
    )(x, y0, y1, g3, b3)


def _swiglu_up_body(x_ref, wg_ref, wu_ref, o_ref, wgb_ref, wub_ref):
    @pl.when(pl.program_id(1) == 0)
    def _():
        wgb_ref[...] = wg_ref[...].astype(BF16)
        wub_ref[...] = wu_ref[...].astype(BF16)

    x = x_ref[...]
    a = _dot(x, wgb_ref[...])
    u = _dot(x, wub_ref[...])
    o_ref[...] = (a * _sigmoid(a) * u).astype(o_ref.dtype)


def swiglu_up(x, wg, wu, layer, tm, tn):
    M, K = x.shape
    N = wg.shape[-1]
    wspec = pl.BlockSpec((None, K, tn), lambda j, i: (layer, 0, j))
    return pl.pallas_call(
        _swiglu_up_body,
        grid=(N // tn, M // tm),
        in_specs=[pl.BlockSpec((tm, K), lambda j, i: (i, 0)), wspec, wspec],
        out_specs=pl.BlockSpec((tm, tn), lambda j, i: (i, j)),
        out_shape=jax.ShapeDtypeStruct((M, N), BF16),
        scratch_shapes=[pltpu.VMEM((K, tn), BF16), pltpu.VMEM((K, tn), BF16)],
        compiler_params=_params(("arbitrary", "arbitrary")),
        name="swiglu_up",
    )(x, wg, wu)


def _moe_up_body(be_ref, nb_ref, x_ref, wg_ref, wu_ref, o_ref, wgb_ref, wub_ref):
    i = pl.program_id(1)
    prev = be_ref[jnp.maximum(i - 1, 0)]

    @pl.when((i == 0) | (be_ref[i] != prev))
    def _():
        wgb_ref[...] = wg_ref[...].astype(BF16)
        wub_ref[...] = wu_ref[...].astype(BF16)

    @pl.when(i < nb_ref[0])
    def _():
        x = x_ref[...]
        a = _dot(x, wgb_ref[...])
        u = _dot(x, wub_ref[...])
        o_ref[...] = (a * _sigmoid(a) * u).astype(o_ref.dtype)

    @pl.when(i >= nb_ref[0])
    def _():
        o_ref[...] = jnp.zeros_like(o_ref)


def moe_up(block_e, n_used, xs, wg, wu, layer, tm, tn):
    P, K = xs.shape
    N = wg.shape[-1]
    wspec = pl.BlockSpec((None, None, K, tn), lambda j, i, be, nb: (layer, be[i], 0, j))
    return pl.pallas_call(
        _moe_up_body,
        grid_spec=pltpu.PrefetchScalarGridSpec(
            num_scalar_prefetch=2,
            grid=(N // tn, P // tm),
            in_specs=[pl.BlockSpec((tm, K), lambda j, i, be, nb: (i, 0)), wspec, wspec],
            out_specs=pl.BlockSpec((tm, tn), lambda j, i, be, nb: (i, j)),
            scratch_shapes=[pltpu.VMEM((K, tn), BF16), pltpu.VMEM((K, tn), BF16)]),
        out_shape=jax.ShapeDtypeStruct((P, N), BF16),
        compiler_params=_params(("arbitrary", "arbitrary")),
        name="moe_up",
    )(block_e, n_used, xs, wg, wu)


def _moe_down_body(be_ref, nb_ref, h_ref, w_ref, gate_ref, o_ref, wb_ref):
    i = pl.program_id(1)
    prev = be_ref[jnp.maximum(i - 1, 0)]

    @pl.when((i == 0) | (be_ref[i] != prev))
    def _():
        wb_ref[...] = w_ref[...].astype(BF16)

    @pl.when(i < nb_ref[0])
    def _():
        o_ref[...] = _dot(h_ref[...], wb_ref[...]) * gate_ref[...]

    @pl.when(i >= nb_ref[0])
    def _():
        o_ref[...] = jnp.zeros_like(o_ref)


def moe_down(block_e, n_used, h, wd, gate_rows, layer, tm, tn):
    P, K = h.shape
    N = wd.shape[-1]
    return pl.pallas_call(
        _moe_down_body,
        grid_spec=pltpu.PrefetchScalarGridSpec(
            num_scalar_prefetch=2,
            grid=(N // tn, P // tm),
            in_specs=[pl.BlockSpec((tm, K), lambda j, i, be, nb: (i, 0)),
                      pl.BlockSpec((None, None, K, tn), lambda j, i, be, nb: (layer, be[i], 0, j)),
                      pl.BlockSpec((tm, 1), lambda j, i, be, nb: (i, 0))],
            out_specs=pl.BlockSpec((tm, tn), lambda j, i, be, nb: (i, j)),
            scratch_shapes=[pltpu.VMEM((K, tn), BF16)]),
        out_shape=jax.ShapeDtypeStruct((P, N), F32),
        compiler_params=_params(("arbitrary", "arbitrary")),
        name="moe_down",
    )(block_e, n_used, h, wd, gate_rows)


def _router_body(x_ref, w_ref, b_ref, e_ref, g_ref):
    x = x_ref[...]
    w = w_ref[...]
    xh = x.astype(BF16)
    xl = (x - xh.astype(F32)).astype(BF16)
    wh = w.astype(BF16)
    wl = (w - wh.astype(F32)).astype(BF16)
    logits = _dot(xh, wh) + (_dot(xh, wl) + _dot(xl, wh)) + b_ref[...]
    lane = lax.broadcasted_iota(jnp.int32, logits.shape, 1).astype(F32)
    m1 = jnp.max(logits, axis=-1, keepdims=True)
    i1 = jnp.min(jnp.where(logits == m1, lane, float(LANES)), axis=-1, keepdims=True)
    rest = jnp.where(lane == i1, 2.0 * NEG, logits)
    m2 = jnp.max(rest, axis=-1, keepdims=True)
    i2 = jnp.min(jnp.where(rest == m2, lane, float(LANES)), axis=-1, keepdims=True)
    g1 = 1.0 / (1.0 + jnp.exp(m2 - m1))
    two = lax.broadcasted_iota(jnp.int32, (logits.shape[0], TOP_K), 1)
    e_ref[...] = jnp.where(two == 0, i1, i2).astype(jnp.int32)
    g_ref[...] = jnp.where(two == 0, g1, 1.0 - g1)


def router(x, w_router, b_router, j, tm):
    M = x.shape[0]
    n_moe = w_router.shape[0]
    w_pad = jnp.pad(w_router.astype(F32), ((0, 0), (0, 0), (0, LANES - N_EXPERTS)))
    b3 = jnp.pad(b_router.astype(F32).reshape(n_moe, 1, N_EXPERTS), ((0, 0), (0, 0), (0, LANES - N_EXPERTS)),
                 constant_values=NEG)
    return pl.pallas_call(
        _router_body,
        grid=(M // tm,),
        in_specs=[pl.BlockSpec((tm, D_MODEL), lambda i: (i, 0)),
                  pl.BlockSpec((None, D_MODEL, LANES), lambda i: (j, 0, 0)),
                  pl.BlockSpec((None, 1, LANES), lambda i: (j, 0, 0))],
        out_specs=[pl.BlockSpec((tm, TOP_K), lambda i: (i, 0)),
                   pl.BlockSpec((tm, TOP_K), lambda i: (i, 0))],
        out_shape=[jax.ShapeDtypeStruct((M, TOP_K), jnp.int32),
                   jax.ShapeDtypeStruct((M, TOP_K), F32)],
        compiler_params=_params(("arbitrary",)),
        name="router",
    )(x, w_pad, b3)


HG_CHUNK = 128
HG_LEVELS = 7
_ROW_B = HG_LEVELS * HG_CHUNK
_ROW_U = _ROW_B + HG_CHUNK


def _hgrn_tables():
    C = HG_CHUNK
    t = np.arange(C)[:, None]
    u = np.arange(C)[None, :]
    sel, low, seg = [], [], []
    for lvl in range(HG_LEVELS):
        w = C >> (lvl + 1)
        ref = (t // (2 * w)) * (2 * w) + w - 1
        lower = (t % (2 * w)) >= w
        sel.append(((u > np.minimum(t, ref)) & (u <= np.maximum(t, ref))).astype(np.float32))
        low.append(np.broadcast_to(lower, (C, C)).astype(np.float32))
        seg.append(((t // (2 * w)) == (u // (2 * w))).astype(np.float32))
    sel.append((u <= t).astype(np.float32))
    sel.append((u > t).astype(np.float32))
    return (jnp.asarray(np.concatenate(sel, axis=0), BF16),
            jnp.asarray(np.stack(low)), jnp.asarray(np.stack(seg)))


def _hgrn_body(sel_ref, low_ref, seg_ref, hq_ref, hf_ref, hi_ref, hg_ref, lb_ref, gn_ref,
               o_ref, s_ref, st_ref, *, nc):
    c = pl.program_id(2)

    @pl.when(c == 0)
    def _():
        st_ref[...] = jnp.zeros_like(st_ref)

    C = HG_CHUNK
    q = hq_ref[...]
    hf = hf_ref[...]
    v = hi_ref[...]
    lb = lb_ref[...]
    g = jnp.log(lb + (1.0 - lb) * _sigmoid(hf))
    kk = (1.0 - lb) * _sigmoid(-hf)
    g_hi = g.astype(BF16)
    r1 = g - g_hi.astype(F32)
    g_mid = r1.astype(BF16)
    g_lo = (r1 - g_mid.astype(F32)).astype(BF16)
    sel = sel_ref[...]
    ex = jnp.exp(_dot(sel, g_hi) + (_dot(sel, g_mid) + _dot(sel, g_lo)))
    eb = ex[_ROW_B:_ROW_B + C]
    eu = ex[_ROW_U:_ROW_U + C]
    eb_end = eb[C - 1:C, :]

    st = st_ref[...]
    o = _dot_nt((q * eb).astype(BF16), st.astype(BF16))
    a = jnp.zeros((C, C), F32)
    for lvl in range(HG_LEVELS):
        e = ex[lvl * C:(lvl + 1) * C]
        low = low_ref[lvl]
        qt = (q * e * low).astype(BF16)
        kt = (kk * e * (1.0 - low)).astype(BF16)
        a = a + _dot_nt(qt, kt) * seg_ref[lvl]
    o = o + _dot(a.astype(BF16), v.astype(BF16))
    o = o + jnp.sum(q * kk, axis=-1, keepdims=True) * v

    st_new = st * eb_end + _dot_tn(v.astype(BF16), (kk * eu).astype(BF16))
    st_ref[...] = st_new

    hg = hg_ref[...]
    on = o * lax.rsqrt(jnp.mean(o * o, axis=-1, keepdims=True) + RMS_EPS) * gn_ref[...]
    o_ref[...] = (on * (hg * _sigmoid(hg))).astype(o_ref.dtype)

    @pl.when(c == nc - 1)
    def _():
        s_ref[...] = st_new.T


def hgrn_prompt(z, lb, gnorm, layer, B, T):
    C = HG_CHUNK
    nc = T // C
    sel, low, seg = _hgrn_tables()
    lb3 = lb.reshape(DEPTH, 1, HG_WIDTH)
    gn3 = gnorm.reshape(DEPTH, 1, HG_D)

    def col(off):
        return pl.BlockSpec((C, HG_D), lambda b, h, c: (b * nc + c, off // HG_D + h))

    return pl.pallas_call(
        functools.partial(_hgrn_body, nc=nc),
        grid=(B, HG_HEADS, nc),
        in_specs=[pl.BlockSpec(sel.shape, lambda b, h, c: (0, 0)),
                  pl.BlockSpec(low.shape, lambda b, h, c: (0, 0, 0)),
                  pl.BlockSpec(seg.shape, lambda b, h, c: (0, 0, 0)),
                  col(HQ_OFF), col(HF_OFF), col(HI_OFF), col(HGATE_OFF),
                  pl.BlockSpec((None, 1, HG_D), lambda b, h, c: (layer, 0, h)),
                  pl.BlockSpec((None, 1, HG_D), lambda b, h, c: (layer, 0, 0))],
        out_specs=[pl.BlockSpec((C, HG_D), lambda b, h, c: (b * nc + c, h)),
                   pl.BlockSpec((None, None, HG_D, HG_D), lambda b, h, c: (b, h, 0, 0))],
        out_shape=[jax.ShapeDtypeStruct((B * T, HG_WIDTH), BF16),
                   jax.ShapeDtypeStruct((B, HG_HEADS, HG_D, HG_D), F32)],
        scratch_shapes=[pltpu.VMEM((HG_D, HG_D), F32)],
        compiler_params=_params(("parallel", "parallel", "arbitrary")),
        name="hgrn_prompt",
    )(sel, low, seg, z, z, z, z, lb3, gn3)


def _hgrn_step_body(s0_ref, qc_ref, fc_ref, vr_ref, gr_ref, lbc_ref, gn_ref, o_ref, s_ref):
    for h in range(HG_HEADS):
        s0 = s0_ref[h]
        hf = fc_ref[h]
        lb = lbc_ref[h]
        f = lb + (1.0 - lb) * _sigmoid(hf)
        kk = (1.0 - lb) * _sigmoid(-hf)
        v = vr_ref[h]
        s_new = f * s0 + kk * v
        s_ref[h] = s_new
        o = jnp.sum(qc_ref[h] * s_new, axis=0, keepdims=True)
        hg = gr_ref[h]
        on = o * lax.rsqrt(jnp.mean(o * o, axis=-1, keepdims=True) + RMS_EPS) * gn_ref[...]
        o_ref[h] = (on * (hg * _sigmoid(hg))).astype(o_ref.dtype)


def hgrn_step(zs, state, lb, gnorm, layer):
    Bs = zs.shape[0]
    heads = lambda off: zs[:, off:off + HG_WIDTH].reshape(Bs, HG_HEADS, HG_D)
    qc = heads(HQ_OFF)[..., None]
    fc = heads(HF_OFF)[..., None]
    vr = heads(HI_OFF)[:, :, None, :]
    gr = heads(HGATE_OFF)[:, :, None, :]
    lbc = lb.reshape(DEPTH, HG_HEADS, HG_D, 1)
    gn3 = gnorm.reshape(DEPTH, 1, HG_D)
    colspec = pl.BlockSpec((None, HG_HEADS, HG_D, 1), lambda b: (b, 0, 0, 0))
    rowspec = pl.BlockSpec((None, HG_HEADS, 1, HG_D), lambda b: (b, 0, 0, 0))
    o, s = pl.pallas_call(
        _hgrn_step_body,
        grid=(Bs,),
        in_specs=[pl.BlockSpec((None, None, HG_HEADS, HG_D, HG_D), lambda b: (layer, b, 0, 0, 0)),
                  colspec, colspec, rowspec, rowspec,
                  pl.BlockSpec((None, HG_HEADS, HG_D, 1), lambda b: (layer, 0, 0, 0)),
                  pl.BlockSpec((None, 1, HG_D), lambda b: (layer, 0, 0))],
        out_specs=[rowspec,
                   pl.BlockSpec((None, HG_HEADS, HG_D, HG_D), lambda b: (b, 0, 0, 0))],
        out_shape=[jax.ShapeDtypeStruct((Bs, HG_HEADS, 1, HG_D), BF16),
                   jax.ShapeDtypeStruct((Bs, HG_HEADS, HG_D, HG_D), F32)],
        compiler_params=_params(("arbitrary",)),
        name="hgrn_step",
    )(state, qc, fc, vr, gr, lbc, gn3)
    return o.reshape(Bs, HG_WIDTH), s


def _slope(g, h):
    return float(2.0 ** (-8.0 * (g * ATT_HPG + h + 1) / ATT_HEADS))


def _attn_body(q_ref, kc_ref, vc_ref, kp_ref, vp_ref, o_ref, lse_ref, *, group, dil, nqb):
    t = pl.program_id(2)
    row = lax.broadcasted_iota(jnp.int32, (BAND, BAND), 0)
    colj = lax.broadcasted_iota(jnp.int32, (BAND, BAND), 1)
    dist_c = (row - colj).astype(F32)
    dist_p = dist_c + float(BAND)
    for h in range(ATT_HPG):
        pen = _slope(group, h) * dil
        hs = slice(h * ATT_DH, (h + 1) * ATT_DH)
        for qb in range(nqb):
            rs = slice(qb * BAND, (qb + 1) * BAND)
            q = q_ref[rs, hs].astype(BF16)
            kc = kc_ref[rs, hs].astype(BF16)
            vc = vc_ref[rs, hs].astype(BF16)
            if qb == 0:
                kp = kp_ref[:, hs].astype(BF16)
                vp = vp_ref[:, hs].astype(BF16)
                off_p = jnp.where(t > 0, 0.0, NEG)
            else:
                ps = slice((qb - 1) * BAND, qb * BAND)
                kp = kc_ref[ps, hs].astype(BF16)
                vp = vc_ref[ps, hs].astype(BF16)
                off_p = 0.0
            s_c = jnp.where(colj <= row, _dot_nt(q, kc) * ATT_SCALE - pen * dist_c, NEG)
            s_p = jnp.where(colj >= row, _dot_nt(q, kp) * ATT_SCALE - pen * dist_p, NEG) + off_p
            m = jnp.maximum(jnp.max(s_c, axis=-1, keepdims=True), jnp.max(s_p, axis=-1, keepdims=True))
            p_c = jnp.exp(s_c - m)
            p_p = jnp.exp(s_p - m)
            l = jnp.sum(p_c, axis=-1, keepdims=True) + jnp.sum(p_p, axis=-1, keepdims=True)
            acc = _dot(p_c.astype(BF16), vc) + _dot(p_p.astype(BF16), vp)
            o_ref[rs, hs] = acc / l
            lse_ref[rs, hs] = jnp.broadcast_to(m + jnp.log(l), (BAND, ATT_DH))


def attn_group(z, group, B, T):
    _, dil = ATT_GROUPS[group]
    L = T // dil
    lq = min(L, 512)
    nqb = lq // BAND
    nt = L // lq
    zr = z.reshape(B * L, dil * N_IN)
    cpr = N_IN // ATT_GW

    def cur(off):
        c0 = off // ATT_GW + group
        return pl.BlockSpec((lq, ATT_GW), lambda b, r, t: (b * nt + t, r * cpr + c0))

    def prev(off):
        c0 = off // ATT_GW + group
        return pl.BlockSpec(
            (BAND, ATT_GW),
            lambda b, r, t: (jnp.maximum(b * (L // BAND) + t * nqb - 1, 0), r * cpr + c0))

    ospec = pl.BlockSpec((lq, ATT_GW), lambda b, r, t: (b * nt + t, r))
    o, lse = pl.pallas_call(
        functools.partial(_attn_body, group=group, dil=dil, nqb=nqb),
        grid=(B, dil, nt),
        in_specs=[cur(AQ_OFF), cur(AK_OFF), cur(AV_OFF), prev(AK_OFF), prev(AV_OFF)],
        out_specs=[ospec, ospec],
        out_shape=[jax.ShapeDtypeStruct((B * L, dil * ATT_GW), F32),
                   jax.ShapeDtypeStruct((B * L, dil * ATT_GW), F32)],
        compiler_params=_params(("parallel", "parallel", "arbitrary")),
        name=f"attn_g{group}",
    )(zr, zr, zr, zr, zr)
    return o.reshape(B * T, ATT_GW), lse.reshape(B * T, ATT_GW)


def _attn_merge_body(o0, o1, o2, l0, l1, l2, out_ref):
    a0, a1, a2 = l0[...], l1[...], l2[...]
    m = jnp.maximum(jnp.maximum(a0, a1), a2)
    w0, w1, w2 = jnp.exp(a0 - m), jnp.exp(a1 - m), jnp.exp(a2 - m)
    out = (w0 * o0[...] + w1 * o1[...] + w2 * o2[...]) / (w0 + w1 + w2)
    out_ref[...] = out.astype(out_ref.dtype)


def attn_merge(outs, lses, tm):
    M = outs[0].shape[0]
    spec = pl.BlockSpec((tm, ATT_GW), lambda i: (i, 0))
    return pl.pallas_call(
        _attn_merge_body,
        grid=(M // tm,),
        in_specs=[spec] * 6,
        out_specs=spec,
        out_shape=jax.ShapeDtypeStruct((M, ATT_GW), BF16),
        compiler_params=_params(("arbitrary",)),
        name="attn_merge",
    )(*outs, *lses)


def _attn_step_body(z_ref, c0_ref, c1_ref, c2_ref, o_ref):
    caches = (c0_ref, c1_ref, c2_ref)
    steps = (float(BAND) - lax.broadcasted_iota(jnp.int32, (BAND, 1), 0).astype(F32))
    for h in range(ATT_HPG):
        outs, lses = [], []
        for g in range(N_GROUPS):
            _, dil = ATT_GROUPS[g]
            pen = _slope(g, h) * dil
            c0 = (g * ATT_HPG + h) * ATT_DH
            q = z_ref[:, AQ_OFF + c0:AQ_OFF + c0 + ATT_DH]
            kn = z_ref[:, AK_OFF + c0:AK_OFF + c0 + ATT_DH]
            vn = z_ref[:, AV_OFF + c0:AV_OFF + c0 + ATT_DH]
            kc = caches[g][:, h * ATT_DH:(h + 1) * ATT_DH]
            vc = caches[g][:, ATT_GW + h * ATT_DH:ATT_GW + (h + 1) * ATT_DH]
            s_c = jnp.sum(kc * q, axis=-1, keepdims=True) * ATT_SCALE - pen * steps
            s_n = jnp.sum(kn * q, axis=-1, keepdims=True) * ATT_SCALE
            m = jnp.maximum(jnp.max(s_c, axis=0, keepdims=True), s_n)
            p_c = jnp.exp(s_c - m)
            p_n = jnp.exp(s_n - m)
            l = jnp.sum(p_c, axis=0, keepdims=True) + p_n
            o = (jnp.sum(p_c * vc, axis=0, keepdims=True) + p_n * vn) / l
            outs.append(o)
            lses.append(m + jnp.log(l))
        mm = jnp.maximum(jnp.maximum(lses[0], lses[1]), lses[2])
        ws = [jnp.exp(x - mm) for x in lses]
        out = (ws[0] * outs[0] + ws[1] * outs[1] + ws[2] * outs[2]) / (ws[0] + ws[1] + ws[2])
        o_ref[:, h * ATT_DH:(h + 1) * ATT_DH] = out.astype(o_ref.dtype)


def attn_step(zs, caches, layer):
    Bs = zs.shape[0]
    z3 = zs.reshape(Bs, 1, N_IN)
    cache_specs, cache_args = [], []
    for g, (win, dil) in enumerate(ATT_GROUPS):
        c = caches[g]
        lw = c.shape[2]
        assert lw == win and lw // dil == BAND
        cache_args.append(c.reshape(DEPTH, Bs, BAND, dil * 2 * ATT_GW))
        cache_specs.append(pl.BlockSpec((None, None, BAND, 2 * ATT_GW), lambda b: (layer, b, 0, 0)))
    o = pl.pallas_call(
        _attn_step_body,
        grid=(Bs,),
        in_specs=[pl.BlockSpec((None, 1, N_IN), lambda b: (b, 0, 0))] + cache_specs,
        out_specs=pl.BlockSpec((None, 1, ATT_GW), lambda b: (b, 0, 0)),
        out_shape=jax.ShapeDtypeStruct((Bs, 1, ATT_GW), BF16),
        compiler_params=_params(("arbitrary",)),
        name="attn_step",
    )(z3, *cache_args)
    return o.reshape(Bs, ATT_GW)


MOE_TM = 256


def _moe_tables(top_e, gate, tm):
    N = top_e.shape[0]
    A = N * TOP_K
    n_blocks = -(-A // tm) + N_EXPERTS
    P = n_blocks * tm
    flat_e = top_e.reshape(A)
    onehot = (flat_e[:, None] == jnp.arange(N_EXPERTS)[None, :]).astype(jnp.int32)
    rank = jnp.take_along_axis(jnp.cumsum(onehot, axis=0), flat_e[:, None], axis=1)[:, 0] - 1
    counts = jnp.sum(onehot, axis=0)
    padded = (counts + tm - 1) // tm * tm
    pend = jnp.cumsum(padded)
    pstart = pend - padded
    dest = (pstart[flat_e] + rank).astype(jnp.int32)
    rows = jnp.zeros((P,), jnp.int32).at[dest].set(jnp.arange(A, dtype=jnp.int32) // TOP_K)
    gate_rows = jnp.zeros((P,), F32).at[dest].set(gate.reshape(A))
    n_used = (pend[-1] // tm).astype(jnp.int32)
    blk_start = jnp.arange(n_blocks, dtype=jnp.int32) * tm
    block_e = jnp.searchsorted(pend, jnp.minimum(blk_start, pend[-1] - 1), side='right')
    block_e = jnp.minimum(block_e, N_EXPERTS - 1).astype(jnp.int32)
    return rows, gate_rows.reshape(P, 1), dest.reshape(N, TOP_K), block_e, n_used.reshape(1)


def _mixer(xp_b, xs_b, xp_f, xs_f, i, B, T, caches, state_hgrn, w_in, lb_all, hgrn_norm,
           w_branch_a, w_branch_b, w_out, ln1_g, ln1_b):
    Np = B * T
    Bs = xs_b.shape[0]
    zp = matmul(xp_b, w_in, i, 1024, 1280, F32)
    zs = matmul(xs_b, w_in, i, Bs, 1280, F32)

    oh_p, S_p = hgrn_prompt(zp, lb_all, hgrn_norm, i, B, T)
    outs, lses = [], []
    for g in range(N_GROUPS):
        o, lse = attn_group(zp, g, B, T)
        outs.append(o)
        lses.append(lse)
    oa_p = attn_merge(outs, lses, 1024)
    mp = branch_merge(oh_p, oa_p, zp, w_branch_a, w_branch_b, i, 1024)
    hp_f, hp_b = matmul_ln(mp, w_out, i, xp_f, ln1_g, ln1_b, i, 512, 1024)

    oh_s, S_s = hgrn_step(zs, state_hgrn, lb_all, hgrn_norm, i)
    oa_s = attn_step(zs, caches, i)
    ms = branch_merge(oh_s, oa_s, zs, w_branch_a, w_branch_b, i, Bs)
    hs_f, hs_b = matmul_ln(ms, w_out, i, xs_f, ln1_g, ln1_b, i, Bs, 1024)

    zp3 = zp.reshape(B, T, N_IN)
    kv_p, kv_s = [], []
    for g, (win, _) in enumerate(ATT_GROUPS):
        keep = min(win, T)
        ko, vo = AK_OFF + g * ATT_GW, AV_OFF + g * ATT_GW
        kp = zp3[:, T - keep:, ko:ko + ATT_GW].reshape(B, keep, ATT_HPG, ATT_DH)
        vp = zp3[:, T - keep:, vo:vo + ATT_GW].reshape(B, keep, ATT_HPG, ATT_DH)
        kv_p.append(jnp.stack([kp, vp], axis=2))
        ks = zs[:, ko:ko + ATT_GW].reshape(Bs, 1, ATT_HPG, ATT_DH)
        vs = zs[:, vo:vo + ATT_GW].reshape(Bs, 1, ATT_HPG, ATT_DH)
        kv_s.append(jnp.stack([ks, vs], axis=2))
    return (hp_f, hp_b, hs_f, hs_b, S_p, S_s, kv_p, kv_s)


def _dense_ffn(h_f, h_b, j, ffn_w_gate, ffn_w_up, ffn_w_down, ln2_g, ln2_b, i, tm_up, tm_dn):
    u = swiglu_up(h_b, ffn_w_gate, ffn_w_up, j, tm_up, 512)
    return matmul_ln(u, ffn_w_down, j, h_f, ln2_g, ln2_b, i, tm_dn, 512)


def _moe_ffn(hp_f, hp_b, hs_f, hs_b, j, moe_w_router, moe_b_router, moe_w_gate, moe_w_up, moe_w_down,
             ln2_g, ln2_b, i):
    Np, Bs = hp_f.shape[0], hs_f.shape[0]
    ep, gp = router(hp_f, moe_w_router, moe_b_router, j, 1024)
    es, gs = router(hs_f, moe_w_router, moe_b_router, j, Bs)
    top_e = jnp.concatenate([ep, es], axis=0)
    gate = jnp.concatenate([gp, gs], axis=0)
    rows, gate_rows, dest, block_e, n_used = _moe_tables(top_e, gate, MOE_TM)
    h_b = jnp.concatenate([hp_b, hs_b], axis=0)
    xs = h_b[rows]
    u = moe_up(block_e, n_used, xs, moe_w_gate, moe_w_up, j, MOE_TM, 512)
    yb = moe_down(block_e, n_used, u, moe_w_down, gate_rows, j, MOE_TM, 512)
    y0 = yb[dest[:, 0]]
    y1 = yb[dest[:, 1]]
    xp_f, xp_b = add_ln(hp_f, y0[:Np], y1[:Np], ln2_g, ln2_b, i, 256)
    xs_f, xs_b = add_ln(hs_f, y0[Np:], y1[Np:], ln2_g, ln2_b, i, Bs)
    return xp_f, xp_b, xs_f, xs_b


def kernel(x_prompt, x_sample, cache_kv_w128, cache_kv_w512, cache_kv_w2048, state_hgrn,
           w_in, lb_param, hgrn_norm, w_branch_a, w_branch_b, w_out,
           ln1_g, ln1_b, ln2_g, ln2_b, ffn_w_gate, ffn_w_up, ffn_w_down,
           moe_w_router, moe_b_router, moe_w_gate, moe_w_up, moe_w_down):
    B, T, _ = x_prompt.shape
    Bs = x_sample.shape[0]
    lb_cum = jnp.cumsum(jax.nn.softmax(lb_param.astype(F32), axis=0), axis=0)
    lb_all = lb_cum - lb_cum[0]
    caches = (cache_kv_w128, cache_kv_w512, cache_kv_w2048)
    xp_f = x_prompt.reshape(B * T, D_MODEL)
    xs_f = x_sample.reshape(Bs, D_MODEL)
    xp_b = xp_f.astype(BF16)
    xs_b = xs_f.astype(BF16)
    kv_p = [[] for _ in ATT_GROUPS]
    kv_s = [[] for _ in ATT_GROUPS]
    hg_p, hg_s = [], []
    for i in range(DEPTH):
        hp_f, hp_b, hs_f, hs_b, S_p, S_s, kvp, kvs = _mixer(
            xp_b, xs_b, xp_f, xs_f, i, B, T, caches, state_hgrn, w_in, lb_all, hgrn_norm,
            w_branch_a, w_branch_b, w_out, ln1_g, ln1_b)
        for g in range(N_GROUPS):
            kv_p[g].append(kvp[g])
            kv_s[g].append(kvs[g])
        hg_p.append(S_p)
        hg_s.append(S_s)
        j = i // 2
        if i % 2 == 0:
            xp_f, xp_b = _dense_ffn(hp_f, hp_b, j, ffn_w_gate, ffn_w_up, ffn_w_down, ln2_g, ln2_b, i, 1024, 512)
            xs_f, xs_b = _dense_ffn(hs_f, hs_b, j, ffn_w_gate, ffn_w_up, ffn_w_down, ln2_g, ln2_b, i, Bs, Bs)
        else:
            xp_f, xp_b, xs_f, xs_b = _moe_ffn(hp_f, hp_b, hs_f, hs_b, j, moe_w_router, moe_b_router,
                                              moe_w_gate, moe_w_up, moe_w_down, ln2_g, ln2_b, i)
    return (xp_f.reshape(B, T, D_MODEL), xs_f.reshape(Bs, 1, D_MODEL),
            jnp.stack(kv_p[0], axis=0), jnp.stack(kv_s[0], axis=0),
            jnp.stack(kv_p[1], axis=0), jnp.stack(kv_s[1], axis=0),
            jnp.stack(kv_p[2], axis=0), jnp.stack(kv_s[2], axis=0),
            jnp.stack(hg_p, axis=0), jnp.stack(hg_s, axis=0))
```

```python
import functools

import numpy as np
import jax
import jax.numpy as jnp
from jax import lax
from jax.experimental import pallas as pl
from jax.experimental.pallas import tpu as pltpu

F32 = jnp.float32
BF16 = jnp.bfloat16

D_MODEL = 2048
DEPTH = 4
HG_HEADS = 8
HG_D = 128
HG_WIDTH = HG_HEADS * HG_D
ATT_GROUPS = ((128, 1), (512, 4), (2048, 16))
N_GROUPS = 3
ATT_HPG = 4
ATT_DH = 128
ATT_HEADS = N_GROUPS * ATT_HPG
ATT_GW = ATT_HPG * ATT_DH
ATT_SCALE = ATT_DH ** -0.5
BAND = 128
HQ_OFF, HF_OFF, HI_OFF, HGATE_OFF = 0, 1024, 2048, 3072
AQ_OFF, AK_OFF, AV_OFF = 4096, 5632, 7168
GA_OFF, GB_OFF = 8704, 10752
N_IN = 12800
N_EXPERTS = 8
TOP_K = 2
ALPHA = (2 * DEPTH) ** 0.25
LN_EPS = 1e-5
RMS_EPS = 1e-6
NEG = -1e30
LANES = 128
LOG2E = 1.4426950408889634

VMEM_LIMIT = 56 * 1024 * 1024


def _params(sem):
    return pltpu.CompilerParams(dimension_semantics=sem, vmem_limit_bytes=VMEM_LIMIT)


def _dot(a, b):
    return jnp.dot(a, b, preferred_element_type=F32)


def _dot_nt(a, b):
    return lax.dot_general(a, b, (((1,), (1,)), ((), ())), preferred_element_type=F32)


def _dot_tn(a, b):
    return lax.dot_general(a, b, (((0,), (0,)), ((), ())), preferred_element_type=F32)


def _sigmoid(x):
    return 1.0 / (1.0 + jnp.exp(-x))


def _layer_norm(y, g, b):
    mu = jnp.mean(y, axis=-1, keepdims=True)
    d = y - mu
    var = jnp.mean(d * d, axis=-1, keepdims=True)
    return d * lax.rsqrt(var + LN_EPS) * g + b


def _mm_body(x_ref, w_ref, o_ref, wb_ref):
    @pl.when(pl.program_id(1) == 0)
    def _():
        wb_ref[...] = w_ref[...].astype(BF16)

    o_ref[...] = _dot(x_ref[...], wb_ref[...]).astype(o_ref.dtype)


def matmul(x, w, layer, tm, tn, out_dtype):
    M, K = x.shape
    N = w.shape[-1]
    return pl.pallas_call(
        _mm_body,
        grid=(N // tn, M // tm),
        in_specs=[pl.BlockSpec((tm, K), lambda j, i: (i, 0)),
                  pl.BlockSpec((None, K, tn), lambda j, i: (layer, 0, j))],
        out_specs=pl.BlockSpec((tm, tn), lambda j, i: (i, j)),
        out_shape=jax.ShapeDtypeStruct((M, N), out_dtype),
        scratch_shapes=[pltpu.VMEM((K, tn), BF16)],
        compiler_params=_params(("arbitrary", "arbitrary")),
        name="mm",
    )(x, w)


def _merge_body(oh_ref, oa_ref, ga_ref, gb_ref, wa_ref, wb_ref, o_ref, wab_ref, wbb_ref):
    @pl.when(pl.program_id(1) == 0)
    def _():
        wab_ref[...] = wa_ref[...].astype(BF16)
        wbb_ref[...] = wb_ref[...].astype(BF16)

    ya = _dot(oh_ref[...], wab_ref[...])
    yb = _dot(oa_ref[...], wbb_ref[...])
    o_ref[...] = (_sigmoid(ga_ref[...]) * ya + _sigmoid(gb_ref[...]) * yb).astype(o_ref.dtype)


def branch_merge(oh, oa, z, w_a, w_b, layer, tm):
    M = oh.shape[0]
    tn = 512
    ga0, gb0 = GA_OFF // tn, GB_OFF // tn
    return pl.pallas_call(
        _merge_body,
        grid=(D_MODEL // tn, M // tm),
        in_specs=[pl.BlockSpec((tm, HG_WIDTH), lambda j, i: (i, 0)),
                  pl.BlockSpec((tm, ATT_GW), lambda j, i: (i, 0)),
                  pl.BlockSpec((tm, tn), lambda j, i: (i, ga0 + j)),
                  pl.BlockSpec((tm, tn), lambda j, i: (i, gb0 + j)),
                  pl.BlockSpec((None, HG_WIDTH, tn), lambda j, i: (layer, 0, j)),
                  pl.BlockSpec((None, ATT_GW, tn), lambda j, i: (layer, 0, j))],
        out_specs=pl.BlockSpec((tm, tn), lambda j, i: (i, j)),
        out_shape=jax.ShapeDtypeStruct((M, D_MODEL), BF16),
        scratch_shapes=[pltpu.VMEM((HG_WIDTH, tn), BF16), pltpu.VMEM((ATT_GW, tn), BF16)],
        compiler_params=_params(("arbitrary", "arbitrary")),
        name="branch_merge",
    )(oh, oa, z, z, w_a, w_b)


def _mm_ln_body(a_ref, w_ref, x_ref, g_ref, b_ref, of_ref, ob_ref, acc_ref, *, nk):
    k = pl.program_id(1)

    @pl.when(k == 0)
    def _():
        acc_ref[...] = jnp.zeros_like(acc_ref)

    acc_ref[...] += _dot(a_ref[...], w_ref[...].astype(BF16))

    @pl.when(k == nk - 1)
    def _():
        out = _layer_norm(ALPHA * x_ref[...] + acc_ref[...], g_ref[...], b_ref[...])
        of_ref[...] = out
        ob_ref[...] = out.astype(BF16)


def matmul_ln(a, w, layer, x, g, b, ln_layer, tm, tk):
    M, K = a.shape
    nk = K // tk
    g3 = g.reshape(DEPTH, 1, D_MODEL)
    b3 = b.reshape(DEPTH, 1, D_MODEL)
    return pl.pallas_call(
        functools.partial(_mm_ln_body, nk=nk),
        grid=(M // tm, nk),
        in_specs=[pl.BlockSpec((tm, tk), lambda i, k: (i, k)),
                  pl.BlockSpec((None, tk, D_MODEL), lambda i, k: (layer, k, 0)),
                  pl.BlockSpec((tm, D_MODEL), lambda i, k: (i, 0)),
                  pl.BlockSpec((None, 1, D_MODEL), lambda i, k: (ln_layer, 0, 0)),
                  pl.BlockSpec((None, 1, D_MODEL), lambda i, k: (ln_layer, 0, 0))],
        out_specs=[pl.BlockSpec((tm, D_MODEL), lambda i, k: (i, 0)),
                   pl.BlockSpec((tm, D_MODEL), lambda i, k: (i, 0))],
        out_shape=[jax.ShapeDtypeStruct((M, D_MODEL), F32),
                   jax.ShapeDtypeStruct((M, D_MODEL), BF16)],
        scratch_shapes=[pltpu.VMEM((tm, D_MODEL), F32)],
        compiler_params=_params(("arbitrary", "arbitrary")),
        name="mm_ln",
    )(a, w, x, g3, b3)


def _add_ln_body(x_ref, y0_ref, y1_ref, gate_ref, g_ref, b_ref, of_ref, ob_ref):
    gate = gate_ref[...]
    y = gate[:, 0:1] * y0_ref[...] + gate[:, 1:2] * y1_ref[...]
    out = _layer_norm(ALPHA * x_ref[...] + y, g_ref[...], b_ref[...])
    of_ref[...] = out
    ob_ref[...] = out.astype(BF16)


def add_ln(x, y0, y1, gate, g, b, ln_layer, tm):
    M = x.shape[0]
    g3 = g.reshape(DEPTH, 1, D_MODEL)
    b3 = b.reshape(DEPTH, 1, D_MODEL)
    row = pl.BlockSpec((tm, D_MODEL), lambda i: (i, 0))
    vec = pl.BlockSpec((None, 1, D_MODEL), lambda i: (ln_layer, 0, 0))
    return pl.pallas_call(
        _add_ln_body,
        grid=(M // tm,),
        in_specs=[row, row, row, pl.BlockSpec((tm, TOP_K), lambda i: (i, 0)), vec, vec],
        out_specs=[row, row],
        out_shape=[jax.ShapeDtypeStruct((M, D_MODEL), F32),
                   jax.ShapeDtypeStruct((M, D_MODEL), BF16)],
        compiler_params=_params(("arbitrary",)),
        name="add_ln",
    )(x, y0, y1, gate, g3, b3)


def _swiglu_up_body(x_ref, wg_ref, wu_ref, o_ref, wgb_ref, wub_ref):
    @pl.when(pl.program_id(1) == 0)
    def _():
        wgb_ref[...] = wg_ref[...].astype(BF16)
        wub_ref[...] = wu_ref[...].astype(BF16)

    x = x_ref[...]
    a = _dot(x, wgb_ref[...])
    u = _dot(x, wub_ref[...])
    o_ref[...] = (a * _sigmoid(a) * u).astype(o_ref.dtype)


def swiglu_up(x, wg, wu, layer, tm, tn):
    M, K = x.shape
    N = wg.shape[-1]
    wspec = pl.BlockSpec((None, K, tn), lambda j, i: (layer, 0, j))
    return pl.pallas_call(
        _swiglu_up_body,
        grid=(N // tn, M // tm),
        in_specs=[pl.BlockSpec((tm, K), lambda j, i: (i, 0)), wspec, wspec],
        out_specs=pl.BlockSpec((tm, tn), lambda j, i: (i, j)),
        out_shape=jax.ShapeDtypeStruct((M, N), BF16),
        scratch_shapes=[pltpu.VMEM((K, tn), BF16), pltpu.VMEM((K, tn), BF16)],
        compiler_params=_params(("arbitrary", "arbitrary")),
        name="swiglu_up",
    )(x, wg, wu)


MOE_TM = 256
MOE_GROUP = 10
MOE_TF = 256


def _moe_body(se_ref, sb_ref, sn_ref, sf_ref, x_hbm, wg_ref, wu_ref, wd_ref, out_hbm,
              xbuf, acc, wgb, wub, wdb, sem_in, sem_out, *, nj):
    s = pl.program_id(0)
    j = pl.program_id(1)
    nb = sn_ref[s]
    fill = sf_ref[s] == 1
    active = (nb > 0) & (sf_ref[s] == 0)
    row0 = sb_ref[s] * MOE_TM

    def x_copy(c):
        return pltpu.make_async_copy(x_hbm.at[pl.ds(row0 + c * MOE_TM, MOE_TM)], xbuf.at[c], sem_in)

    def out_copy(c):
        return pltpu.make_async_copy(acc.at[c], out_hbm.at[pl.ds(row0 + c * MOE_TM, MOE_TM)], sem_out)

    def each_block(fn):
        def body(c, carry):
            fn(c)
            return carry
        lax.fori_loop(0, nb, body, 0)

    @pl.when((j == 0) & active)
    def _():
        each_block(lambda c: x_copy(c).start())
        each_block(lambda c: x_copy(c).wait())

    def compute(c):
        x = xbuf[c]
        a = _dot(x, wgb[...])
        u = _dot(x, wub[...])
        y = _dot((a * _sigmoid(a) * u).astype(BF16), wdb[...])

        @pl.when(j == 0)
        def _():
            acc[c] = y

        @pl.when(j > 0)
        def _():
            acc[c] += y

    @pl.when(active)
    def _():
        wgb[...] = wg_ref[...].astype(BF16)
        wub[...] = wu_ref[...].astype(BF16)
        wdb[...] = wd_ref[...].astype(BF16)
        each_block(compute)

    def zero(c):
        acc[c] = jnp.zeros((MOE_TM, acc.shape[-1]), F32)

    @pl.when((j == nj - 1) & fill)
    def _():
        each_block(zero)

    @pl.when((j == nj - 1) & (nb > 0))
    def _():
        each_block(lambda c: out_copy(c).start())
        each_block(lambda c: out_copy(c).wait())


def moe_experts(tile_e, tile_b, tile_n, tile_f, xs, wg, wu, wd, layer):
    P, D = xs.shape
    F = wg.shape[-1]
    nj = F // MOE_TF
    S = tile_e.shape[0]

    def jj(s, j, sn, sf):
        return jnp.where((sn[s] > 0) & (sf[s] == 0), j, nj - 1)

    up = pl.BlockSpec((None, None, D, MOE_TF),
                      lambda s, j, se, sb, sn, sf: (layer, se[s], 0, jj(s, j, sn, sf)))
    down = pl.BlockSpec((None, None, MOE_TF, D),
                        lambda s, j, se, sb, sn, sf: (layer, se[s], jj(s, j, sn, sf), 0))
    return pl.pallas_call(
        functools.partial(_moe_body, nj=nj),
        grid_spec=pltpu.PrefetchScalarGridSpec(
            num_scalar_prefetch=4,
            grid=(S, nj),
            in_specs=[pl.BlockSpec(memory_space=pl.ANY), up, up, down],
            out_specs=pl.BlockSpec(memory_space=pl.ANY),
            scratch_shapes=[pltpu.VMEM((MOE_GROUP, MOE_TM, D), BF16),
                            pltpu.VMEM((MOE_GROUP, MOE_TM, D), F32),
                            pltpu.VMEM((D, MOE_TF), BF16), pltpu.VMEM((D, MOE_TF), BF16),
                            pltpu.VMEM((MOE_TF, D), BF16),
                            pltpu.SemaphoreType.DMA(()), pltpu.SemaphoreType.DMA(())]),
        out_shape=jax.ShapeDtypeStruct((P, D), F32),
        compiler_params=_params(("arbitrary", "arbitrary")),
        name="moe_experts",
    )(tile_e, tile_b, tile_n, tile_f, xs, wg, wu, wd)


def _router_body(x_ref, w_ref, b_ref, e_ref, g_ref):
    x = x_ref[...]
    w = w_ref[...]
    xh = x.astype(BF16)
    xl = (x - xh.astype(F32)).astype(BF16)
    wh = w.astype(BF16)
    wl = (w - wh.astype(F32)).astype(BF16)
    logits = _dot(xh, wh) + (_dot(xh, wl) + _dot(xl, wh)) + b_ref[...]
    lane = lax.broadcasted_iota(jnp.int32, logits.shape, 1).astype(F32)
    m1 = jnp.max(logits, axis=-1, keepdims=True)
    i1 = jnp.min(jnp.where(logits == m1, lane, float(LANES)), axis=-1, keepdims=True)
    rest = jnp.where(lane == i1, 2.0 * NEG, logits)
    m2 = jnp.max(rest, axis=-1, keepdims=True)
    i2 = jnp.min(jnp.where(rest == m2, lane, float(LANES)), axis=-1, keepdims=True)
    g1 = 1.0 / (1.0 + jnp.exp(m2 - m1))
    two = lax.broadcasted_iota(jnp.int32, (logits.shape[0], TOP_K), 1)
    e_ref[...] = jnp.where(two == 0, i1, i2).astype(jnp.int32)
    g_ref[...] = jnp.where(two == 0, g1, 1.0 - g1)


def router(x, w_router, b_router, j, tm):
    M = x.shape[0]
    n_moe = w_router.shape[0]
    w_pad = jnp.pad(w_router.astype(F32), ((0, 0), (0, 0), (0, LANES - N_EXPERTS)))
    b3 = jnp.pad(b_router.astype(F32).reshape(n_moe, 1, N_EXPERTS), ((0, 0), (0, 0), (0, LANES - N_EXPERTS)),
                 constant_values=NEG)
    return pl.pallas_call(
        _router_body,
        grid=(M // tm,),
        in_specs=[pl.BlockSpec((tm, D_MODEL), lambda i: (i, 0)),
                  pl.BlockSpec((None, D_MODEL, LANES), lambda i: (j, 0, 0)),
                  pl.BlockSpec((None, 1, LANES), lambda i: (j, 0, 0))],
        out_specs=[pl.BlockSpec((tm, TOP_K), lambda i: (i, 0)),
                   pl.BlockSpec((tm, TOP_K), lambda i: (i, 0))],
        out_shape=[jax.ShapeDtypeStruct((M, TOP_K), jnp.int32),
                   jax.ShapeDtypeStruct((M, TOP_K), F32)],
        compiler_params=_params(("arbitrary",)),
        name="router",
    )(x, w_pad, b3)


HG_CHUNK = 128
HG_HPS = 8
HG_LEVELS = 7
_ROW_B = HG_LEVELS * HG_CHUNK
_ROW_U = _ROW_B + HG_CHUNK


def _hgrn_tables():
    C = HG_CHUNK
    t = np.arange(C)[:, None]
    u = np.arange(C)[None, :]
    sel, low, seg = [], [], []
    for lvl in range(HG_LEVELS):
        w = C >> (lvl + 1)
        ref = (t // (2 * w)) * (2 * w) + w - 1
        lower = (t % (2 * w)) >= w
        sel.append(((u > np.minimum(t, ref)) & (u <= np.maximum(t, ref))).astype(np.float32))
        low.append(np.broadcast_to(lower, (C, HG_HPS * HG_D)).astype(np.float32))
        seg.append(((t // (2 * w)) == (u // (2 * w))).astype(np.float32))
    sel.append((u <= t).astype(np.float32))
    sel.append((u > t).astype(np.float32))
    return (jnp.asarray(np.concatenate(sel, axis=0), BF16),
            jnp.asarray(np.stack(low)), jnp.asarray(np.stack(seg)))


def _hgrn_body(sel_ref, low_ref, seg_ref, hq_ref, hf_ref, hi_ref, hg_ref, lb_ref, gn_ref,
               o_ref, s_ref, st_ref, *, nc):
    c = pl.program_id(2)

    @pl.when(c == 0)
    def _():
        st_ref[...] = jnp.zeros_like(st_ref)

    C = HG_CHUNK
    heads = [slice(hh * HG_D, (hh + 1) * HG_D) for hh in range(HG_HPS)]
    q = hq_ref[...]
    hf = hf_ref[...]
    v = hi_ref[...]
    lb = lb_ref[...]
    g = jnp.log(lb + (1.0 - lb) * _sigmoid(hf)) * LOG2E
    kk = (1.0 - lb) * _sigmoid(-hf)
    g_hi = g.astype(BF16)
    r1 = g - g_hi.astype(F32)
    g_mid = r1.astype(BF16)
    g_lo = (r1 - g_mid.astype(F32)).astype(BF16)
    sel = sel_ref[...]
    ex = jnp.exp2(_dot(sel, g_hi) + (_dot(sel, g_mid) + _dot(sel, g_lo)))
    eb = ex[_ROW_B:_ROW_B + C]
    eu = ex[_ROW_U:_ROW_U + C]
    eb_end = eb[C - 1:C, :]
    v_b = v.astype(BF16)

    qe = (q * eb).astype(BF16)
    sts = [st_ref[hh] for hh in range(HG_HPS)]
    o = [_dot_nt(qe[:, hs], sts[hh].astype(BF16)) for hh, hs in enumerate(heads)]
    a = [jnp.zeros((C, C), F32) for _ in heads]
    for lvl in range(HG_LEVELS):
        e = ex[lvl * C:(lvl + 1) * C]
        low = low_ref[lvl]
        qt = (q * e * low).astype(BF16)
        kt = (kk * e * (1.0 - low)).astype(BF16)
        seg = seg_ref[lvl]
        for hh, hs in enumerate(heads):
            a[hh] = a[hh] + _dot_nt(qt[:, hs], kt[:, hs]) * seg
    kd = (kk * eu).astype(BF16)
    qk = q * kk
    for hh, hs in enumerate(heads):
        o[hh] = o[hh] + _dot(a[hh].astype(BF16), v_b[:, hs])
        o[hh] = o[hh] + jnp.sum(qk[:, hs], axis=-1, keepdims=True) * v[:, hs]
        st_ref[hh] = sts[hh] * eb_end[:, hs] + _dot_tn(v_b[:, hs], kd[:, hs])

    hg = hg_ref[...]
    gate = hg * _sigmoid(hg)
    for hh, hs in enumerate(heads):
        on = o[hh] * lax.rsqrt(jnp.mean(o[hh] * o[hh], axis=-1, keepdims=True) + RMS_EPS) * gn_ref[...]
        o_ref[:, hs] = (on * gate[:, hs]).astype(o_ref.dtype)

    @pl.when(c == nc - 1)
    def _():
        for hh in range(HG_HPS):
            s_ref[hh] = st_ref[hh].T


def hgrn_prompt(z, lb, gnorm, layer, B, T):
    C = HG_CHUNK
    nc = T // C
    W = HG_HPS * HG_D
    sel, low, seg = _hgrn_tables()
    lb3 = lb.reshape(DEPTH, 1, HG_WIDTH)
    gn3 = gnorm.reshape(DEPTH, 1, HG_D)

    def col(off):
        return pl.BlockSpec((C, W), lambda b, h, c: (b * nc + c, off // W + h))

    return pl.pallas_call(
        functools.partial(_hgrn_body, nc=nc),
        grid=(B, HG_HEADS // HG_HPS, nc),
        in_specs=[pl.BlockSpec(sel.shape, lambda b, h, c: (0, 0)),
                  pl.BlockSpec(low.shape, lambda b, h, c: (0, 0, 0)),
                  pl.BlockSpec(seg.shape, lambda b, h, c: (0, 0, 0)),
                  col(HQ_OFF), col(HF_OFF), col(HI_OFF), col(HGATE_OFF),
                  pl.BlockSpec((None, 1, W), lambda b, h, c: (layer, 0, h)),
                  pl.BlockSpec((None, 1, HG_D), lambda b, h, c: (layer, 0, 0))],
        out_specs=[pl.BlockSpec((C, W), lambda b, h, c: (b * nc + c, h)),
                   pl.BlockSpec((None, HG_HPS, HG_D, HG_D), lambda b, h, c: (b, h, 0, 0))],
        out_shape=[jax.ShapeDtypeStruct((B * T, HG_WIDTH), BF16),
                   jax.ShapeDtypeStruct((B, HG_HEADS, HG_D, HG_D), F32)],
        scratch_shapes=[pltpu.VMEM((HG_HPS, HG_D, HG_D), F32)],
        compiler_params=_params(("parallel", "parallel", "arbitrary")),
        name="hgrn_prompt",
    )(sel, low, seg, z, z, z, z, lb3, gn3)


def _hgrn_step_body(s0_ref, qc_ref, fc_ref, vr_ref, gr_ref, lbc_ref, gn_ref, o_ref, s_ref):
    for h in range(HG_HEADS):
        s0 = s0_ref[h]
        hf = fc_ref[h]
        lb = lbc_ref[h]
        f = lb + (1.0 - lb) * _sigmoid(hf)
        kk = (1.0 - lb) * _sigmoid(-hf)
        v = vr_ref[h]
        s_new = f * s0 + kk * v
        s_ref[h] = s_new
        o = jnp.sum(qc_ref[h] * s_new, axis=0, keepdims=True)
        hg = gr_ref[h]
        on = o * lax.rsqrt(jnp.mean(o * o, axis=-1, keepdims=True) + RMS_EPS) * gn_ref[...]
        o_ref[h] = (on * (hg * _sigmoid(hg))).astype(o_ref.dtype)


def hgrn_step(zs, state, lb, gnorm, layer):
    Bs = zs.shape[0]
    heads = lambda off: zs[:, off:off + HG_WIDTH].reshape(Bs, HG_HEADS, HG_D)
    qc = heads(HQ_OFF)[..., None]
    fc = heads(HF_OFF)[..., None]
    vr = heads(HI_OFF)[:, :, None, :]
    gr = heads(HGATE_OFF)[:, :, None, :]
    lbc = lb.reshape(DEPTH, HG_HEADS, HG_D, 1)
    gn3 = gnorm.reshape(DEPTH, 1, HG_D)
    colspec = pl.BlockSpec((None, HG_HEADS, HG_D, 1), lambda b: (b, 0, 0, 0))
    rowspec = pl.BlockSpec((None, HG_HEADS, 1, HG_D), lambda b: (b, 0, 0, 0))
    o, s = pl.pallas_call(
        _hgrn_step_body,
        grid=(Bs,),
        in_specs=[pl.BlockSpec((None, None, HG_HEADS, HG_D, HG_D), lambda b: (layer, b, 0, 0, 0)),
                  colspec, colspec, rowspec, rowspec,
                  pl.BlockSpec((None, HG_HEADS, HG_D, 1), lambda b: (layer, 0, 0, 0)),
                  pl.BlockSpec((None, 1, HG_D), lambda b: (layer, 0, 0))],
        out_specs=[rowspec,
                   pl.BlockSpec((None, HG_HEADS, HG_D, HG_D), lambda b: (b, 0, 0, 0))],
        out_shape=[jax.ShapeDtypeStruct((Bs, HG_HEADS, 1, HG_D), BF16),
                   jax.ShapeDtypeStruct((Bs, HG_HEADS, HG_D, HG_D), F32)],
        compiler_params=_params(("arbitrary",)),
        name="hgrn_step",
    )(state, qc, fc, vr, gr, lbc, gn3)
    return o.reshape(Bs, HG_WIDTH), s


def _slope(g, h):
    return float(2.0 ** (-8.0 * (g * ATT_HPG + h + 1) / ATT_HEADS))


def _band_rows(start, dil):
    return pl.ds(start, BAND, stride=dil) if dil > 1 else pl.ds(start, BAND)


def _attn_unit(q_ref, k_ref, v_ref, o_scr, l_scr, start, prev_start, prev_off, dil, pen, has_prev):
    row = lax.broadcasted_iota(jnp.int32, (BAND, BAND), 0)
    colj = lax.broadcasted_iota(jnp.int32, (BAND, BAND), 1)
    dist_c = (row - colj).astype(F32)
    rows_c = _band_rows(start, dil)
    q = q_ref[rows_c, :].astype(BF16)
    kc = k_ref[rows_c, :].astype(BF16)
    vc = v_ref[rows_c, :].astype(BF16)
    s_c = jnp.where(colj <= row, _dot_nt(q, kc) * ATT_SCALE - pen * dist_c, NEG)
    m = jnp.max(s_c, axis=-1, keepdims=True)
    if has_prev:
        rows_p = _band_rows(prev_start, dil)
        kp = k_ref[rows_p, :].astype(BF16)
        vp = v_ref[rows_p, :].astype(BF16)
        s_p = jnp.where(colj >= row, _dot_nt(q, kp) * ATT_SCALE - pen * (dist_c + float(BAND)), NEG) + prev_off
        m = jnp.maximum(m, jnp.max(s_p, axis=-1, keepdims=True))
    p_c = jnp.exp(s_c - m)
    l = jnp.sum(p_c, axis=-1, keepdims=True)
    acc = _dot(p_c.astype(BF16), vc)
    if has_prev:
        p_p = jnp.exp(s_p - m)
        l = l + jnp.sum(p_p, axis=-1, keepdims=True)
        acc = acc + _dot(p_p.astype(BF16), vp)
    o_scr[rows_c, :] = acc / l
    l_scr[rows_c, :] = jnp.broadcast_to(m + jnp.log(l), (BAND, ATT_DH))


def _attn_body(slope_ref, *refs, T):
    qkv = refs[:3 * N_GROUPS]
    out_ref, o_scr, l_scr = refs[3 * N_GROUPS:]
    h = pl.program_id(1)
    for g, (_, dil) in enumerate(ATT_GROUPS):
        q_ref, k_ref, v_ref = qkv[3 * g:3 * g + 3]
        pen = slope_ref[g * ATT_HPG + h] * float(dil)
        nqb = T // dil // BAND

        def unit(idx, carry, q_ref=q_ref, k_ref=k_ref, v_ref=v_ref, g=g, dil=dil, pen=pen, nqb=nqb):
            r = idx // nqb
            qb = idx - r * nqb
            start = r + qb * (BAND * dil)
            prev_start = jnp.maximum(start - BAND * dil, 0)
            if dil == 1:
                start = pl.multiple_of(start, BAND)
                prev_start = pl.multiple_of(prev_start, BAND)
            prev_off = jnp.where(qb > 0, 0.0, NEG)
            _attn_unit(q_ref, k_ref, v_ref, o_scr.at[g], l_scr.at[g], start, prev_start, prev_off,
                       dil, pen, has_prev=nqb > 1)
            return carry

        lax.fori_loop(0, dil * nqb, unit, 0, unroll=2)

    def merge(i, carry):
        rs = pl.ds(pl.multiple_of(i * BAND, BAND), BAND)
        a0, a1, a2 = l_scr[0, rs, :], l_scr[1, rs, :], l_scr[2, rs, :]
        m = jnp.maximum(jnp.maximum(a0, a1), a2)
        w0, w1, w2 = jnp.exp(a0 - m), jnp.exp(a1 - m), jnp.exp(a2 - m)
        out = (w0 * o_scr[0, rs, :] + w1 * o_scr[1, rs, :] + w2 * o_scr[2, rs, :]) / (w0 + w1 + w2)
        out_ref[rs, :] = out.astype(out_ref.dtype)
        return carry

    lax.fori_loop(0, T // BAND, merge, 0)


def attn_prompt(z, B, T):
    slopes = jnp.asarray([_slope(g, h) for g in range(N_GROUPS) for h in range(ATT_HPG)], F32)

    def col(off, g):
        c0 = off // ATT_DH + g * ATT_HPG
        return pl.BlockSpec((T, ATT_DH), lambda b, h: (b, c0 + h))

    in_specs = [pl.BlockSpec(memory_space=pltpu.SMEM)]
    for g in range(N_GROUPS):
        in_specs += [col(AQ_OFF, g), col(AK_OFF, g), col(AV_OFF, g)]
    return pl.pallas_call(
        functools.partial(_attn_body, T=T),
        grid=(B, ATT_HPG),
        in_specs=in_specs,
        out_specs=pl.BlockSpec((T, ATT_DH), lambda b, h: (b, h)),
        out_shape=jax.ShapeDtypeStruct((B * T, ATT_GW), BF16),
        scratch_shapes=[pltpu.VMEM((N_GROUPS, T, ATT_DH), F32), pltpu.VMEM((N_GROUPS, T, ATT_DH), F32)],
        compiler_params=_params(("parallel", "arbitrary")),
        name="attn_prompt",
    )(slopes, *([z] * (3 * N_GROUPS)))


def _attn_step_body(z_ref, c0_ref, c1_ref, c2_ref, o_ref):
    caches = (c0_ref, c1_ref, c2_ref)
    steps = (float(BAND) - lax.broadcasted_iota(jnp.int32, (BAND, 1), 0).astype(F32))
    for h in range(ATT_HPG):
        outs, lses = [], []
        for g in range(N_GROUPS):
            _, dil = ATT_GROUPS[g]
            pen = _slope(g, h) * dil
            c0 = (g * ATT_HPG + h) * ATT_DH
            q = z_ref[:, AQ_OFF + c0:AQ_OFF + c0 + ATT_DH]
            kn = z_ref[:, AK_OFF + c0:AK_OFF + c0 + ATT_DH]
            vn = z_ref[:, AV_OFF + c0:AV_OFF + c0 + ATT_DH]
            kc = caches[g][:, h * ATT_DH:(h + 1) * ATT_DH]
            vc = caches[g][:, ATT_GW + h * ATT_DH:ATT_GW + (h + 1) * ATT_DH]
            s_c = jnp.sum(kc * q, axis=-1, keepdims=True) * ATT_SCALE - pen * steps
            s_n = jnp.sum(kn * q, axis=-1, keepdims=True) * ATT_SCALE
            m = jnp.maximum(jnp.max(s_c, axis=0, keepdims=True), s_n)
            p_c = jnp.exp(s_c - m)
            p_n = jnp.exp(s_n - m)
            l = jnp.sum(p_c, axis=0, keepdims=True) + p_n
            o = (jnp.sum(p_c * vc, axis=0, keepdims=True) + p_n * vn) / l
            outs.append(o)
            lses.append(m + jnp.log(l))
        mm = jnp.maximum(jnp.maximum(lses[0], lses[1]), lses[2])
        ws = [jnp.exp(x - mm) for x in lses]
        out = (ws[0] * outs[0] + ws[1] * outs[1] + ws[2] * outs[2]) / (ws[0] + ws[1] + ws[2])
        o_ref[:, h * ATT_DH:(h + 1) * ATT_DH] = out.astype(o_ref.dtype)


def attn_step(zs, caches, layer):
    Bs = zs.shape[0]
    z3 = zs.reshape(Bs, 1, N_IN)
    cache_specs, cache_args = [], []
    for g, (win, dil) in enumerate(ATT_GROUPS):
        c = caches[g]
        lw = c.shape[2]
        assert lw == win and lw // dil == BAND
        cache_args.append(c.reshape(DEPTH, Bs, BAND, dil * 2 * ATT_GW))
        cache_specs.append(pl.BlockSpec((None, None, BAND, 2 * ATT_GW), lambda b: (layer, b, 0, 0)))
    o = pl.pallas_call(
        _attn_step_body,
        grid=(Bs,),
        in_specs=[pl.BlockSpec((None, 1, N_IN), lambda b: (b, 0, 0))] + cache_specs,
        out_specs=pl.BlockSpec((None, 1, ATT_GW), lambda b: (b, 0, 0)),
        out_shape=jax.ShapeDtypeStruct((Bs, 1, ATT_GW), BF16),
        compiler_params=_params(("arbitrary",)),
        name="attn_step",
    )(z3, *cache_args)
    return o.reshape(Bs, ATT_GW)


def _moe_tables(top_e):
    N = top_e.shape[0]
    A = N * TOP_K
    tm, G = MOE_TM, MOE_GROUP
    n_blocks = -(-A // tm) + N_EXPERTS
    P = n_blocks * tm
    n_tiles = -(-n_blocks // G) + N_EXPERTS + 1
    i32 = jnp.int32
    flat_e = top_e.reshape(A)
    onehot = (flat_e[:, None] == jnp.arange(N_EXPERTS)[None, :]).astype(i32)
    rank = jnp.take_along_axis(jnp.cumsum(onehot, axis=0), flat_e[:, None], axis=1)[:, 0] - 1
    counts = jnp.sum(onehot, axis=0)
    blocks = (counts + tm - 1) // tm
    bend = jnp.cumsum(blocks)
    bstart = bend - blocks
    dest = (bstart[flat_e] * tm + rank).astype(i32)
    rows = jnp.zeros((P,), i32).at[dest].set(jnp.arange(A, dtype=i32) // TOP_K)
    tiles = (blocks + G - 1) // G
    tend = jnp.cumsum(tiles)
    tstart = tend - tiles
    n_used = tend[-1]
    t = jnp.arange(n_tiles, dtype=i32)
    last_e = jnp.max(jnp.where(blocks > 0, jnp.arange(N_EXPERTS), 0))
    e_of = jnp.minimum(jnp.searchsorted(tend, t, side='right'), N_EXPERTS - 1)
    k = t - tstart[e_of]
    used = t < n_used
    tile_e = jnp.where(used, e_of, last_e)
    tile_b = jnp.where(used, bstart[e_of] + k * G, 0)
    tile_n = jnp.where(used, jnp.minimum(G, blocks[e_of] - k * G), 0)
    is_fill = t == n_used
    tile_b = jnp.where(is_fill, bend[-1], tile_b)
    tile_n = jnp.where(is_fill, n_blocks - bend[-1], tile_n)
    return (rows, dest.reshape(N, TOP_K), tile_e.astype(i32), tile_b.astype(i32), tile_n.astype(i32),
            is_fill.astype(i32))


def _mixer(xp_b, xs_b, xp_f, xs_f, i, B, T, caches, state_hgrn, w_in, lb_all, hgrn_norm,
           w_branch_a, w_branch_b, w_out, ln1_g, ln1_b):
    Np = B * T
    Bs = xs_b.shape[0]
    zp = matmul(xp_b, w_in, i, 1024, 1280, F32)
    zs = matmul(xs_b, w_in, i, Bs, 1280, F32)

    oh_p, S_p = hgrn_prompt(zp, lb_all, hgrn_norm, i, B, T)
    oa_p = attn_prompt(zp, B, T)
    mp = branch_merge(oh_p, oa_p, zp, w_branch_a, w_branch_b, i, 1024)
    hp_f, hp_b = matmul_ln(mp, w_out, i, xp_f, ln1_g, ln1_b, i, 512, 1024)

    oh_s, S_s = hgrn_step(zs, state_hgrn, lb_all, hgrn_norm, i)
    oa_s = attn_step(zs, caches, i)
    ms = branch_merge(oh_s, oa_s, zs, w_branch_a, w_branch_b, i, Bs)
    hs_f, hs_b = matmul_ln(ms, w_out, i, xs_f, ln1_g, ln1_b, i, Bs, 1024)

    zp3 = zp.reshape(B, T, N_IN)
    kv_p, kv_s = [], []
    for g, (win, _) in enumerate(ATT_GROUPS):
        keep = min(win, T)
        ko, vo = AK_OFF + g * ATT_GW, AV_OFF + g * ATT_GW
        kp = zp3[:, T - keep:, ko:ko + ATT_GW].reshape(B, keep, ATT_HPG, ATT_DH)
        vp = zp3[:, T - keep:, vo:vo + ATT_GW].reshape(B, keep, ATT_HPG, ATT_DH)
        kv_p.append(jnp.stack([kp, vp], axis=2))
        ks = zs[:, ko:ko + ATT_GW].reshape(Bs, 1, ATT_HPG, ATT_DH)
        vs = zs[:, vo:vo + ATT_GW].reshape(Bs, 1, ATT_HPG, ATT_DH)
        kv_s.append(jnp.stack([ks, vs], axis=2))
    return (hp_f, hp_b, hs_f, hs_b, S_p, S_s, kv_p, kv_s)


def _dense_ffn(h_f, h_b, j, ffn_w_gate, ffn_w_up, ffn_w_down, ln2_g, ln2_b, i, tm_up, tm_dn):
    u = swiglu_up(h_b, ffn_w_gate, ffn_w_up, j, tm_up, 512)
    return matmul_ln(u, ffn_w_down, j, h_f, ln2_g, ln2_b, i, tm_dn, 512)


def _moe_ffn(hp_f, hp_b, hs_f, hs_b, j, moe_w_router, moe_b_router, moe_w_gate, moe_w_up, moe_w_down,
             ln2_g, ln2_b, i):
    Np, Bs = hp_f.shape[0], hs_f.shape[0]
    ep, gp = router(hp_f, moe_w_router, moe_b_router, j, 1024)
    es, gs = router(hs_f, moe_w_router, moe_b_router, j, Bs)
    top_e = jnp.concatenate([ep, es], axis=0)
    rows, dest, tile_e, tile_b, tile_n, tile_f = _moe_tables(top_e)
    h_b = jnp.concatenate([hp_b, hs_b], axis=0)
    xs = h_b[rows]
    yb = moe_experts(tile_e, tile_b, tile_n, tile_f, xs, moe_w_gate, moe_w_up, moe_w_down, j)
    y0 = yb[dest[:, 0]]
    y1 = yb[dest[:, 1]]
    xp_f, xp_b = add_ln(hp_f, y0[:Np], y1[:Np], gp, ln2_g, ln2_b, i, 256)
    xs_f, xs_b = add_ln(hs_f, y0[Np:], y1[Np:], gs, ln2_g, ln2_b, i, Bs)
    return xp_f, xp_b, xs_f, xs_b


def kernel(x_prompt, x_sample, cache_kv_w128, cache_kv_w512, cache_kv_w2048, state_hgrn,
           w_in, lb_param, hgrn_norm, w_branch_a, w_branch_b, w_out,
           ln1_g, ln1_b, ln2_g, ln2_b, ffn_w_gate, ffn_w_up, ffn_w_down,
           moe_w_router, moe_b_router, moe_w_gate, moe_w_up, moe_w_down):
    B, T, _ = x_prompt.shape
    Bs = x_sample.shape[0]
    lb_cum = jnp.cumsum(jax.nn.softmax(lb_param.astype(F32), axis=0), axis=0)
    lb_all = lb_cum - lb_cum[0]
    caches = (cache_kv_w128, cache_kv_w512, cache_kv_w2048)
    xp_f = x_prompt.reshape(B * T, D_MODEL)
    xs_f = x_sample.reshape(Bs, D_MODEL)
    xp_b = xp_f.astype(BF16)
    xs_b = xs_f.astype(BF16)
    kv_p = [[] for _ in ATT_GROUPS]
    kv_s = [[] for _ in ATT_GROUPS]
    hg_p, hg_s = [], []
    for i in range(DEPTH):
        hp_f, hp_b, hs_f, hs_b, S_p, S_s, kvp, kvs = _mixer(
            xp_b, xs_b, xp_f, xs_f, i, B, T, caches, state_hgrn, w_in, lb_all, hgrn_norm,
            w_branch_a, w_branch_b, w_out, ln1_g, ln1_b)
        for g in range(N_GROUPS):
            kv_p[g].append(kvp[g])
            kv_s[g].append(kvs[g])
        hg_p.append(S_p)
        hg_s.append(S_s)
        j = i // 2
        if i % 2 == 0:
            xp_f, xp_b = _dense_ffn(hp_f, hp_b, j, ffn_w_gate, ffn_w_up, ffn_w_down, ln2_g, ln2_b, i, 1024, 512)
            xs_f, xs_b = _dense_ffn(hs_f, hs_b, j, ffn_w_gate, ffn_w_up, ffn_w_down, ln2_g, ln2_b, i, Bs, Bs)
        else:
            xp_f, xp_b, xs_f, xs_b = _moe_ffn(hp_f, hp_b, hs_f, hs_b, j, moe_w_router, moe_b_router,
                                              moe_w_gate, moe_w_up, moe_w_down, ln2_g, ln2_b, i)
    return (xp_f.reshape(B, T, D_MODEL), xs_f.reshape(Bs, 1, D_MODEL),
            jnp.stack(kv_p[0], axis=0), jnp.stack(kv_s[0], axis=0),
            jnp.stack(kv_p[1], axis=0), jnp.stack(kv_s[1], axis=0),
            jnp.stack(kv_p[2], axis=0), jnp.stack(kv_s[2], axis=0),
            jnp.stack(hg_p, axis=0), jnp.stack(hg_s, axis=0))
```

```python
import functools

import numpy as np
import jax
import jax.numpy as jnp
from jax import lax
from jax.experimental import pallas as pl
from jax.experimental.pallas import tpu as pltpu

F32 = jnp.float32
BF16 = jnp.bfloat16

D_MODEL = 2048
DEPTH = 4
HG_HEADS = 8
HG_D = 128
HG_WIDTH = HG_HEADS * HG_D
ATT_GROUPS = ((128, 1), (512, 4), (2048, 16))
N_GROUPS = 3
ATT_HPG = 4
ATT_DH = 128
ATT_HEADS = N_GROUPS * ATT_HPG
ATT_GW = ATT_HPG * ATT_DH
ATT_SCALE = ATT_DH ** -0.5
BAND = 128
HQ_OFF, HF_OFF, HI_OFF, HGATE_OFF = 0, 1024, 2048, 3072
AQ_OFF, AK_OFF, AV_OFF = 4096, 5632, 7168
GA_OFF, GB_OFF = 8704, 10752
N_IN = 12800
N_EXPERTS = 8
TOP_K = 2
ALPHA = (2 * DEPTH) ** 0.25
LN_EPS = 1e-5
RMS_EPS = 1e-6
NEG = -1e30
LANES = 128
LOG2E = 1.4426950408889634

VMEM_LIMIT = 56 * 1024 * 1024


def _params(sem):
    return pltpu.CompilerParams(dimension_semantics=sem, vmem_limit_bytes=VMEM_LIMIT)


def _dot(a, b):
    return jnp.dot(a, b, preferred_element_type=F32)


def _dot_nt(a, b):
    return lax.dot_general(a, b, (((1,), (1,)), ((), ())), preferred_element_type=F32)


def _dot_tn(a, b):
    return lax.dot_general(a, b, (((0,), (0,)), ((), ())), preferred_element_type=F32)


def _sigmoid(x):
    return 1.0 / (1.0 + jnp.exp(-x))


def _layer_norm(y, g, b):
    mu = jnp.mean(y, axis=-1, keepdims=True)
    d = y - mu
    var = jnp.mean(d * d, axis=-1, keepdims=True)
    return d * lax.rsqrt(var + LN_EPS) * g + b


def _mm_body(x_ref, w_ref, o_ref, wb_ref):
    @pl.when(pl.program_id(1) == 0)
    def _():
        wb_ref[...] = w_ref[...].astype(BF16)

    o_ref[...] = _dot(x_ref[...], wb_ref[...]).astype(o_ref.dtype)


def matmul(x, w, layer, tm, tn, out_dtype):
    M, K = x.shape
    N = w.shape[-1]
    return pl.pallas_call(
        _mm_body,
        grid=(N // tn, M // tm),
        in_specs=[pl.BlockSpec((tm, K), lambda j, i: (i, 0)),
                  pl.BlockSpec((None, K, tn), lambda j, i: (layer, 0, j))],
        out_specs=pl.BlockSpec((tm, tn), lambda j, i: (i, j)),
        out_shape=jax.ShapeDtypeStruct((M, N), out_dtype),
        scratch_shapes=[pltpu.VMEM((K, tn), BF16)],
        compiler_params=_params(("arbitrary", "arbitrary")),
        name="mm",
    )(x, w)


def _merge_body(oh_ref, oa_ref, ga_ref, gb_ref, wa_ref, wb_ref, o_ref, wab_ref, wbb_ref):
    @pl.when(pl.program_id(1) == 0)
    def _():
        wab_ref[...] = wa_ref[...].astype(BF16)
        wbb_ref[...] = wb_ref[...].astype(BF16)

    ya = _dot(oh_ref[...], wab_ref[...])
    yb = _dot(oa_ref[...], wbb_ref[...])
    o_ref[...] = (_sigmoid(ga_ref[...]) * ya + _sigmoid(gb_ref[...]) * yb).astype(o_ref.dtype)


def branch_merge(oh, oa, z, w_a, w_b, layer, tm):
    M = oh.shape[0]
    tn = 512
    ga0, gb0 = GA_OFF // tn, GB_OFF // tn
    return pl.pallas_call(
        _merge_body,
        grid=(D_MODEL // tn, M // tm),
        in_specs=[pl.BlockSpec((tm, HG_WIDTH), lambda j, i: (i, 0)),
                  pl.BlockSpec((tm, ATT_GW), lambda j, i: (i, 0)),
                  pl.BlockSpec((tm, tn), lambda j, i: (i, ga0 + j)),
                  pl.BlockSpec((tm, tn), lambda j, i: (i, gb0 + j)),
                  pl.BlockSpec((None, HG_WIDTH, tn), lambda j, i: (layer, 0, j)),
                  pl.BlockSpec((None, ATT_GW, tn), lambda j, i: (layer, 0, j))],
        out_specs=pl.BlockSpec((tm, tn), lambda j, i: (i, j)),
        out_shape=jax.ShapeDtypeStruct((M, D_MODEL), BF16),
        scratch_shapes=[pltpu.VMEM((HG_WIDTH, tn), BF16), pltpu.VMEM((ATT_GW, tn), BF16)],
        compiler_params=_params(("arbitrary", "arbitrary")),
        name="branch_merge",
    )(oh, oa, z, z, w_a, w_b)


def _mm_ln_body(a_ref, w_ref, x_ref, g_ref, b_ref, of_ref, ob_ref, acc_ref, *, nk):
    k = pl.program_id(1)

    @pl.when(k == 0)
    def _():
        acc_ref[...] = jnp.zeros_like(acc_ref)

    acc_ref[...] += _dot(a_ref[...], w_ref[...].astype(BF16))

    @pl.when(k == nk - 1)
    def _():
        out = _layer_norm(ALPHA * x_ref[...] + acc_ref[...], g_ref[...], b_ref[...])
        of_ref[...] = out
        ob_ref[...] = out.astype(BF16)


def matmul_ln(a, w, layer, x, g, b, ln_layer, tm, tk):
    M, K = a.shape
    nk = K // tk
    g3 = g.reshape(DEPTH, 1, D_MODEL)
    b3 = b.reshape(DEPTH, 1, D_MODEL)
    return pl.pallas_call(
        functools.partial(_mm_ln_body, nk=nk),
        grid=(M // tm, nk),
        in_specs=[pl.BlockSpec((tm, tk), lambda i, k: (i, k)),
                  pl.BlockSpec((None, tk, D_MODEL), lambda i, k: (layer, k, 0)),
                  pl.BlockSpec((tm, D_MODEL), lambda i, k: (i, 0)),
                  pl.BlockSpec((None, 1, D_MODEL), lambda i, k: (ln_layer, 0, 0)),
                  pl.BlockSpec((None, 1, D_MODEL), lambda i, k: (ln_layer, 0, 0))],
        out_specs=[pl.BlockSpec((tm, D_MODEL), lambda i, k: (i, 0)),
                   pl.BlockSpec((tm, D_MODEL), lambda i, k: (i, 0))],
        out_shape=[jax.ShapeDtypeStruct((M, D_MODEL), F32),
                   jax.ShapeDtypeStruct((M, D_MODEL), BF16)],
        scratch_shapes=[pltpu.VMEM((tm, D_MODEL), F32)],
        compiler_params=_params(("arbitrary", "arbitrary")),
        name="mm_ln",
    )(a, w, x, g3, b3)


def _add_ln_body(x_ref, y0_ref, y1_ref, gate_ref, g_ref, b_ref, of_ref, ob_ref):
    gate = gate_ref[...]
    y = gate[:, 0:1] * y0_ref[...] + gate[:, 1:2] * y1_ref[...]
    out = _layer_norm(ALPHA * x_ref[...] + y, g_ref[...], b_ref[...])
    of_ref[...] = out
    ob_ref[...] = out.astype(BF16)


def add_ln(x, y0, y1, gate, g, b, ln_layer, tm):
    M = x.shape[0]
    g3 = g.reshape(DEPTH, 1, D_MODEL)
    b3 = b.reshape(DEPTH, 1, D_MODEL)
    row = pl.BlockSpec((tm, D_MODEL), lambda i: (i, 0))
    vec = pl.BlockSpec((None, 1, D_MODEL), lambda i: (ln_layer, 0, 0))
    return pl.pallas_call(
        _add_ln_body,
        grid=(M // tm,),
        in_specs=[row, row, row, pl.BlockSpec((tm, TOP_K), lambda i: (i, 0)), vec, vec],
        out_specs=[row, row],
        out_shape=[jax.ShapeDtypeStruct((M, D_MODEL), F32),
                   jax.ShapeDtypeStruct((M, D_MODEL), BF16)],
        compiler_params=_params(("arbitrary",)),
        name="add_ln",
    )(x, y0, y1, gate, g3, b3)


def _swiglu_up_body(x_ref, wg_ref, wu_ref, o_ref, wgb_ref, wub_ref):
    @pl.when(pl.program_id(1) == 0)
    def _():
        wgb_ref[...] = wg_ref[...].astype(BF16)
        wub_ref[...] = wu_ref[...].astype(BF16)

    x = x_ref[...]
    a = _dot(x, wgb_ref[...])
    u = _dot(x, wub_ref[...])
    o_ref[...] = (a * _sigmoid(a) * u).astype(o_ref.dtype)


def swiglu_up(x, wg, wu, layer, tm, tn):
    M, K = x.shape
    N = wg.shape[-1]
    wspec = pl.BlockSpec((None, K, tn), lambda j, i: (layer, 0, j))
    return pl.pallas_call(
        _swiglu_up_body,
        grid=(N // tn, M // tm),
        in_specs=[pl.BlockSpec((tm, K), lambda j, i: (i, 0)), wspec, wspec],
        out_specs=pl.BlockSpec((tm, tn), lambda j, i: (i, j)),
        out_shape=jax.ShapeDtypeStruct((M, N), BF16),
        scratch_shapes=[pltpu.VMEM((K, tn), BF16), pltpu.VMEM((K, tn), BF16)],
        compiler_params=_params(("arbitrary", "arbitrary")),
        name="swiglu_up",
    )(x, wg, wu)


MOE_TM = 256
MOE_GROUP = 10
MOE_CHUNKS = (8, 4, 2, 1)
MOE_TF = 256


def _moe_body(se_ref, sb_ref, sn_ref, sf_ref, x_hbm, wg_ref, wu_ref, wd_ref, out_hbm,
              xbuf, acc, wgb, wub, wdb, sem_in, sem_out, *, nj):
    s = pl.program_id(0)
    j = pl.program_id(1)
    nb = sn_ref[s]
    fill = sf_ref[s] == 1
    active = (nb > 0) & (sf_ref[s] == 0)
    row0 = sb_ref[s] * MOE_TM

    def rows(c, n=1):
        return pl.ds(pl.multiple_of(c * MOE_TM, MOE_TM), n * MOE_TM)

    def x_copy(c):
        return pltpu.make_async_copy(x_hbm.at[pl.ds(row0 + c * MOE_TM, MOE_TM)], xbuf.at[rows(c)], sem_in)

    def out_copy(c):
        return pltpu.make_async_copy(acc.at[rows(c)], out_hbm.at[pl.ds(row0 + c * MOE_TM, MOE_TM)], sem_out)

    def each_block(fn):
        def body(c, carry):
            fn(c)
            return carry
        lax.fori_loop(0, nb, body, 0)

    def zero(c):
        acc[rows(c), :] = jnp.zeros((MOE_TM, acc.shape[-1]), F32)

    @pl.when((j == 0) & active)
    def _():
        each_block(lambda c: x_copy(c).start())
        each_block(zero)
        each_block(lambda c: x_copy(c).wait())

    def compute(c, n):
        x = xbuf[rows(c, n), :]
        a = _dot(x, wgb[...])
        u = _dot(x, wub[...])
        acc[rows(c, n), :] += _dot((a * _sigmoid(a) * u).astype(BF16), wdb[...])

    @pl.when(active)
    def _():
        wgb[...] = wg_ref[...].astype(BF16)
        wub[...] = wu_ref[...].astype(BF16)
        wdb[...] = wd_ref[...].astype(BF16)
        big = MOE_CHUNKS[0]

        def big_chunk(i, carry):
            compute(i * big, big)
            return carry

        n_big = nb // big
        lax.fori_loop(0, n_big, big_chunk, 0)
        rem = nb - n_big * big
        for n in MOE_CHUNKS[1:]:
            @pl.when((rem & n) != 0)
            def _(n=n):
                compute(n_big * big + (rem & ~(2 * n - 1)), n)

    @pl.when((j == nj - 1) & fill)
    def _():
        each_block(zero)

    @pl.when((j == nj - 1) & (nb > 0))
    def _():
        each_block(lambda c: out_copy(c).start())
        each_block(lambda c: out_copy(c).wait())


def moe_experts(tile_e, tile_b, tile_n, tile_f, xs, wg, wu, wd, layer):
    P, D = xs.shape
    F = wg.shape[-1]
    nj = F // MOE_TF
    S = tile_e.shape[0]

    def jj(s, j, sn, sf):
        return jnp.where((sn[s] > 0) & (sf[s] == 0), j, nj - 1)

    up = pl.BlockSpec((None, None, D, MOE_TF),
                      lambda s, j, se, sb, sn, sf: (layer, se[s], 0, jj(s, j, sn, sf)))
    down = pl.BlockSpec((None, None, MOE_TF, D),
                        lambda s, j, se, sb, sn, sf: (layer, se[s], jj(s, j, sn, sf), 0))
    return pl.pallas_call(
        functools.partial(_moe_body, nj=nj),
        grid_spec=pltpu.PrefetchScalarGridSpec(
            num_scalar_prefetch=4,
            grid=(S, nj),
            in_specs=[pl.BlockSpec(memory_space=pl.ANY), up, up, down],
            out_specs=pl.BlockSpec(memory_space=pl.ANY),
            scratch_shapes=[pltpu.VMEM((MOE_GROUP * MOE_TM, D), BF16),
                            pltpu.VMEM((MOE_GROUP * MOE_TM, D), F32),
                            pltpu.VMEM((D, MOE_TF), BF16), pltpu.VMEM((D, MOE_TF), BF16),
                            pltpu.VMEM((MOE_TF, D), BF16),
                            pltpu.SemaphoreType.DMA(()), pltpu.SemaphoreType.DMA(())]),
        out_shape=jax.ShapeDtypeStruct((P, D), F32),
        compiler_params=_params(("arbitrary", "arbitrary")),
        name="moe_experts",
    )(tile_e, tile_b, tile_n, tile_f, xs, wg, wu, wd)


def _router_body(x_ref, w_ref, b_ref, e_ref, g_ref):
    x = x_ref[...]
    w = w_ref[...]
    xh = x.astype(BF16)
    xl = (x - xh.astype(F32)).astype(BF16)
    wh = w.astype(BF16)
    wl = (w - wh.astype(F32)).astype(BF16)
    logits = _dot(xh, wh) + (_dot(xh, wl) + _dot(xl, wh)) + b_ref[...]
    lane = lax.broadcasted_iota(jnp.int32, logits.shape, 1).astype(F32)
    m1 = jnp.max(logits, axis=-1, keepdims=True)
    i1 = jnp.min(jnp.where(logits == m1, lane, float(LANES)), axis=-1, keepdims=True)
    rest = jnp.where(lane == i1, 2.0 * NEG, logits)
    m2 = jnp.max(rest, axis=-1, keepdims=True)
    i2 = jnp.min(jnp.where(rest == m2, lane, float(LANES)), axis=-1, keepdims=True)
    g1 = 1.0 / (1.0 + jnp.exp(m2 - m1))
    two = lax.broadcasted_iota(jnp.int32, (logits.shape[0], TOP_K), 1)
    e_ref[...] = jnp.where(two == 0, i1, i2).astype(jnp.int32)
    g_ref[...] = jnp.where(two == 0, g1, 1.0 - g1)


def router(x, w_router, b_router, j, tm):
    M = x.shape[0]
    n_moe = w_router.shape[0]
    w_pad = jnp.pad(w_router.astype(F32), ((0, 0), (0, 0), (0, LANES - N_EXPERTS)))
    b3 = jnp.pad(b_router.astype(F32).reshape(n_moe, 1, N_EXPERTS), ((0, 0), (0, 0), (0, LANES - N_EXPERTS)),
                 constant_values=NEG)
    return pl.pallas_call(
        _router_body,
        grid=(M // tm,),
        in_specs=[pl.BlockSpec((tm, D_MODEL), lambda i: (i, 0)),
                  pl.BlockSpec((None, D_MODEL, LANES), lambda i: (j, 0, 0)),
                  pl.BlockSpec((None, 1, LANES), lambda i: (j, 0, 0))],
        out_specs=[pl.BlockSpec((tm, TOP_K), lambda i: (i, 0)),
                   pl.BlockSpec((tm, TOP_K), lambda i: (i, 0))],
        out_shape=[jax.ShapeDtypeStruct((M, TOP_K), jnp.int32),
                   jax.ShapeDtypeStruct((M, TOP_K), F32)],
        compiler_params=_params(("arbitrary",)),
        name="router",
    )(x, w_pad, b3)


HG_CHUNK = 128
HG_HPS = 8
HG_LEVELS = 7
_ROW_B = HG_LEVELS * HG_CHUNK
_ROW_U = _ROW_B + HG_CHUNK


def _hgrn_tables():
    C = HG_CHUNK
    t = np.arange(C)[:, None]
    u = np.arange(C)[None, :]
    sel, low, seg = [], [], []
    for lvl in range(HG_LEVELS):
        w = C >> (lvl + 1)
        ref = (t // (2 * w)) * (2 * w) + w - 1
        lower = (t % (2 * w)) >= w
        sel.append(((u > np.minimum(t, ref)) & (u <= np.maximum(t, ref))).astype(np.float32))
        low.append(np.broadcast_to(lower, (C, HG_HPS * HG_D)).astype(np.float32))
        seg.append(((t // (2 * w)) == (u // (2 * w))).astype(np.float32))
    sel.append((u <= t).astype(np.float32))
    sel.append((u > t).astype(np.float32))
    return (jnp.asarray(np.concatenate(sel, axis=0), BF16),
            jnp.asarray(np.stack(low)), jnp.asarray(np.stack(seg)))


def _hgrn_body(sel_ref, low_ref, seg_ref, hq_ref, hf_ref, hi_ref, hg_ref, lb_ref, gn_ref,
               o_ref, s_ref, st_ref, *, nc):
    c = pl.program_id(2)

    @pl.when(c == 0)
    def _():
        st_ref[...] = jnp.zeros_like(st_ref)

    C = HG_CHUNK
    heads = [slice(hh * HG_D, (hh + 1) * HG_D) for hh in range(HG_HPS)]
    q = hq_ref[...]
    hf = hf_ref[...]
    v = hi_ref[...]
    lb = lb_ref[...]
    g = jnp.log(lb + (1.0 - lb) * _sigmoid(hf)) * LOG2E
    kk = (1.0 - lb) * _sigmoid(-hf)
    g_hi = g.astype(BF16)
    r1 = g - g_hi.astype(F32)
    g_mid = r1.astype(BF16)
    g_lo = (r1 - g_mid.astype(F32)).astype(BF16)
    sel = sel_ref[...]
    ex = jnp.exp2(_dot(sel, g_hi) + (_dot(sel, g_mid) + _dot(sel, g_lo)))
    eb = ex[_ROW_B:_ROW_B + C]
    eu = ex[_ROW_U:_ROW_U + C]
    eb_end = eb[C - 1:C, :]
    v_b = v.astype(BF16)

    qe = (q * eb).astype(BF16)
    sts = [st_ref[hh] for hh in range(HG_HPS)]
    o = [_dot_nt(qe[:, hs], sts[hh].astype(BF16)) for hh, hs in enumerate(heads)]
    a = [jnp.zeros((C, C), F32) for _ in heads]
    for lvl in range(HG_LEVELS):
        e = ex[lvl * C:(lvl + 1) * C]
        low = low_ref[lvl]
        qt = (q * e * low).astype(BF16)
        kt = (kk * e * (1.0 - low)).astype(BF16)
        seg = seg_ref[lvl]
        for hh, hs in enumerate(heads):
            a[hh] = a[hh] + _dot_nt(qt[:, hs], kt[:, hs]) * seg
    kd = (kk * eu).astype(BF16)
    qk = q * kk
    for hh, hs in enumerate(heads):
        o[hh] = o[hh] + _dot(a[hh].astype(BF16), v_b[:, hs])
        o[hh] = o[hh] + jnp.sum(qk[:, hs], axis=-1, keepdims=True) * v[:, hs]
        st_ref[hh] = sts[hh] * eb_end[:, hs] + _dot_tn(v_b[:, hs], kd[:, hs])

    hg = hg_ref[...]
    gate = hg * _sigmoid(hg)
    for hh, hs in enumerate(heads):
        on = o[hh] * lax.rsqrt(jnp.mean(o[hh] * o[hh], axis=-1, keepdims=True) + RMS_EPS) * gn_ref[...]
        o_ref[:, hs] = (on * gate[:, hs]).astype(o_ref.dtype)

    @pl.when(c == nc - 1)
    def _():
        for hh in range(HG_HPS):
            s_ref[hh] = st_ref[hh].T


def hgrn_prompt(z, lb, gnorm, layer, B, T):
    C = HG_CHUNK
    nc = T // C
    W = HG_HPS * HG_D
    sel, low, seg = _hgrn_tables()
    lb3 = lb.reshape(DEPTH, 1, HG_WIDTH)
    gn3 = gnorm.reshape(DEPTH, 1, HG_D)

    def col(off):
        return pl.BlockSpec((C, W), lambda b, h, c: (b * nc + c, off // W + h))

    return pl.pallas_call(
        functools.partial(_hgrn_body, nc=nc),
        grid=(B, HG_HEADS // HG_HPS, nc),
        in_specs=[pl.BlockSpec(sel.shape, lambda b, h, c: (0, 0)),
                  pl.BlockSpec(low.shape, lambda b, h, c: (0, 0, 0)),
                  pl.BlockSpec(seg.shape, lambda b, h, c: (0, 0, 0)),
                  col(HQ_OFF), col(HF_OFF), col(HI_OFF), col(HGATE_OFF),
                  pl.BlockSpec((None, 1, W), lambda b, h, c: (layer, 0, h)),
                  pl.BlockSpec((None, 1, HG_D), lambda b, h, c: (layer, 0, 0))],
        out_specs=[pl.BlockSpec((C, W), lambda b, h, c: (b * nc + c, h)),
                   pl.BlockSpec((None, HG_HPS, HG_D, HG_D), lambda b, h, c: (b, h, 0, 0))],
        out_shape=[jax.ShapeDtypeStruct((B * T, HG_WIDTH), BF16),
                   jax.ShapeDtypeStruct((B, HG_HEADS, HG_D, HG_D), F32)],
        scratch_shapes=[pltpu.VMEM((HG_HPS, HG_D, HG_D), F32)],
        compiler_params=_params(("parallel", "parallel", "arbitrary")),
        name="hgrn_prompt",
    )(sel, low, seg, z, z, z, z, lb3, gn3)


def _hgrn_step_body(s0_ref, qc_ref, fc_ref, vr_ref, gr_ref, lbc_ref, gn_ref, o_ref, s_ref):
    for h in range(HG_HEADS):
        s0 = s0_ref[h]
        hf = fc_ref[h]
        lb = lbc_ref[h]
        f = lb + (1.0 - lb) * _sigmoid(hf)
        kk = (1.0 - lb) * _sigmoid(-hf)
        v = vr_ref[h]
        s_new = f * s0 + kk * v
        s_ref[h] = s_new
        o = jnp.sum(qc_ref[h] * s_new, axis=0, keepdims=True)
        hg = gr_ref[h]
        on = o * lax.rsqrt(jnp.mean(o * o, axis=-1, keepdims=True) + RMS_EPS) * gn_ref[...]
        o_ref[h] = (on * (hg * _sigmoid(hg))).astype(o_ref.dtype)


def hgrn_step(zs, state, lb, gnorm, layer):
    Bs = zs.shape[0]
    heads = lambda off: zs[:, off:off + HG_WIDTH].reshape(Bs, HG_HEADS, HG_D)
    qc = heads(HQ_OFF)[..., None]
    fc = heads(HF_OFF)[..., None]
    vr = heads(HI_OFF)[:, :, None, :]
    gr = heads(HGATE_OFF)[:, :, None, :]
    lbc = lb.reshape(DEPTH, HG_HEADS, HG_D, 1)
    gn3 = gnorm.reshape(DEPTH, 1, HG_D)
    colspec = pl.BlockSpec((None, HG_HEADS, HG_D, 1), lambda b: (b, 0, 0, 0))
    rowspec = pl.BlockSpec((None, HG_HEADS, 1, HG_D), lambda b: (b, 0, 0, 0))
    o, s = pl.pallas_call(
        _hgrn_step_body,
        grid=(Bs,),
        in_specs=[pl.BlockSpec((None, None, HG_HEADS, HG_D, HG_D), lambda b: (layer, b, 0, 0, 0)),
                  colspec, colspec, rowspec, rowspec,
                  pl.BlockSpec((None, HG_HEADS, HG_D, 1), lambda b: (layer, 0, 0, 0)),
                  pl.BlockSpec((None, 1, HG_D), lambda b: (layer, 0, 0))],
        out_specs=[rowspec,
                   pl.BlockSpec((None, HG_HEADS, HG_D, HG_D), lambda b: (b, 0, 0, 0))],
        out_shape=[jax.ShapeDtypeStruct((Bs, HG_HEADS, 1, HG_D), BF16),
                   jax.ShapeDtypeStruct((Bs, HG_HEADS, HG_D, HG_D), F32)],
        compiler_params=_params(("arbitrary",)),
        name="hgrn_step",
    )(state, qc, fc, vr, gr, lbc, gn3)
    return o.reshape(Bs, HG_WIDTH), s


def _slope(g, h):
    return float(2.0 ** (-8.0 * (g * ATT_HPG + h + 1) / ATT_HEADS))


def _band_rows(start, dil):
    return pl.ds(start, BAND, stride=dil) if dil > 1 else pl.ds(start, BAND)


ATT_UNITS = 4


def _attn_units(q_ref, k_ref, v_ref, o_scr, l_scr, starts, prev_starts, prev_offs, dil, pen, has_prev):
    n = range(len(starts))
    row = lax.broadcasted_iota(jnp.int32, (BAND, BAND), 0)
    colj = lax.broadcasted_iota(jnp.int32, (BAND, BAND), 1)
    pen_c = pen * (row - colj).astype(F32)
    rows_c = [_band_rows(s, dil) for s in starts]
    q = [q_ref[r, :].astype(BF16) for r in rows_c]
    kc = [k_ref[r, :].astype(BF16) for r in rows_c]
    vc = [v_ref[r, :].astype(BF16) for r in rows_c]
    s_c = [jnp.where(colj <= row, _dot_nt(q[i], kc[i]) * ATT_SCALE - pen_c, NEG) for i in n]
    m = [jnp.max(s, axis=-1, keepdims=True) for s in s_c]
    if has_prev:
        pen_p = pen_c + pen * float(BAND)
        rows_p = [_band_rows(s, dil) for s in prev_starts]
        kp = [k_ref[r, :].astype(BF16) for r in rows_p]
        vp = [v_ref[r, :].astype(BF16) for r in rows_p]
        s_p = [jnp.where(colj >= row, _dot_nt(q[i], kp[i]) * ATT_SCALE - pen_p, NEG) + prev_offs[i] for i in n]
        m = [jnp.maximum(m[i], jnp.max(s_p[i], axis=-1, keepdims=True)) for i in n]
    p_c = [jnp.exp(s_c[i] - m[i]) for i in n]
    l = [jnp.sum(p, axis=-1, keepdims=True) for p in p_c]
    acc = [_dot(p_c[i].astype(BF16), vc[i]) for i in n]
    if has_prev:
        p_p = [jnp.exp(s_p[i] - m[i]) for i in n]
        l = [l[i] + jnp.sum(p_p[i], axis=-1, keepdims=True) for i in n]
        acc = [acc[i] + _dot(p_p[i].astype(BF16), vp[i]) for i in n]
    for i in n:
        o_scr[rows_c[i], :] = acc[i] / l[i]
        l_scr[rows_c[i], :] = jnp.broadcast_to(m[i] + jnp.log(l[i]), (BAND, ATT_DH))


def _attn_body(slope_ref, *refs, T):
    qkv = refs[:3 * N_GROUPS]
    out_ref, o_scr, l_scr = refs[3 * N_GROUPS:]
    h = pl.program_id(1)
    for g, (_, dil) in enumerate(ATT_GROUPS):
        q_ref, k_ref, v_ref = qkv[3 * g:3 * g + 3]
        pen = slope_ref[g * ATT_HPG + h] * float(dil)
        nqb = T // dil // BAND

        def units(it, carry, q_ref=q_ref, k_ref=k_ref, v_ref=v_ref, g=g, dil=dil, pen=pen, nqb=nqb):
            starts, prev_starts, prev_offs = [], [], []
            for u in range(ATT_UNITS):
                idx = it * ATT_UNITS + u
                r = idx // nqb
                qb = idx - r * nqb
                start = r + qb * (BAND * dil)
                prev_start = jnp.maximum(start - BAND * dil, 0)
                if dil == 1:
                    start = pl.multiple_of(start, BAND)
                    prev_start = pl.multiple_of(prev_start, BAND)
                starts.append(start)
                prev_starts.append(prev_start)
                prev_offs.append(jnp.where(qb > 0, 0.0, NEG))
            _attn_units(q_ref, k_ref, v_ref, o_scr.at[g], l_scr.at[g], starts, prev_starts, prev_offs,
                        dil, pen, has_prev=nqb > 1)
            return carry

        lax.fori_loop(0, dil * nqb // ATT_UNITS, units, 0)

    def merge(i, carry):
        rs = pl.ds(pl.multiple_of(i * BAND, BAND), BAND)
        a0, a1, a2 = l_scr[0, rs, :], l_scr[1, rs, :], l_scr[2, rs, :]
        m = jnp.maximum(jnp.maximum(a0, a1), a2)
        w0, w1, w2 = jnp.exp(a0 - m), jnp.exp(a1 - m), jnp.exp(a2 - m)
        out = (w0 * o_scr[0, rs, :] + w1 * o_scr[1, rs, :] + w2 * o_scr[2, rs, :]) / (w0 + w1 + w2)
        out_ref[rs, :] = out.astype(out_ref.dtype)
        return carry

    lax.fori_loop(0, T // BAND, merge, 0)


def attn_prompt(z, B, T):
    slopes = jnp.asarray([_slope(g, h) for g in range(N_GROUPS) for h in range(ATT_HPG)], F32)

    def col(off, g):
        c0 = off // ATT_DH + g * ATT_HPG
        return pl.BlockSpec((T, ATT_DH), lambda b, h: (b, c0 + h))

    in_specs = [pl.BlockSpec(memory_space=pltpu.SMEM)]
    for g in range(N_GROUPS):
        in_specs += [col(AQ_OFF, g), col(AK_OFF, g), col(AV_OFF, g)]
    return pl.pallas_call(
        functools.partial(_attn_body, T=T),
        grid=(B, ATT_HPG),
        in_specs=in_specs,
        out_specs=pl.BlockSpec((T, ATT_DH), lambda b, h: (b, h)),
        out_shape=jax.ShapeDtypeStruct((B * T, ATT_GW), BF16),
        scratch_shapes=[pltpu.VMEM((N_GROUPS, T, ATT_DH), F32), pltpu.VMEM((N_GROUPS, T, ATT_DH), F32)],
        compiler_params=_params(("parallel", "arbitrary")),
        name="attn_prompt",
    )(slopes, *([z] * (3 * N_GROUPS)))


def _attn_step_body(z_ref, c0_ref, c1_ref, c2_ref, o_ref):
    caches = (c0_ref, c1_ref, c2_ref)
    steps = float(BAND) - lax.broadcasted_iota(jnp.int32, (BAND, 1, 1), 0).astype(F32)
    head = lax.broadcasted_iota(jnp.int32, (ATT_HPG, 1), 0).astype(F32)
    outs, lses = [], []
    for g, (_, dil) in enumerate(ATT_GROUPS):
        slope = jnp.exp2((head + float(g * ATT_HPG + 1)) * (-8.0 / ATT_HEADS))
        pen = slope * float(dil)
        r0 = g * ATT_HPG
        q = z_ref[AQ_OFF // ATT_DH + r0:AQ_OFF // ATT_DH + r0 + ATT_HPG, :]
        kn = z_ref[AK_OFF // ATT_DH + r0:AK_OFF // ATT_DH + r0 + ATT_HPG, :]
        vn = z_ref[AV_OFF // ATT_DH + r0:AV_OFF // ATT_DH + r0 + ATT_HPG, :]
        kc = caches[g][:, 0]
        vc = caches[g][:, 1]
        s_c = jnp.sum(kc * q[None], axis=-1, keepdims=True) * ATT_SCALE - pen[None] * steps
        s_n = jnp.sum(kn * q, axis=-1, keepdims=True) * ATT_SCALE
        m = jnp.maximum(jnp.max(s_c, axis=0), s_n)
        p_c = jnp.exp(s_c - m[None])
        p_n = jnp.exp(s_n - m)
        l = jnp.sum(p_c, axis=0) + p_n
        outs.append((jnp.sum(p_c * vc, axis=0) + p_n * vn) / l)
        lses.append(m + jnp.log(l))
    mm = jnp.maximum(jnp.maximum(lses[0], lses[1]), lses[2])
    ws = [jnp.exp(x - mm) for x in lses]
    out = (ws[0] * outs[0] + ws[1] * outs[1] + ws[2] * outs[2]) / (ws[0] + ws[1] + ws[2])
    o_ref[...] = out.astype(o_ref.dtype)


def attn_step(zs, caches, layer):
    Bs = zs.shape[0]
    zr = zs.reshape(Bs, N_IN // ATT_DH, ATT_DH)
    cache_specs, cache_args = [], []
    for g, (win, dil) in enumerate(ATT_GROUPS):
        c = caches[g]
        assert c.shape[2] == win and win // dil == BAND
        cache_args.append(c.reshape(DEPTH, Bs, BAND, dil, 2, ATT_HPG, ATT_DH))
        cache_specs.append(pl.BlockSpec((None, None, BAND, None, 2, ATT_HPG, ATT_DH),
                                        lambda b: (layer, b, 0, 0, 0, 0, 0)))
    o = pl.pallas_call(
        _attn_step_body,
        grid=(Bs,),
        in_specs=[pl.BlockSpec((None, N_IN // ATT_DH, ATT_DH), lambda b: (b, 0, 0))] + cache_specs,
        out_specs=pl.BlockSpec((None, ATT_HPG, ATT_DH), lambda b: (b, 0, 0)),
        out_shape=jax.ShapeDtypeStruct((Bs, ATT_HPG, ATT_DH), BF16),
        compiler_params=_params(("arbitrary",)),
        name="attn_step",
    )(zr, *cache_args)
    return o.reshape(Bs, ATT_GW)


def _moe_tables(top_e):
    N = top_e.shape[0]
    A = N * TOP_K
    tm, G = MOE_TM, MOE_GROUP
    n_blocks = -(-A // tm) + N_EXPERTS
    P = n_blocks * tm
    n_tiles = -(-n_blocks // G) + N_EXPERTS + 1
    i32 = jnp.int32
    flat_e = top_e.reshape(A)
    onehot = (flat_e[:, None] == jnp.arange(N_EXPERTS)[None, :]).astype(i32)
    rank = jnp.take_along_axis(jnp.cumsum(onehot, axis=0), flat_e[:, None], axis=1)[:, 0] - 1
    counts = jnp.sum(onehot, axis=0)
    blocks = (counts + tm - 1) // tm
    bend = jnp.cumsum(blocks)
    bstart = bend - blocks
    dest = (bstart[flat_e] * tm + rank).astype(i32)
    rows = jnp.zeros((P,), i32).at[dest].set(jnp.arange(A, dtype=i32) // TOP_K)
    tiles = (blocks + G - 1) // G
    tend = jnp.cumsum(tiles)
    tstart = tend - tiles
    n_used = tend[-1]
    t = jnp.arange(n_tiles, dtype=i32)
    last_e = jnp.max(jnp.where(blocks > 0, jnp.arange(N_EXPERTS), 0))
    e_of = jnp.minimum(jnp.searchsorted(tend, t, side='right'), N_EXPERTS - 1)
    k = t - tstart[e_of]
    used = t < n_used
    tile_e = jnp.where(used, e_of, last_e)
    tile_b = jnp.where(used, bstart[e_of] + k * G, 0)
    tile_n = jnp.where(used, jnp.minimum(G, blocks[e_of] - k * G), 0)
    is_fill = t == n_used
    tile_b = jnp.where(is_fill, bend[-1], tile_b)
    tile_n = jnp.where(is_fill, n_blocks - bend[-1], tile_n)
    return (rows, dest.reshape(N, TOP_K), tile_e.astype(i32), tile_b.astype(i32), tile_n.astype(i32),
            is_fill.astype(i32))


def _mixer(xp_b, xs_b, xp_f, xs_f, i, B, T, caches, state_hgrn, w_in, lb_all, hgrn_norm,
           w_branch_a, w_branch_b, w_out, ln1_g, ln1_b):
    Np = B * T
    Bs = xs_b.shape[0]
    zp = matmul(xp_b, w_in, i, 2048, 640, F32)
    zs = matmul(xs_b, w_in, i, Bs, 1280, F32)

    oh_p, S_p = hgrn_prompt(zp, lb_all, hgrn_norm, i, B, T)
    oa_p = attn_prompt(zp, B, T)
    mp = branch_merge(oh_p, oa_p, zp, w_branch_a, w_branch_b, i, 1024)
    hp_f, hp_b = matmul_ln(mp, w_out, i, xp_f, ln1_g, ln1_b, i, 512, 1024)

    oh_s, S_s = hgrn_step(zs, state_hgrn, lb_all, hgrn_norm, i)
    oa_s = attn_step(zs, caches, i)
    ms = branch_merge(oh_s, oa_s, zs, w_branch_a, w_branch_b, i, Bs)
    hs_f, hs_b = matmul_ln(ms, w_out, i, xs_f, ln1_g, ln1_b, i, Bs, 1024)

    zp3 = zp.reshape(B, T, N_IN)
    kv_p, kv_s = [], []
    for g, (win, _) in enumerate(ATT_GROUPS):
        keep = min(win, T)
        ko, vo = AK_OFF + g * ATT_GW, AV_OFF + g * ATT_GW
        kp = zp3[:, T - keep:, ko:ko + ATT_GW].reshape(B, keep, ATT_HPG, ATT_DH)
        vp = zp3[:, T - keep:, vo:vo + ATT_GW].reshape(B, keep, ATT_HPG, ATT_DH)
        kv_p.append(jnp.stack([kp, vp], axis=2))
        ks = zs[:, ko:ko + ATT_GW].reshape(Bs, 1, ATT_HPG, ATT_DH)
        vs = zs[:, vo:vo + ATT_GW].reshape(Bs, 1, ATT_HPG, ATT_DH)
        kv_s.append(jnp.stack([ks, vs], axis=2))
    return (hp_f, hp_b, hs_f, hs_b, S_p, S_s, kv_p, kv_s)


def _dense_ffn(h_f, h_b, j, ffn_w_gate, ffn_w_up, ffn_w_down, ln2_g, ln2_b, i, tm_up, tm_dn):
    u = swiglu_up(h_b, ffn_w_gate, ffn_w_up, j, tm_up, 512)
    return matmul_ln(u, ffn_w_down, j, h_f, ln2_g, ln2_b, i, tm_dn, 1408)


def _moe_ffn(hp_f, hp_b, hs_f, hs_b, j, moe_w_router, moe_b_router, moe_w_gate, moe_w_up, moe_w_down,
             ln2_g, ln2_b, i):
    Np, Bs = hp_f.shape[0], hs_f.shape[0]
    ep, gp = router(hp_f, moe_w_router, moe_b_router, j, 1024)
    es, gs = router(hs_f, moe_w_router, moe_b_router, j, Bs)
    top_e = jnp.concatenate([ep, es], axis=0)
    rows, dest, tile_e, tile_b, tile_n, tile_f = _moe_tables(top_e)
    h_b = jnp.concatenate([hp_b, hs_b], axis=0)
    xs = h_b[rows]
    yb = moe_experts(tile_e, tile_b, tile_n, tile_f, xs, moe_w_gate, moe_w_up, moe_w_down, j)
    y0 = yb[dest[:, 0]]
    y1 = yb[dest[:, 1]]
    xp_f, xp_b = add_ln(hp_f, y0[:Np], y1[:Np], gp, ln2_g, ln2_b, i, 256)
    xs_f, xs_b = add_ln(hs_f, y0[Np:], y1[Np:], gs, ln2_g, ln2_b, i, Bs)
    return xp_f, xp_b, xs_f, xs_b


def kernel(x_prompt, x_sample, cache_kv_w128, cache_kv_w512, cache_kv_w2048, state_hgrn,
           w_in, lb_param, hgrn_norm, w_branch_a, w_branch_b, w_out,
           ln1_g, ln1_b, ln2_g, ln2_b, ffn_w_gate, ffn_w_up, ffn_w_down,
           moe_w_router, moe_b_router, moe_w_gate, moe_w_up, moe_w_down):
    B, T, _ = x_prompt.shape
    Bs = x_sample.shape[0]
    lb_cum = jnp.cumsum(jax.nn.softmax(lb_param.astype(F32), axis=0), axis=0)
    lb_all = lb_cum - lb_cum[0]
    caches = (cache_kv_w128, cache_kv_w512, cache_kv_w2048)
    xp_f = x_prompt.reshape(B * T, D_MODEL)
    xs_f = x_sample.reshape(Bs, D_MODEL)
    xp_b = xp_f.astype(BF16)
    xs_b = xs_f.astype(BF16)
    kv_p = [[] for _ in ATT_GROUPS]
    kv_s = [[] for _ in ATT_GROUPS]
    hg_p, hg_s = [], []
    for i in range(DEPTH):
        hp_f, hp_b, hs_f, hs_b, S_p, S_s, kvp, kvs = _mixer(
            xp_b, xs_b, xp_f, xs_f, i, B, T, caches, state_hgrn, w_in, lb_all, hgrn_norm,
            w_branch_a, w_branch_b, w_out, ln1_g, ln1_b)
        for g in range(N_GROUPS):
            kv_p[g].append(kvp[g])
            kv_s[g].append(kvs[g])
        hg_p.append(S_p)
        hg_s.append(S_s)
        j = i // 2
        if i % 2 == 0:
            xp_f, xp_b = _dense_ffn(hp_f, hp_b, j, ffn_w_gate, ffn_w_up, ffn_w_down, ln2_g, ln2_b, i, 2048, 512)
            xs_f, xs_b = _dense_ffn(hs_f, hs_b, j, ffn_w_gate, ffn_w_up, ffn_w_down, ln2_g, ln2_b, i, Bs, Bs)
        else:
            xp_f, xp_b, xs_f, xs_b = _moe_ffn(hp_f, hp_b, hs_f, hs_b, j, moe_w_router, moe_b_router,
                                              moe_w_gate, moe_w_up, moe_w_down, ln2_g, ln2_b, i)
    return (xp_f.reshape(B, T, D_MODEL), xs_f.reshape(Bs, 1, D_MODEL),
            jnp.stack(kv_p[0], axis=0), jnp.stack(kv_s[0], axis=0),
            jnp.stack(kv_p[1], axis=0), jnp.stack(kv_s[1], axis=0),
            jnp.stack(kv_p[2], axis=0), jnp.stack(kv_s[2], axis=0),
            jnp.stack(hg_p, axis=0), jnp.stack(hg_s, axis=0))
```

```python
import functools

import numpy as np
import jax
import jax.numpy as jnp
from jax import lax
from jax.experimental import pallas as pl
from jax.experimental.pallas import tpu as pltpu

F32 = jnp.float32
BF16 = jnp.bfloat16

D_MODEL = 2048
DEPTH = 4
HG_HEADS = 8
HG_D = 128
HG_WIDTH = HG_HEADS * HG_D
ATT_GROUPS = ((128, 1), (512, 4), (2048, 16))
N_GROUPS = 3
ATT_HPG = 4
ATT_DH = 128
ATT_HEADS = N_GROUPS * ATT_HPG
ATT_GW = ATT_HPG * ATT_DH
ATT_SCALE = ATT_DH ** -0.5
BAND = 128
HQ_OFF, HF_OFF, HI_OFF, HGATE_OFF = 0, 1024, 2048, 3072
AQ_OFF, AK_OFF, AV_OFF = 4096, 5632, 7168
GA_OFF, GB_OFF = 8704, 10752
N_IN = 12800
N_EXPERTS = 8
TOP_K = 2
ALPHA = (2 * DEPTH) ** 0.25
LN_EPS = 1e-5
RMS_EPS = 1e-6
NEG = -1e30
LANES = 128
LOG2E = 1.4426950408889634

VMEM_LIMIT = 56 * 1024 * 1024


def _params(sem):
    return pltpu.CompilerParams(dimension_semantics=sem, vmem_limit_bytes=VMEM_LIMIT)


def _dot(a, b):
    return jnp.dot(a, b, preferred_element_type=F32)


def _dot_nt(a, b):
    return lax.dot_general(a, b, (((1,), (1,)), ((), ())), preferred_element_type=F32)


def _dot_tn(a, b):
    return lax.dot_general(a, b, (((0,), (0,)), ((), ())), preferred_element_type=F32)


def _sigmoid(x):
    return 1.0 / (1.0 + jnp.exp(-x))


def _layer_norm(y, g, b):
    mu = jnp.mean(y, axis=-1, keepdims=True)
    d = y - mu
    var = jnp.mean(d * d, axis=-1, keepdims=True)
    return d * lax.rsqrt(var + LN_EPS) * g + b


def _mm_body(x_ref, w_ref, o_ref, wb_ref):
    @pl.when(pl.program_id(1) == 0)
    def _():
        wb_ref[...] = w_ref[...].astype(BF16)

    o_ref[...] = _dot(x_ref[...], wb_ref[...]).astype(o_ref.dtype)


def matmul(x, w, layer, tm, tn, out_dtype):
    M, K = x.shape
    N = w.shape[-1]
    return pl.pallas_call(
        _mm_body,
        grid=(N // tn, M // tm),
        in_specs=[pl.BlockSpec((tm, K), lambda j, i: (i, 0)),
                  pl.BlockSpec((None, K, tn), lambda j, i: (layer, 0, j))],
        out_specs=pl.BlockSpec((tm, tn), lambda j, i: (i, j)),
        out_shape=jax.ShapeDtypeStruct((M, N), out_dtype),
        scratch_shapes=[pltpu.VMEM((K, tn), BF16)],
        compiler_params=_params(("arbitrary", "arbitrary")),
        name="mm",
    )(x, w)


def _merge_body(oh_ref, oa_ref, ga_ref, gb_ref, wa_ref, wb_ref, o_ref, wab_ref, wbb_ref):
    @pl.when(pl.program_id(1) == 0)
    def _():
        wab_ref[...] = wa_ref[...].astype(BF16)
        wbb_ref[...] = wb_ref[...].astype(BF16)

    ya = _dot(oh_ref[...], wab_ref[...])
    yb = _dot(oa_ref[...], wbb_ref[...])
    o_ref[...] = (_sigmoid(ga_ref[...]) * ya + _sigmoid(gb_ref[...]) * yb).astype(o_ref.dtype)


def branch_merge(oh, oa, z, w_a, w_b, layer, tm):
    M = oh.shape[0]
    tn = 512
    ga0, gb0 = GA_OFF // tn, GB_OFF // tn
    return pl.pallas_call(
        _merge_body,
        grid=(D_MODEL // tn, M // tm),
        in_specs=[pl.BlockSpec((tm, HG_WIDTH), lambda j, i: (i, 0)),
                  pl.BlockSpec((tm, ATT_GW), lambda j, i: (i, 0)),
                  pl.BlockSpec((tm, tn), lambda j, i: (i, ga0 + j)),
                  pl.BlockSpec((tm, tn), lambda j, i: (i, gb0 + j)),
                  pl.BlockSpec((None, HG_WIDTH, tn), lambda j, i: (layer, 0, j)),
                  pl.BlockSpec((None, ATT_GW, tn), lambda j, i: (layer, 0, j))],
        out_specs=pl.BlockSpec((tm, tn), lambda j, i: (i, j)),
        out_shape=jax.ShapeDtypeStruct((M, D_MODEL), BF16),
        scratch_shapes=[pltpu.VMEM((HG_WIDTH, tn), BF16), pltpu.VMEM((ATT_GW, tn), BF16)],
        compiler_params=_params(("arbitrary", "arbitrary")),
        name="branch_merge",
    )(oh, oa, z, z, w_a, w_b)


def _mm_ln_body(a_ref, w_ref, x_ref, g_ref, b_ref, of_ref, ob_ref, acc_ref, wb_ref, *, nk):
    i = pl.program_id(0)
    k = pl.program_id(1)

    @pl.when(i == 0)
    def _():
        wb_ref[k] = w_ref[...].astype(BF16)

    @pl.when(k == 0)
    def _():
        acc_ref[...] = jnp.zeros_like(acc_ref)

    acc_ref[...] += _dot(a_ref[...], wb_ref[k])

    @pl.when(k == nk - 1)
    def _():
        out = _layer_norm(ALPHA * x_ref[...] + acc_ref[...], g_ref[...], b_ref[...])
        of_ref[...] = out
        ob_ref[...] = out.astype(BF16)


def matmul_ln(a, w, layer, x, g, b, ln_layer, tm, tk):
    M, K = a.shape
    nk = K // tk
    g3 = g.reshape(DEPTH, 1, D_MODEL)
    b3 = b.reshape(DEPTH, 1, D_MODEL)
    return pl.pallas_call(
        functools.partial(_mm_ln_body, nk=nk),
        grid=(M // tm, nk),
        in_specs=[pl.BlockSpec((tm, tk), lambda i, k: (i, k)),
                  pl.BlockSpec((None, tk, D_MODEL), lambda i, k: (layer, jnp.where(i == 0, k, nk - 1), 0),
                               pipeline_mode=pl.Buffered(1)),
                  pl.BlockSpec((tm, D_MODEL), lambda i, k: (i, 0)),
                  pl.BlockSpec((None, 1, D_MODEL), lambda i, k: (ln_layer, 0, 0)),
                  pl.BlockSpec((None, 1, D_MODEL), lambda i, k: (ln_layer, 0, 0))],
        out_specs=[pl.BlockSpec((tm, D_MODEL), lambda i, k: (i, 0)),
                   pl.BlockSpec((tm, D_MODEL), lambda i, k: (i, 0))],
        out_shape=[jax.ShapeDtypeStruct((M, D_MODEL), F32),
                   jax.ShapeDtypeStruct((M, D_MODEL), BF16)],
        scratch_shapes=[pltpu.VMEM((tm, D_MODEL), F32), pltpu.VMEM((nk, tk, D_MODEL), BF16)],
        compiler_params=_params(("arbitrary", "arbitrary")),
        name="mm_ln",
    )(a, w, x, g3, b3)


def _add_ln_body(x_ref, y0_ref, y1_ref, gate_ref, g_ref, b_ref, of_ref, ob_ref):
    gate = gate_ref[...]
    y = gate[:, 0:1] * y0_ref[...] + gate[:, 1:2] * y1_ref[...]
    out = _layer_norm(ALPHA * x_ref[...] + y, g_ref[...], b_ref[...])
    of_ref[...] = out
    ob_ref[...] = out.astype(BF16)


def add_ln(x, y0, y1, gate, g, b, ln_layer, tm):
    M = x.shape[0]
    g3 = g.reshape(DEPTH, 1, D_MODEL)
    b3 = b.reshape(DEPTH, 1, D_MODEL)
    row = pl.BlockSpec((tm, D_MODEL), lambda i: (i, 0))
    vec = pl.BlockSpec((None, 1, D_MODEL), lambda i: (ln_layer, 0, 0))
    return pl.pallas_call(
        _add_ln_body,
        grid=(M // tm,),
        in_specs=[row, row, row, pl.BlockSpec((tm, TOP_K), lambda i: (i, 0)), vec, vec],
        out_specs=[row, row],
        out_shape=[jax.ShapeDtypeStruct((M, D_MODEL), F32),
                   jax.ShapeDtypeStruct((M, D_MODEL), BF16)],
        compiler_params=_params(("arbitrary",)),
        name="add_ln",
    )(x, y0, y1, gate, g3, b3)


def _swiglu_up_body(x_ref, wg_ref, wu_ref, o_ref, wgb_ref, wub_ref):
    @pl.when(pl.program_id(1) == 0)
    def _():
        wgb_ref[...] = wg_ref[...].astype(BF16)
        wub_ref[...] = wu_ref[...].astype(BF16)

    x = x_ref[...]
    a = _dot(x, wgb_ref[...])
    u = _dot(x, wub_ref[...])
    o_ref[...] = (a * _sigmoid(a) * u).astype(o_ref.dtype)


def swiglu_up(x, wg, wu, layer, tm, tn):
    M, K = x.shape
    N = wg.shape[-1]
    wspec = pl.BlockSpec((None, K, tn), lambda j, i: (layer, 0, j))
    return pl.pallas_call(
        _swiglu_up_body,
        grid=(N // tn, M // tm),
        in_specs=[pl.BlockSpec((tm, K), lambda j, i: (i, 0)), wspec, wspec],
        out_specs=pl.BlockSpec((tm, tn), lambda j, i: (i, j)),
        out_shape=jax.ShapeDtypeStruct((M, N), BF16),
        scratch_shapes=[pltpu.VMEM((K, tn), BF16), pltpu.VMEM((K, tn), BF16)],
        compiler_params=_params(("arbitrary", "arbitrary")),
        name="swiglu_up",
    )(x, wg, wu)


MOE_TM = 256
MOE_GROUP = 10
MOE_CHUNKS = (8, 4, 2, 1)
MOE_TF = 256


def _moe_body(se_ref, sb_ref, sn_ref, sf_ref, x_hbm, wg_ref, wu_ref, wd_ref, out_hbm,
              xbuf, acc, wgb, wub, wdb, sem_in, sem_out, *, nj):
    s = pl.program_id(0)
    j = pl.program_id(1)
    nb = sn_ref[s]
    fill = sf_ref[s] == 1
    active = (nb > 0) & (sf_ref[s] == 0)
    row0 = sb_ref[s] * MOE_TM

    def rows(c, n=1):
        return pl.ds(pl.multiple_of(c * MOE_TM, MOE_TM), n * MOE_TM)

    def x_copy(c):
        return pltpu.make_async_copy(x_hbm.at[pl.ds(row0 + c * MOE_TM, MOE_TM)], xbuf.at[rows(c)], sem_in)

    def out_copy(c):
        return pltpu.make_async_copy(acc.at[rows(c)], out_hbm.at[pl.ds(row0 + c * MOE_TM, MOE_TM)], sem_out)

    def each_block(fn):
        def body(c, carry):
            fn(c)
            return carry
        lax.fori_loop(0, nb, body, 0)

    def zero(c):
        acc[rows(c), :] = jnp.zeros((MOE_TM, acc.shape[-1]), F32)

    @pl.when((j == 0) & active)
    def _():
        each_block(lambda c: x_copy(c).start())
        each_block(zero)
        each_block(lambda c: x_copy(c).wait())

    def compute(c, n):
        x = xbuf[rows(c, n), :]
        a = _dot(x, wgb[...])
        u = _dot(x, wub[...])
        acc[rows(c, n), :] += _dot((a * _sigmoid(a) * u).astype(BF16), wdb[...])

    @pl.when(active)
    def _():
        wgb[...] = wg_ref[...].astype(BF16)
        wub[...] = wu_ref[...].astype(BF16)
        wdb[...] = wd_ref[...].astype(BF16)
        big = MOE_CHUNKS[0]

        def big_chunk(i, carry):
            compute(i * big, big)
            return carry

        n_big = nb // big
        lax.fori_loop(0, n_big, big_chunk, 0)
        rem = nb - n_big * big
        for n in MOE_CHUNKS[1:]:
            @pl.when((rem & n) != 0)
            def _(n=n):
                compute(n_big * big + (rem & ~(2 * n - 1)), n)

    @pl.when((j == nj - 1) & fill)
    def _():
        each_block(zero)

    @pl.when((j == nj - 1) & (nb > 0))
    def _():
        each_block(lambda c: out_copy(c).start())
        each_block(lambda c: out_copy(c).wait())


def moe_experts(tile_e, tile_b, tile_n, tile_f, xs, wg, wu, wd, layer):
    P, D = xs.shape
    F = wg.shape[-1]
    nj = F // MOE_TF
    S = tile_e.shape[0]

    def jj(s, j, sn, sf):
        return jnp.where((sn[s] > 0) & (sf[s] == 0), j, nj - 1)

    up = pl.BlockSpec((None, None, D, MOE_TF),
                      lambda s, j, se, sb, sn, sf: (layer, se[s], 0, jj(s, j, sn, sf)))
    down = pl.BlockSpec((None, None, MOE_TF, D),
                        lambda s, j, se, sb, sn, sf: (layer, se[s], jj(s, j, sn, sf), 0))
    return pl.pallas_call(
        functools.partial(_moe_body, nj=nj),
        grid_spec=pltpu.PrefetchScalarGridSpec(
            num_scalar_prefetch=4,
            grid=(S, nj),
            in_specs=[pl.BlockSpec(memory_space=pl.ANY), up, up, down],
            out_specs=pl.BlockSpec(memory_space=pl.ANY),
            scratch_shapes=[pltpu.VMEM((MOE_GROUP * MOE_TM, D), BF16),
                            pltpu.VMEM((MOE_GROUP * MOE_TM, D), F32),
                            pltpu.VMEM((D, MOE_TF), BF16), pltpu.VMEM((D, MOE_TF), BF16),
                            pltpu.VMEM((MOE_TF, D), BF16),
                            pltpu.SemaphoreType.DMA(()), pltpu.SemaphoreType.DMA(())]),
        out_shape=jax.ShapeDtypeStruct((P, D), F32),
        compiler_params=_params(("arbitrary", "arbitrary")),
        name="moe_experts",
    )(tile_e, tile_b, tile_n, tile_f, xs, wg, wu, wd)


def _router_body(x_ref, w_ref, b_ref, e_ref, g_ref):
    x = x_ref[...]
    w = w_ref[...]
    xh = x.astype(BF16)
    xl = (x - xh.astype(F32)).astype(BF16)
    wh = w.astype(BF16)
    wl = (w - wh.astype(F32)).astype(BF16)
    logits = _dot(xh, wh) + (_dot(xh, wl) + _dot(xl, wh)) + b_ref[...]
    lane = lax.broadcasted_iota(jnp.int32, logits.shape, 1).astype(F32)
    m1 = jnp.max(logits, axis=-1, keepdims=True)
    i1 = jnp.min(jnp.where(logits == m1, lane, float(LANES)), axis=-1, keepdims=True)
    rest = jnp.where(lane == i1, 2.0 * NEG, logits)
    m2 = jnp.max(rest, axis=-1, keepdims=True)
    i2 = jnp.min(jnp.where(rest == m2, lane, float(LANES)), axis=-1, keepdims=True)
    g1 = 1.0 / (1.0 + jnp.exp(m2 - m1))
    two = lax.broadcasted_iota(jnp.int32, (logits.shape[0], TOP_K), 1)
    e_ref[...] = jnp.where(two == 0, i1, i2).astype(jnp.int32)
    g_ref[...] = jnp.where(two == 0, g1, 1.0 - g1)


def router(x, w_router, b_router, j, tm):
    M = x.shape[0]
    n_moe = w_router.shape[0]
    w_pad = jnp.pad(w_router.astype(F32), ((0, 0), (0, 0), (0, LANES - N_EXPERTS)))
    b3 = jnp.pad(b_router.astype(F32).reshape(n_moe, 1, N_EXPERTS), ((0, 0), (0, 0), (0, LANES - N_EXPERTS)),
                 constant_values=NEG)
    return pl.pallas_call(
        _router_body,
        grid=(M // tm,),
        in_specs=[pl.BlockSpec((tm, D_MODEL), lambda i: (i, 0)),
                  pl.BlockSpec((None, D_MODEL, LANES), lambda i: (j, 0, 0)),
                  pl.BlockSpec((None, 1, LANES), lambda i: (j, 0, 0))],
        out_specs=[pl.BlockSpec((tm, TOP_K), lambda i: (i, 0)),
                   pl.BlockSpec((tm, TOP_K), lambda i: (i, 0))],
        out_shape=[jax.ShapeDtypeStruct((M, TOP_K), jnp.int32),
                   jax.ShapeDtypeStruct((M, TOP_K), F32)],
        compiler_params=_params(("arbitrary",)),
        name="router",
    )(x, w_pad, b3)


HG_CHUNK = 128
HG_HPS = 8
HG_LEVELS = 7
_ROW_B = HG_LEVELS * HG_CHUNK
_ROW_U = _ROW_B + HG_CHUNK


def _hgrn_tables():
    C = HG_CHUNK
    t = np.arange(C)[:, None]
    u = np.arange(C)[None, :]
    sel, low, seg = [], [], []
    for lvl in range(HG_LEVELS):
        w = C >> (lvl + 1)
        ref = (t // (2 * w)) * (2 * w) + w - 1
        lower = (t % (2 * w)) >= w
        sel.append(((u > np.minimum(t, ref)) & (u <= np.maximum(t, ref))).astype(np.float32))
        low.append(np.broadcast_to(lower, (C, HG_HPS * HG_D)).astype(np.float32))
        upper_s = (u % (2 * w)) < w
        seg.append((((t // (2 * w)) == (u // (2 * w))) & lower & upper_s).astype(np.float32))
    sel.append((u <= t).astype(np.float32))
    sel.append((u > t).astype(np.float32))
    sel = np.concatenate(sel, axis=0)
    return (jnp.asarray(np.concatenate([sel, sel], axis=1), BF16),
            jnp.asarray(np.stack(low)), jnp.asarray(np.stack(seg)))


def _hgrn_body(sel_ref, low_ref, seg_ref, hq_ref, hf_ref, hi_ref, hg_ref, lb_ref, gn_ref,
               o_ref, s_ref, st_ref, *, nc):
    c = pl.program_id(2)

    @pl.when(c == 0)
    def _():
        st_ref[...] = jnp.zeros_like(st_ref)

    C = HG_CHUNK
    heads = [slice(hh * HG_D, (hh + 1) * HG_D) for hh in range(HG_HPS)]
    q = hq_ref[...]
    hf = hf_ref[...]
    v = hi_ref[...]
    lb = lb_ref[...]
    g = jnp.log(lb + (1.0 - lb) * _sigmoid(hf)) * LOG2E
    kk = (1.0 - lb) * _sigmoid(-hf)
    g_hi = g.astype(BF16)
    g_lo = (g - g_hi.astype(F32)).astype(BF16)
    ex = jnp.exp2(_dot(sel_ref[...], jnp.concatenate([g_hi, g_lo], axis=0)))
    eb = ex[_ROW_B:_ROW_B + C]
    eu = ex[_ROW_U:_ROW_U + C]
    eb_end = eb[C - 1:C, :]
    v_b = v.astype(BF16)

    qe = (q * eb).astype(BF16)
    sts = [st_ref[hh] for hh in range(HG_HPS)]
    o = [_dot_nt(qe[:, hs], sts[hh].astype(BF16)) for hh, hs in enumerate(heads)]
    a = [jnp.zeros((C, C), F32) for _ in heads]
    for lvl in range(HG_LEVELS):
        e = ex[lvl * C:(lvl + 1) * C]
        xt = (jnp.where(low_ref[lvl] > 0.5, q, kk) * e).astype(BF16)
        seg = seg_ref[lvl]
        for hh, hs in enumerate(heads):
            a[hh] = a[hh] + _dot_nt(xt[:, hs], xt[:, hs]) * seg
    kd = (kk * eu).astype(BF16)
    qk = q * kk
    for hh, hs in enumerate(heads):
        o[hh] = o[hh] + _dot(a[hh].astype(BF16), v_b[:, hs])
        o[hh] = o[hh] + jnp.sum(qk[:, hs], axis=-1, keepdims=True) * v[:, hs]
        st_ref[hh] = sts[hh] * eb_end[:, hs] + _dot_tn(v_b[:, hs], kd[:, hs])

    hg = hg_ref[...]
    gate = hg * _sigmoid(hg)
    for hh, hs in enumerate(heads):
        on = o[hh] * lax.rsqrt(jnp.mean(o[hh] * o[hh], axis=-1, keepdims=True) + RMS_EPS) * gn_ref[...]
        o_ref[:, hs] = (on * gate[:, hs]).astype(o_ref.dtype)

    @pl.when(c == nc - 1)
    def _():
        for hh in range(HG_HPS):
            s_ref[hh] = st_ref[hh].T


def hgrn_prompt(z, lb, gnorm, layer, B, T):
    C = HG_CHUNK
    nc = T // C
    W = HG_HPS * HG_D
    sel, low, seg = _hgrn_tables()
    lb3 = lb.reshape(DEPTH, 1, HG_WIDTH)
    gn3 = gnorm.reshape(DEPTH, 1, HG_D)

    def col(off):
        return pl.BlockSpec((C, W), lambda b, h, c: (b * nc + c, off // W + h))

    return pl.pallas_call(
        functools.partial(_hgrn_body, nc=nc),
        grid=(B, HG_HEADS // HG_HPS, nc),
        in_specs=[pl.BlockSpec(sel.shape, lambda b, h, c: (0, 0)),
                  pl.BlockSpec(low.shape, lambda b, h, c: (0, 0, 0)),
                  pl.BlockSpec(seg.shape, lambda b, h, c: (0, 0, 0)),
                  col(HQ_OFF), col(HF_OFF), col(HI_OFF), col(HGATE_OFF),
                  pl.BlockSpec((None, 1, W), lambda b, h, c: (layer, 0, h)),
                  pl.BlockSpec((None, 1, HG_D), lambda b, h, c: (layer, 0, 0))],
        out_specs=[pl.BlockSpec((C, W), lambda b, h, c: (b * nc + c, h)),
                   pl.BlockSpec((None, HG_HPS, HG_D, HG_D), lambda b, h, c: (b, h, 0, 0))],
        out_shape=[jax.ShapeDtypeStruct((B * T, HG_WIDTH), BF16),
                   jax.ShapeDtypeStruct((B, HG_HEADS, HG_D, HG_D), F32)],
        scratch_shapes=[pltpu.VMEM((HG_HPS, HG_D, HG_D), F32)],
        compiler_params=_params(("parallel", "parallel", "arbitrary")),
        name="hgrn_prompt",
    )(sel, low, seg, z, z, z, z, lb3, gn3)


def _hgrn_step_body(s0_ref, qc_ref, fc_ref, vr_ref, gr_ref, lbc_ref, gn_ref, o_ref, s_ref):
    for h in range(HG_HEADS):
        s0 = s0_ref[h]
        hf = fc_ref[h]
        lb = lbc_ref[h]
        f = lb + (1.0 - lb) * _sigmoid(hf)
        kk = (1.0 - lb) * _sigmoid(-hf)
        v = vr_ref[h]
        s_new = f * s0 + kk * v
        s_ref[h] = s_new
        o = jnp.sum(qc_ref[h] * s_new, axis=0, keepdims=True)
        hg = gr_ref[h]
        on = o * lax.rsqrt(jnp.mean(o * o, axis=-1, keepdims=True) + RMS_EPS) * gn_ref[...]
        o_ref[h] = (on * (hg * _sigmoid(hg))).astype(o_ref.dtype)


def hgrn_step(zs, state, lb, gnorm, layer):
    Bs = zs.shape[0]
    heads = lambda off: zs[:, off:off + HG_WIDTH].reshape(Bs, HG_HEADS, HG_D)
    qc = heads(HQ_OFF)[..., None]
    fc = heads(HF_OFF)[..., None]
    vr = heads(HI_OFF)[:, :, None, :]
    gr = heads(HGATE_OFF)[:, :, None, :]
    lbc = lb.reshape(DEPTH, HG_HEADS, HG_D, 1)
    gn3 = gnorm.reshape(DEPTH, 1, HG_D)
    colspec = pl.BlockSpec((None, HG_HEADS, HG_D, 1), lambda b: (b, 0, 0, 0))
    rowspec = pl.BlockSpec((None, HG_HEADS, 1, HG_D), lambda b: (b, 0, 0, 0))
    o, s = pl.pallas_call(
        _hgrn_step_body,
        grid=(Bs,),
        in_specs=[pl.BlockSpec((None, None, HG_HEADS, HG_D, HG_D), lambda b: (layer, b, 0, 0, 0)),
                  colspec, colspec, rowspec, rowspec,
                  pl.BlockSpec((None, HG_HEADS, HG_D, 1), lambda b: (layer, 0, 0, 0)),
                  pl.BlockSpec((None, 1, HG_D), lambda b: (layer, 0, 0))],
        out_specs=[rowspec,
                   pl.BlockSpec((None, HG_HEADS, HG_D, HG_D), lambda b: (b, 0, 0, 0))],
        out_shape=[jax.ShapeDtypeStruct((Bs, HG_HEADS, 1, HG_D), BF16),
                   jax.ShapeDtypeStruct((Bs, HG_HEADS, HG_D, HG_D), F32)],
        compiler_params=_params(("arbitrary",)),
        name="hgrn_step",
    )(state, qc, fc, vr, gr, lbc, gn3)
    return o.reshape(Bs, HG_WIDTH), s


def _slope(g, h):
    return float(2.0 ** (-8.0 * (g * ATT_HPG + h + 1) / ATT_HEADS))


def _band_rows(start, dil):
    return pl.ds(start, BAND, stride=dil) if dil > 1 else pl.ds(start, BAND)


ATT_UNITS = 4


def _attn_units(q_ref, k_ref, v_ref, o_scr, l_scr, starts, prev_starts, prev_offs, dil, pen, has_prev):
    n = range(len(starts))
    row = lax.broadcasted_iota(jnp.int32, (BAND, BAND), 0)
    colj = lax.broadcasted_iota(jnp.int32, (BAND, BAND), 1)
    pen_c = pen * (row - colj).astype(F32)
    rows_c = [_band_rows(s, dil) for s in starts]
    q = [q_ref[r, :].astype(BF16) for r in rows_c]
    kc = [k_ref[r, :].astype(BF16) for r in rows_c]
    vc = [v_ref[r, :].astype(BF16) for r in rows_c]
    s_c = [jnp.where(colj <= row, _dot_nt(q[i], kc[i]) * ATT_SCALE - pen_c, NEG) for i in n]
    m = [jnp.max(s, axis=-1, keepdims=True) for s in s_c]
    if has_prev:
        pen_p = pen_c + pen * float(BAND)
        rows_p = [_band_rows(s, dil) for s in prev_starts]
        kp = [k_ref[r, :].astype(BF16) for r in rows_p]
        vp = [v_ref[r, :].astype(BF16) for r in rows_p]
        s_p = [jnp.where(colj >= row, _dot_nt(q[i], kp[i]) * ATT_SCALE - pen_p, NEG) + prev_offs[i] for i in n]
        m = [jnp.maximum(m[i], jnp.max(s_p[i], axis=-1, keepdims=True)) for i in n]
    p_c = [jnp.exp(s_c[i] - m[i]) for i in n]
    l = [jnp.sum(p, axis=-1, keepdims=True) for p in p_c]
    acc = [_dot(p_c[i].astype(BF16), vc[i]) for i in n]
    if has_prev:
        p_p = [jnp.exp(s_p[i] - m[i]) for i in n]
        l = [l[i] + jnp.sum(p_p[i], axis=-1, keepdims=True) for i in n]
        acc = [acc[i] + _dot(p_p[i].astype(BF16), vp[i]) for i in n]
    for i in n:
        o_scr[rows_c[i], :] = acc[i] / l[i]
        l_scr[rows_c[i], :] = jnp.broadcast_to(m[i] + jnp.log(l[i]), (BAND, ATT_DH))


def _attn_body(slope_ref, *refs, T):
    qkv = refs[:3 * N_GROUPS]
    out_ref, o_scr, l_scr = refs[3 * N_GROUPS:]
    h = pl.program_id(1)
    for g, (_, dil) in enumerate(ATT_GROUPS):
        q_ref, k_ref, v_ref = qkv[3 * g:3 * g + 3]
        pen = slope_ref[g * ATT_HPG + h] * float(dil)
        nqb = T // dil // BAND

        def units(it, carry, q_ref=q_ref, k_ref=k_ref, v_ref=v_ref, g=g, dil=dil, pen=pen, nqb=nqb):
            starts, prev_starts, prev_offs = [], [], []
            for u in range(ATT_UNITS):
                idx = it * ATT_UNITS + u
                r = idx // nqb
                qb = idx - r * nqb
                start = r + qb * (BAND * dil)
                prev_start = jnp.maximum(start - BAND * dil, 0)
                if dil == 1:
                    start = pl.multiple_of(start, BAND)
                    prev_start = pl.multiple_of(prev_start, BAND)
                starts.append(start)
                prev_starts.append(prev_start)
                prev_offs.append(jnp.where(qb > 0, 0.0, NEG))
            _attn_units(q_ref, k_ref, v_ref, o_scr.at[g], l_scr.at[g], starts, prev_starts, prev_offs,
                        dil, pen, has_prev=nqb > 1)
            return carry

        lax.fori_loop(0, dil * nqb // ATT_UNITS, units, 0)

    def merge(i, carry):
        rs = pl.ds(pl.multiple_of(i * BAND, BAND), BAND)
        a0, a1, a2 = l_scr[0, rs, :], l_scr[1, rs, :], l_scr[2, rs, :]
        m = jnp.maximum(jnp.maximum(a0, a1), a2)
        w0, w1, w2 = jnp.exp(a0 - m), jnp.exp(a1 - m), jnp.exp(a2 - m)
        out = (w0 * o_scr[0, rs, :] + w1 * o_scr[1, rs, :] + w2 * o_scr[2, rs, :]) / (w0 + w1 + w2)
        out_ref[rs, :] = out.astype(out_ref.dtype)
        return carry

    lax.fori_loop(0, T // BAND, merge, 0)


def attn_prompt(z, B, T):
    slopes = jnp.asarray([_slope(g, h) for g in range(N_GROUPS) for h in range(ATT_HPG)], F32)

    def col(off, g):
        c0 = off // ATT_DH + g * ATT_HPG
        return pl.BlockSpec((T, ATT_DH), lambda b, h: (b, c0 + h))

    in_specs = [pl.BlockSpec(memory_space=pltpu.SMEM)]
    for g in range(N_GROUPS):
        in_specs += [col(AQ_OFF, g), col(AK_OFF, g), col(AV_OFF, g)]
    return pl.pallas_call(
        functools.partial(_attn_body, T=T),
        grid=(B, ATT_HPG),
        in_specs=in_specs,
        out_specs=pl.BlockSpec((T, ATT_DH), lambda b, h: (b, h)),
        out_shape=jax.ShapeDtypeStruct((B * T, ATT_GW), BF16),
        scratch_shapes=[pltpu.VMEM((N_GROUPS, T, ATT_DH), F32), pltpu.VMEM((N_GROUPS, T, ATT_DH), F32)],
        compiler_params=_params(("parallel", "arbitrary")),
        name="attn_prompt",
    )(slopes, *([z] * (3 * N_GROUPS)))


def _attn_step_body(z_ref, c0_ref, c1_ref, c2_ref, o_ref):
    caches = (c0_ref, c1_ref, c2_ref)
    steps = float(BAND) - lax.broadcasted_iota(jnp.int32, (BAND, 1, 1), 0).astype(F32)
    head = lax.broadcasted_iota(jnp.int32, (ATT_HPG, 1), 0).astype(F32)
    outs, lses = [], []
    for g, (_, dil) in enumerate(ATT_GROUPS):
        slope = jnp.exp2((head + float(g * ATT_HPG + 1)) * (-8.0 / ATT_HEADS))
        pen = slope * float(dil)
        r0 = g * ATT_HPG
        q = z_ref[AQ_OFF // ATT_DH + r0:AQ_OFF // ATT_DH + r0 + ATT_HPG, :]
        kn = z_ref[AK_OFF // ATT_DH + r0:AK_OFF // ATT_DH + r0 + ATT_HPG, :]
        vn = z_ref[AV_OFF // ATT_DH + r0:AV_OFF // ATT_DH + r0 + ATT_HPG, :]
        kc = caches[g][:, 0]
        vc = caches[g][:, 1]
        s_c = jnp.sum(kc * q[None], axis=-1, keepdims=True) * ATT_SCALE - pen[None] * steps
        s_n = jnp.sum(kn * q, axis=-1, keepdims=True) * ATT_SCALE
        m = jnp.maximum(jnp.max(s_c, axis=0), s_n)
        p_c = jnp.exp(s_c - m[None])
        p_n = jnp.exp(s_n - m)
        l = jnp.sum(p_c, axis=0) + p_n
        outs.append((jnp.sum(p_c * vc, axis=0) + p_n * vn) / l)
        lses.append(m + jnp.log(l))
    mm = jnp.maximum(jnp.maximum(lses[0], lses[1]), lses[2])
    ws = [jnp.exp(x - mm) for x in lses]
    out = (ws[0] * outs[0] + ws[1] * outs[1] + ws[2] * outs[2]) / (ws[0] + ws[1] + ws[2])
    o_ref[...] = out.astype(o_ref.dtype)


def attn_step(zs, caches, layer):
    Bs = zs.shape[0]
    zr = zs.reshape(Bs, N_IN // ATT_DH, ATT_DH)
    cache_specs, cache_args = [], []
    for g, (win, dil) in enumerate(ATT_GROUPS):
        c = caches[g]
        assert c.shape[2] == win and win // dil == BAND
        cache_args.append(c.reshape(DEPTH, Bs, BAND, dil, 2, ATT_HPG, ATT_DH))
        cache_specs.append(pl.BlockSpec((None, None, BAND, None, 2, ATT_HPG, ATT_DH),
                                        lambda b: (layer, b, 0, 0, 0, 0, 0)))
    o = pl.pallas_call(
        _attn_step_body,
        grid=(Bs,),
        in_specs=[pl.BlockSpec((None, N_IN // ATT_DH, ATT_DH), lambda b: (b, 0, 0))] + cache_specs,
        out_specs=pl.BlockSpec((None, ATT_HPG, ATT_DH), lambda b: (b, 0, 0)),
        out_shape=jax.ShapeDtypeStruct((Bs, ATT_HPG, ATT_DH), BF16),
        compiler_params=_params(("arbitrary",)),
        name="attn_step",
    )(zr, *cache_args)
    return o.reshape(Bs, ATT_GW)


def _moe_tables(top_e):
    N = top_e.shape[0]
    A = N * TOP_K
    tm, G = MOE_TM, MOE_GROUP
    n_blocks = -(-A // tm) + N_EXPERTS
    P = n_blocks * tm
    n_tiles = -(-n_blocks // G) + N_EXPERTS + 1
    i32 = jnp.int32
    flat_e = top_e.reshape(A)
    onehot = (flat_e[:, None] == jnp.arange(N_EXPERTS)[None, :]).astype(i32)
    rank = jnp.take_along_axis(jnp.cumsum(onehot, axis=0), flat_e[:, None], axis=1)[:, 0] - 1
    counts = jnp.sum(onehot, axis=0)
    blocks = (counts + tm - 1) // tm
    bend = jnp.cumsum(blocks)
    bstart = bend - blocks
    dest = (bstart[flat_e] * tm + rank).astype(i32)
    rows = jnp.zeros((P,), i32).at[dest].set(jnp.arange(A, dtype=i32) // TOP_K)
    tiles = (blocks + G - 1) // G
    tend = jnp.cumsum(tiles)
    tstart = tend - tiles
    n_used = tend[-1]
    t = jnp.arange(n_tiles, dtype=i32)
    last_e = jnp.max(jnp.where(blocks > 0, jnp.arange(N_EXPERTS), 0))
    e_of = jnp.minimum(jnp.searchsorted(tend, t, side='right'), N_EXPERTS - 1)
    k = t - tstart[e_of]
    used = t < n_used
    tile_e = jnp.where(used, e_of, last_e)
    tile_b = jnp.where(used, bstart[e_of] + k * G, 0)
    tile_n = jnp.where(used, jnp.minimum(G, blocks[e_of] - k * G), 0)
    is_fill = t == n_used
    tile_b = jnp.where(is_fill, bend[-1], tile_b)
    tile_n = jnp.where(is_fill, n_blocks - bend[-1], tile_n)
    return (rows, dest.reshape(N, TOP_K), tile_e.astype(i32), tile_b.astype(i32), tile_n.astype(i32),
            is_fill.astype(i32))


def _mixer(xp_b, xs_b, xp_f, xs_f, i, B, T, caches, state_hgrn, w_in, lb_all, hgrn_norm,
           w_branch_a, w_branch_b, w_out, ln1_g, ln1_b):
    Np = B * T
    Bs = xs_b.shape[0]
    zp = matmul(xp_b, w_in, i, 1024, 1280, F32)
    zs = matmul(xs_b, w_in, i, Bs, 1280, F32)

    oh_p, S_p = hgrn_prompt(zp, lb_all, hgrn_norm, i, B, T)
    oa_p = attn_prompt(zp, B, T)
    mp = branch_merge(oh_p, oa_p, zp, w_branch_a, w_branch_b, i, 1024)
    hp_f, hp_b = matmul_ln(mp, w_out, i, xp_f, ln1_g, ln1_b, i, 512, 512)

    oh_s, S_s = hgrn_step(zs, state_hgrn, lb_all, hgrn_norm, i)
    oa_s = attn_step(zs, caches, i)
    ms = branch_merge(oh_s, oa_s, zs, w_branch_a, w_branch_b, i, Bs)
    hs_f, hs_b = matmul_ln(ms, w_out, i, xs_f, ln1_g, ln1_b, i, Bs, 1024)

    zp3 = zp.reshape(B, T, N_IN)
    kv_p, kv_s = [], []
    for g, (win, _) in enumerate(ATT_GROUPS):
        keep = min(win, T)
        ko, vo = AK_OFF + g * ATT_GW, AV_OFF + g * ATT_GW
        kp = zp3[:, T - keep:, ko:ko + ATT_GW].reshape(B, keep, ATT_HPG, ATT_DH)
        vp = zp3[:, T - keep:, vo:vo + ATT_GW].reshape(B, keep, ATT_HPG, ATT_DH)
        kv_p.append(jnp.stack([kp, vp], axis=2))
        ks = zs[:, ko:ko + ATT_GW].reshape(Bs, 1, ATT_HPG, ATT_DH)
        vs = zs[:, vo:vo + ATT_GW].reshape(Bs, 1, ATT_HPG, ATT_DH)
        kv_s.append(jnp.stack([ks, vs], axis=2))
    return (hp_f, hp_b, hs_f, hs_b, S_p, S_s, kv_p, kv_s)


def _dense_ffn(h_f, h_b, j, ffn_w_gate, ffn_w_up, ffn_w_down, ln2_g, ln2_b, i, tm_up, tm_dn):
    u = swiglu_up(h_b, ffn_w_gate, ffn_w_up, j, tm_up, 512)
    return matmul_ln(u, ffn_w_down, j, h_f, ln2_g, ln2_b, i, tm_dn, 512)


def _moe_ffn(hp_f, hp_b, hs_f, hs_b, j, moe_w_router, moe_b_router, moe_w_gate, moe_w_up, moe_w_down,
             ln2_g, ln2_b, i):
    Np, Bs = hp_f.shape[0], hs_f.shape[0]
    ep, gp = router(hp_f, moe_w_router, moe_b_router, j, 1024)
    es, gs = router(hs_f, moe_w_router, moe_b_router, j, Bs)
    top_e = jnp.concatenate([ep, es], axis=0)
    rows, dest, tile_e, tile_b, tile_n, tile_f = _moe_tables(top_e)
    h_b = jnp.concatenate([hp_b, hs_b], axis=0)
    xs = h_b[rows]
    yb = moe_experts(tile_e, tile_b, tile_n, tile_f, xs, moe_w_gate, moe_w_up, moe_w_down, j)
    y0 = yb[dest[:, 0]]
    y1 = yb[dest[:, 1]]
    xp_f, xp_b = add_ln(hp_f, y0[:Np], y1[:Np], gp, ln2_g, ln2_b, i, 256)
    xs_f, xs_b = add_ln(hs_f, y0[Np:], y1[Np:], gs, ln2_g, ln2_b, i, Bs)
    return xp_f, xp_b, xs_f, xs_b


def kernel(x_prompt, x_sample, cache_kv_w128, cache_kv_w512, cache_kv_w2048, state_hgrn,
           w_in, lb_param, hgrn_norm, w_branch_a, w_branch_b, w_out,
           ln1_g, ln1_b, ln2_g, ln2_b, ffn_w_gate, ffn_w_up, ffn_w_down,
           moe_w_router, moe_b_router, moe_w_gate, moe_w_up, moe_w_down):
    B, T, _ = x_prompt.shape
    Bs = x_sample.shape[0]
    lb_cum = jnp.cumsum(jax.nn.softmax(lb_param.astype(F32), axis=0), axis=0)
    lb_all = lb_cum - lb_cum[0]
    caches = (cache_kv_w128, cache_kv_w512, cache_kv_w2048)
    xp_f = x_prompt.reshape(B * T, D_MODEL)
    xs_f = x_sample.reshape(Bs, D_MODEL)
    xp_b = xp_f.astype(BF16)
    xs_b = xs_f.astype(BF16)
    kv_p = [[] for _ in ATT_GROUPS]
    kv_s = [[] for _ in ATT_GROUPS]
    hg_p, hg_s = [], []
    for i in range(DEPTH):
        hp_f, hp_b, hs_f, hs_b, S_p, S_s, kvp, kvs = _mixer(
            xp_b, xs_b, xp_f, xs_f, i, B, T, caches, state_hgrn, w_in, lb_all, hgrn_norm,
            w_branch_a, w_branch_b, w_out, ln1_g, ln1_b)
        for g in range(N_GROUPS):
            kv_p[g].append(kvp[g])
            kv_s[g].append(kvs[g])
        hg_p.append(S_p)
        hg_s.append(S_s)
        j = i // 2
        if i % 2 == 0:
            xp_f, xp_b = _dense_ffn(hp_f, hp_b, j, ffn_w_gate, ffn_w_up, ffn_w_down, ln2_g, ln2_b, i, 2048, 512)
            xs_f, xs_b = _dense_ffn(hs_f, hs_b, j, ffn_w_gate, ffn_w_up, ffn_w_down, ln2_g, ln2_b, i, Bs, Bs)
        else:
            xp_f, xp_b, xs_f, xs_b = _moe_ffn(hp_f, hp_b, hs_f, hs_b, j, moe_w_router, moe_b_router,
                                              moe_w_gate, moe_w_up, moe_w_down, ln2_g, ln2_b, i)
    return (xp_f.reshape(B, T, D_MODEL), xs_f.reshape(Bs, 1, D_MODEL),
            jnp.stack(kv_p[0], axis=0), jnp.stack(kv_s[0], axis=0),
            jnp.stack(kv_p[1], axis=0), jnp.stack(kv_s[1], axis=0),
            jnp.stack(kv_p[2], axis=0), jnp.stack(kv_s[2], axis=0),
            jnp.stack(hg_p, axis=0), jnp.stack(hg_s, axis=0))
```

```python
import functools

import numpy as np
import jax
import jax.numpy as jnp
from jax import lax
from jax.experimental import pallas as pl
from jax.experimental.pallas import tpu as pltpu

F32 = jnp.float32
BF16 = jnp.bfloat16

D_MODEL = 2048
DEPTH = 4
HG_HEADS = 8
HG_D = 128
HG_WIDTH = HG_HEADS * HG_D
ATT_GROUPS = ((128, 1), (512, 4), (2048, 16))
N_GROUPS = 3
ATT_HPG = 4
ATT_DH = 128
ATT_HEADS = N_GROUPS * ATT_HPG
ATT_GW = ATT_HPG * ATT_DH
ATT_SCALE = ATT_DH ** -0.5
BAND = 128
HQ_OFF, HF_OFF, HI_OFF, HGATE_OFF = 0, 1024, 2048, 3072
AQ_OFF, AK_OFF, AV_OFF = 4096, 5632, 7168
GA_OFF, GB_OFF = 8704, 10752
N_IN = 12800
N_EXPERTS = 8
TOP_K = 2
ALPHA = (2 * DEPTH) ** 0.25
LN_EPS = 1e-5
RMS_EPS = 1e-6
NEG = -1e30
LANES = 128
LOG2E = 1.4426950408889634

VMEM_LIMIT = 56 * 1024 * 1024


def _params(sem):
    return pltpu.CompilerParams(dimension_semantics=sem, vmem_limit_bytes=VMEM_LIMIT)


def _dot(a, b):
    return jnp.dot(a, b, preferred_element_type=F32)


def _dot_nt(a, b):
    return lax.dot_general(a, b, (((1,), (1,)), ((), ())), preferred_element_type=F32)


def _dot_tn(a, b):
    return lax.dot_general(a, b, (((0,), (0,)), ((), ())), preferred_element_type=F32)


def _sigmoid(x):
    return 1.0 / (1.0 + jnp.exp(-x))


def _layer_norm(y, g, b):
    mu = jnp.mean(y, axis=-1, keepdims=True)
    d = y - mu
    var = jnp.mean(d * d, axis=-1, keepdims=True)
    return d * lax.rsqrt(var + LN_EPS) * g + b


def _mm_body(x_ref, w_ref, o_ref, wb_ref):
    @pl.when(pl.program_id(1) == 0)
    def _():
        wb_ref[...] = w_ref[...].astype(BF16)

    o_ref[...] = _dot(x_ref[...], wb_ref[...]).astype(o_ref.dtype)


def matmul(x, w, layer, tm, tn, out_dtype):
    M, K = x.shape
    N = w.shape[-1]
    return pl.pallas_call(
        _mm_body,
        grid=(N // tn, M // tm),
        in_specs=[pl.BlockSpec((tm, K), lambda j, i: (i, 0)),
                  pl.BlockSpec((None, K, tn), lambda j, i: (layer, 0, j))],
        out_specs=pl.BlockSpec((tm, tn), lambda j, i: (i, j)),
        out_shape=jax.ShapeDtypeStruct((M, N), out_dtype),
        scratch_shapes=[pltpu.VMEM((K, tn), BF16)],
        compiler_params=_params(("arbitrary", "arbitrary")),
        name="mm",
    )(x, w)


def _merge_body(oh_ref, oa_ref, ga_ref, gb_ref, wa_ref, wb_ref, o_ref, wab_ref, wbb_ref):
    @pl.when(pl.program_id(1) == 0)
    def _():
        wab_ref[...] = wa_ref[...].astype(BF16)
        wbb_ref[...] = wb_ref[...].astype(BF16)

    ya = _dot(oh_ref[...], wab_ref[...])
    yb = _dot(oa_ref[...], wbb_ref[...])
    o_ref[...] = (_sigmoid(ga_ref[...]) * ya + _sigmoid(gb_ref[...]) * yb).astype(o_ref.dtype)


def branch_merge(oh, oa, z, w_a, w_b, layer, tm):
    M = oh.shape[0]
    tn = 512
    ga0, gb0 = GA_OFF // tn, GB_OFF // tn
    return pl.pallas_call(
        _merge_body,
        grid=(D_MODEL // tn, M // tm),
        in_specs=[pl.BlockSpec((tm, HG_WIDTH), lambda j, i: (i, 0)),
                  pl.BlockSpec((tm, ATT_GW), lambda j, i: (i, 0)),
                  pl.BlockSpec((tm, tn), lambda j, i: (i, ga0 + j)),
                  pl.BlockSpec((tm, tn), lambda j, i: (i, gb0 + j)),
                  pl.BlockSpec((None, HG_WIDTH, tn), lambda j, i: (layer, 0, j)),
                  pl.BlockSpec((None, ATT_GW, tn), lambda j, i: (layer, 0, j))],
        out_specs=pl.BlockSpec((tm, tn), lambda j, i: (i, j)),
        out_shape=jax.ShapeDtypeStruct((M, D_MODEL), BF16),
        scratch_shapes=[pltpu.VMEM((HG_WIDTH, tn), BF16), pltpu.VMEM((ATT_GW, tn), BF16)],
        compiler_params=_params(("arbitrary", "arbitrary")),
        name="branch_merge",
    )(oh, oa, z, z, w_a, w_b)


def _mm_ln_body(a_ref, w_ref, x_ref, g_ref, b_ref, of_ref, ob_ref, acc_ref, wb_ref, *, nk):
    i = pl.program_id(0)
    k = pl.program_id(1)

    @pl.when(i == 0)
    def _():
        wb_ref[k] = w_ref[...].astype(BF16)

    @pl.when(k == 0)
    def _():
        acc_ref[...] = jnp.zeros_like(acc_ref)

    acc_ref[...] += _dot(a_ref[...], wb_ref[k])

    @pl.when(k == nk - 1)
    def _():
        out = _layer_norm(ALPHA * x_ref[...] + acc_ref[...], g_ref[...], b_ref[...])
        of_ref[...] = out
        ob_ref[...] = out.astype(BF16)


def matmul_ln(a, w, layer, x, g, b, ln_layer, tm, tk):
    M, K = a.shape
    nk = K // tk
    g3 = g.reshape(DEPTH, 1, D_MODEL)
    b3 = b.reshape(DEPTH, 1, D_MODEL)
    return pl.pallas_call(
        functools.partial(_mm_ln_body, nk=nk),
        grid=(M // tm, nk),
        in_specs=[pl.BlockSpec((tm, tk), lambda i, k: (i, k)),
                  pl.BlockSpec((None, tk, D_MODEL), lambda i, k: (layer, jnp.where(i == 0, k, nk - 1), 0),
                               pipeline_mode=pl.Buffered(1)),
                  pl.BlockSpec((tm, D_MODEL), lambda i, k: (i, 0)),
                  pl.BlockSpec((None, 1, D_MODEL), lambda i, k: (ln_layer, 0, 0)),
                  pl.BlockSpec((None, 1, D_MODEL), lambda i, k: (ln_layer, 0, 0))],
        out_specs=[pl.BlockSpec((tm, D_MODEL), lambda i, k: (i, 0)),
                   pl.BlockSpec((tm, D_MODEL), lambda i, k: (i, 0))],
        out_shape=[jax.ShapeDtypeStruct((M, D_MODEL), F32),
                   jax.ShapeDtypeStruct((M, D_MODEL), BF16)],
        scratch_shapes=[pltpu.VMEM((tm, D_MODEL), F32), pltpu.VMEM((nk, tk, D_MODEL), BF16)],
        compiler_params=_params(("arbitrary", "arbitrary")),
        name="mm_ln",
    )(a, w, x, g3, b3)


def _add_ln_body(x_ref, y0_ref, y1_ref, gate_ref, g_ref, b_ref, of_ref, ob_ref):
    gate = gate_ref[...]
    y = gate[:, 0:1] * y0_ref[...] + gate[:, 1:2] * y1_ref[...]
    out = _layer_norm(ALPHA * x_ref[...] + y, g_ref[...], b_ref[...])
    of_ref[...] = out
    ob_ref[...] = out.astype(BF16)


def add_ln(x, y0, y1, gate, g, b, ln_layer, tm):
    M = x.shape[0]
    g3 = g.reshape(DEPTH, 1, D_MODEL)
    b3 = b.reshape(DEPTH, 1, D_MODEL)
    row = pl.BlockSpec((tm, D_MODEL), lambda i: (i, 0))
    vec = pl.BlockSpec((None, 1, D_MODEL), lambda i: (ln_layer, 0, 0))
    return pl.pallas_call(
        _add_ln_body,
        grid=(M // tm,),
        in_specs=[row, row, row, pl.BlockSpec((tm, TOP_K), lambda i: (i, 0)), vec, vec],
        out_specs=[row, row],
        out_shape=[jax.ShapeDtypeStruct((M, D_MODEL), F32),
                   jax.ShapeDtypeStruct((M, D_MODEL), BF16)],
        compiler_params=_params(("arbitrary",)),
        name="add_ln",
    )(x, y0, y1, gate, g3, b3)


def _swiglu_up_body(x_ref, wg_ref, wu_ref, o_ref, wgb_ref, wub_ref):
    @pl.when(pl.program_id(1) == 0)
    def _():
        wgb_ref[...] = wg_ref[...].astype(BF16)
        wub_ref[...] = wu_ref[...].astype(BF16)

    x = x_ref[...]
    a = _dot(x, wgb_ref[...])
    u = _dot(x, wub_ref[...])
    o_ref[...] = (a * _sigmoid(a) * u).astype(o_ref.dtype)


def swiglu_up(x, wg, wu, layer, tm, tn):
    M, K = x.shape
    N = wg.shape[-1]
    wspec = pl.BlockSpec((None, K, tn), lambda j, i: (layer, 0, j))
    return pl.pallas_call(
        _swiglu_up_body,
        grid=(N // tn, M // tm),
        in_specs=[pl.BlockSpec((tm, K), lambda j, i: (i, 0)), wspec, wspec],
        out_specs=pl.BlockSpec((tm, tn), lambda j, i: (i, j)),
        out_shape=jax.ShapeDtypeStruct((M, N), BF16),
        scratch_shapes=[pltpu.VMEM((K, tn), BF16), pltpu.VMEM((K, tn), BF16)],
        compiler_params=_params(("arbitrary", "arbitrary")),
        name="swiglu_up",
    )(x, wg, wu)


MOE_TM = 128
MOE_GROUP = 20
MOE_CHUNKS = (16, 8, 4, 2, 1)
MOE_TF = 256


def _moe_body(se_ref, sb_ref, sn_ref, sf_ref, x_hbm, wg_ref, wu_ref, wd_ref, out_hbm,
              xbuf, acc, wgb, wub, wdb, sem_in, sem_out, *, nj):
    s = pl.program_id(0)
    j = pl.program_id(1)
    nb = sn_ref[s]
    fill = sf_ref[s] == 1
    active = (nb > 0) & (sf_ref[s] == 0)
    row0 = sb_ref[s] * MOE_TM

    def rows(c, n=1):
        return pl.ds(pl.multiple_of(c * MOE_TM, MOE_TM), n * MOE_TM)

    def x_copy(c):
        return pltpu.make_async_copy(x_hbm.at[pl.ds(row0 + c * MOE_TM, MOE_TM)], xbuf.at[rows(c)], sem_in)

    def out_copy(c):
        return pltpu.make_async_copy(acc.at[rows(c)], out_hbm.at[pl.ds(row0 + c * MOE_TM, MOE_TM)], sem_out)

    def each_block(fn):
        def body(c, carry):
            fn(c)
            return carry
        lax.fori_loop(0, nb, body, 0)

    def zero(c):
        acc[rows(c), :] = jnp.zeros((MOE_TM, acc.shape[-1]), F32)

    @pl.when((j == 0) & active)
    def _():
        each_block(lambda c: x_copy(c).start())
        each_block(zero)
        each_block(lambda c: x_copy(c).wait())

    def compute(c, n):
        x = xbuf[rows(c, n), :]
        a = _dot(x, wgb[...])
        u = _dot(x, wub[...])
        acc[rows(c, n), :] += _dot((a * _sigmoid(a) * u).astype(BF16), wdb[...])

    @pl.when(active)
    def _():
        wgb[...] = wg_ref[...].astype(BF16)
        wub[...] = wu_ref[...].astype(BF16)
        wdb[...] = wd_ref[...].astype(BF16)
        big = MOE_CHUNKS[0]

        def big_chunk(i, carry):
            compute(i * big, big)
            return carry

        n_big = nb // big
        lax.fori_loop(0, n_big, big_chunk, 0)
        rem = nb - n_big * big
        for n in MOE_CHUNKS[1:]:
            @pl.when((rem & n) != 0)
            def _(n=n):
                compute(n_big * big + (rem & ~(2 * n - 1)), n)

    @pl.when((j == nj - 1) & fill)
    def _():
        each_block(zero)

    @pl.when((j == nj - 1) & (nb > 0))
    def _():
        each_block(lambda c: out_copy(c).start())
        each_block(lambda c: out_copy(c).wait())


def moe_experts(tile_e, tile_b, tile_n, tile_f, xs, wg, wu, wd, layer):
    P, D = xs.shape
    F = wg.shape[-1]
    nj = F // MOE_TF
    S = tile_e.shape[0]

    def jj(s, j, sn, sf):
        return jnp.where((sn[s] > 0) & (sf[s] == 0), j, nj - 1)

    up = pl.BlockSpec((None, None, D, MOE_TF),
                      lambda s, j, se, sb, sn, sf: (layer, se[s], 0, jj(s, j, sn, sf)))
    down = pl.BlockSpec((None, None, MOE_TF, D),
                        lambda s, j, se, sb, sn, sf: (layer, se[s], jj(s, j, sn, sf), 0))
    return pl.pallas_call(
        functools.partial(_moe_body, nj=nj),
        grid_spec=pltpu.PrefetchScalarGridSpec(
            num_scalar_prefetch=4,
            grid=(S, nj),
            in_specs=[pl.BlockSpec(memory_space=pl.ANY), up, up, down],
            out_specs=pl.BlockSpec(memory_space=pl.ANY),
            scratch_shapes=[pltpu.VMEM((MOE_GROUP * MOE_TM, D), BF16),
                            pltpu.VMEM((MOE_GROUP * MOE_TM, D), F32),
                            pltpu.VMEM((D, MOE_TF), BF16), pltpu.VMEM((D, MOE_TF), BF16),
                            pltpu.VMEM((MOE_TF, D), BF16),
                            pltpu.SemaphoreType.DMA(()), pltpu.SemaphoreType.DMA(())]),
        out_shape=jax.ShapeDtypeStruct((P, D), F32),
        compiler_params=_params(("arbitrary", "arbitrary")),
        name="moe_experts",
    )(tile_e, tile_b, tile_n, tile_f, xs, wg, wu, wd)


def _router_body(x_ref, w_ref, b_ref, e_ref, g_ref):
    x = x_ref[...]
    w = w_ref[...]
    xh = x.astype(BF16)
    xl = (x - xh.astype(F32)).astype(BF16)
    wh = w.astype(BF16)
    wl = (w - wh.astype(F32)).astype(BF16)
    logits = _dot(xh, wh) + (_dot(xh, wl) + _dot(xl, wh)) + b_ref[...]
    lane = lax.broadcasted_iota(jnp.int32, logits.shape, 1).astype(F32)
    m1 = jnp.max(logits, axis=-1, keepdims=True)
    i1 = jnp.min(jnp.where(logits == m1, lane, float(LANES)), axis=-1, keepdims=True)
    rest = jnp.where(lane == i1, 2.0 * NEG, logits)
    m2 = jnp.max(rest, axis=-1, keepdims=True)
    i2 = jnp.min(jnp.where(rest == m2, lane, float(LANES)), axis=-1, keepdims=True)
    g1 = 1.0 / (1.0 + jnp.exp(m2 - m1))
    two = lax.broadcasted_iota(jnp.int32, (logits.shape[0], TOP_K), 1)
    e_ref[...] = jnp.where(two == 0, i1, i2).astype(jnp.int32)
    g_ref[...] = jnp.where(two == 0, g1, 1.0 - g1)


def router(x, w_router, b_router, j, tm):
    M = x.shape[0]
    n_moe = w_router.shape[0]
    w_pad = jnp.pad(w_router.astype(F32), ((0, 0), (0, 0), (0, LANES - N_EXPERTS)))
    b3 = jnp.pad(b_router.astype(F32).reshape(n_moe, 1, N_EXPERTS), ((0, 0), (0, 0), (0, LANES - N_EXPERTS)),
                 constant_values=NEG)
    return pl.pallas_call(
        _router_body,
        grid=(M // tm,),
        in_specs=[pl.BlockSpec((tm, D_MODEL), lambda i: (i, 0)),
                  pl.BlockSpec((None, D_MODEL, LANES), lambda i: (j, 0, 0)),
                  pl.BlockSpec((None, 1, LANES), lambda i: (j, 0, 0))],
        out_specs=[pl.BlockSpec((tm, TOP_K), lambda i: (i, 0)),
                   pl.BlockSpec((tm, TOP_K), lambda i: (i, 0))],
        out_shape=[jax.ShapeDtypeStruct((M, TOP_K), jnp.int32),
                   jax.ShapeDtypeStruct((M, TOP_K), F32)],
        compiler_params=_params(("arbitrary",)),
        name="router",
    )(x, w_pad, b3)


HG_CHUNK = 128
HG_HPS = 8
HG_LEVELS = 7
_ROW_B = HG_LEVELS * HG_CHUNK
_ROW_U = _ROW_B + HG_CHUNK


def _hgrn_tables():
    C = HG_CHUNK
    t = np.arange(C)[:, None]
    u = np.arange(C)[None, :]
    sel, low, seg = [], [], []
    for lvl in range(HG_LEVELS):
        w = C >> (lvl + 1)
        ref = (t // (2 * w)) * (2 * w) + w - 1
        lower = (t % (2 * w)) >= w
        sel.append(((u > np.minimum(t, ref)) & (u <= np.maximum(t, ref))).astype(np.float32))
        low.append(np.broadcast_to(lower, (C, HG_HPS * HG_D)).astype(np.float32))
        upper_s = (u % (2 * w)) < w
        seg.append((((t // (2 * w)) == (u // (2 * w))) & lower & upper_s).astype(np.float32))
    sel.append((u <= t).astype(np.float32))
    sel.append((u > t).astype(np.float32))
    sel = np.concatenate(sel, axis=0)
    return (jnp.asarray(np.concatenate([sel, sel], axis=1), BF16),
            jnp.asarray(np.stack(low)), jnp.asarray(np.stack(seg)))


def _hgrn_body(sel_ref, low_ref, seg_ref, hq_ref, hf_ref, hi_ref, hg_ref, lb_ref, gn_ref,
               o_ref, s_ref, st_ref, *, nc):
    c = pl.program_id(2)

    @pl.when(c == 0)
    def _():
        st_ref[...] = jnp.zeros_like(st_ref)

    C = HG_CHUNK
    heads = [slice(hh * HG_D, (hh + 1) * HG_D) for hh in range(HG_HPS)]
    q = hq_ref[...]
    hf = hf_ref[...]
    v = hi_ref[...]
    lb = lb_ref[...]
    g = jnp.log(lb + (1.0 - lb) * _sigmoid(hf)) * LOG2E
    kk = (1.0 - lb) * _sigmoid(-hf)
    g_hi = g.astype(BF16)
    g_lo = (g - g_hi.astype(F32)).astype(BF16)
    ex = jnp.exp2(_dot(sel_ref[...], jnp.concatenate([g_hi, g_lo], axis=0)))
    eb = ex[_ROW_B:_ROW_B + C]
    eu = ex[_ROW_U:_ROW_U + C]
    eb_end = eb[C - 1:C, :]
    v_b = v.astype(BF16)

    qe = (q * eb).astype(BF16)
    sts = [st_ref[hh] for hh in range(HG_HPS)]
    o = [_dot_nt(qe[:, hs], sts[hh].astype(BF16)) for hh, hs in enumerate(heads)]
    a = [jnp.zeros((C, C), F32) for _ in heads]
    for lvl in range(HG_LEVELS):
        e = ex[lvl * C:(lvl + 1) * C]
        xt = (jnp.where(low_ref[lvl] > 0.5, q, kk) * e).astype(BF16)
        seg = seg_ref[lvl]
        for hh, hs in enumerate(heads):
            a[hh] = a[hh] + _dot_nt(xt[:, hs], xt[:, hs]) * seg
    kd = (kk * eu).astype(BF16)
    qk = q * kk
    for hh, hs in enumerate(heads):
        o[hh] = o[hh] + _dot(a[hh].astype(BF16), v_b[:, hs])
        o[hh] = o[hh] + jnp.sum(qk[:, hs], axis=-1, keepdims=True) * v[:, hs]
        st_ref[hh] = sts[hh] * eb_end[:, hs] + _dot_tn(v_b[:, hs], kd[:, hs])

    hg = hg_ref[...]
    gate = hg * _sigmoid(hg)
    for hh, hs in enumerate(heads):
        on = o[hh] * lax.rsqrt(jnp.mean(o[hh] * o[hh], axis=-1, keepdims=True) + RMS_EPS) * gn_ref[...]
        o_ref[:, hs] = (on * gate[:, hs]).astype(o_ref.dtype)

    @pl.when(c == nc - 1)
    def _():
        for hh in range(HG_HPS):
            s_ref[hh] = st_ref[hh].T


def hgrn_prompt(z, lb, gnorm, layer, B, T):
    C = HG_CHUNK
    nc = T // C
    W = HG_HPS * HG_D
    sel, low, seg = _hgrn_tables()
    lb3 = lb.reshape(DEPTH, 1, HG_WIDTH)
    gn3 = gnorm.reshape(DEPTH, 1, HG_D)

    def col(off):
        return pl.BlockSpec((C, W), lambda b, h, c: (b * nc + c, off // W + h))

    return pl.pallas_call(
        functools.partial(_hgrn_body, nc=nc),
        grid=(B, HG_HEADS // HG_HPS, nc),
        in_specs=[pl.BlockSpec(sel.shape, lambda b, h, c: (0, 0)),
                  pl.BlockSpec(low.shape, lambda b, h, c: (0, 0, 0)),
                  pl.BlockSpec(seg.shape, lambda b, h, c: (0, 0, 0)),
                  col(HQ_OFF), col(HF_OFF), col(HI_OFF), col(HGATE_OFF),
                  pl.BlockSpec((None, 1, W), lambda b, h, c: (layer, 0, h)),
                  pl.BlockSpec((None, 1, HG_D), lambda b, h, c: (layer, 0, 0))],
        out_specs=[pl.BlockSpec((C, W), lambda b, h, c: (b * nc + c, h)),
                   pl.BlockSpec((None, HG_HPS, HG_D, HG_D), lambda b, h, c: (b, h, 0, 0))],
        out_shape=[jax.ShapeDtypeStruct((B * T, HG_WIDTH), BF16),
                   jax.ShapeDtypeStruct((B, HG_HEADS, HG_D, HG_D), F32)],
        scratch_shapes=[pltpu.VMEM((HG_HPS, HG_D, HG_D), F32)],
        compiler_params=_params(("parallel", "parallel", "arbitrary")),
        name="hgrn_prompt",
    )(sel, low, seg, z, z, z, z, lb3, gn3)


def _hgrn_step_body(s0_ref, qc_ref, fc_ref, vr_ref, gr_ref, lbc_ref, gn_ref, o_ref, s_ref):
    for h in range(HG_HEADS):
        s0 = s0_ref[h]
        hf = fc_ref[h]
        lb = lbc_ref[h]
        f = lb + (1.0 - lb) * _sigmoid(hf)
        kk = (1.0 - lb) * _sigmoid(-hf)
        v = vr_ref[h]
        s_new = f * s0 + kk * v
        s_ref[h] = s_new
        o = jnp.sum(qc_ref[h] * s_new, axis=0, keepdims=True)
        hg = gr_ref[h]
        on = o * lax.rsqrt(jnp.mean(o * o, axis=-1, keepdims=True) + RMS_EPS) * gn_ref[...]
        o_ref[h] = (on * (hg * _sigmoid(hg))).astype(o_ref.dtype)


def hgrn_step(zs, state, lb, gnorm, layer):
    Bs = zs.shape[0]
    heads = lambda off: zs[:, off:off + HG_WIDTH].reshape(Bs, HG_HEADS, HG_D)
    qc = heads(HQ_OFF)[..., None]
    fc = heads(HF_OFF)[..., None]
    vr = heads(HI_OFF)[:, :, None, :]
    gr = heads(HGATE_OFF)[:, :, None, :]
    lbc = lb.reshape(DEPTH, HG_HEADS, HG_D, 1)
    gn3 = gnorm.reshape(DEPTH, 1, HG_D)
    colspec = pl.BlockSpec((None, HG_HEADS, HG_D, 1), lambda b: (b, 0, 0, 0))
    rowspec = pl.BlockSpec((None, HG_HEADS, 1, HG_D), lambda b: (b, 0, 0, 0))
    o, s = pl.pallas_call(
        _hgrn_step_body,
        grid=(Bs,),
        in_specs=[pl.BlockSpec((None, None, HG_HEADS, HG_D, HG_D), lambda b: (layer, b, 0, 0, 0)),
                  colspec, colspec, rowspec, rowspec,
                  pl.BlockSpec((None, HG_HEADS, HG_D, 1), lambda b: (layer, 0, 0, 0)),
                  pl.BlockSpec((None, 1, HG_D), lambda b: (layer, 0, 0))],
        out_specs=[rowspec,
                   pl.BlockSpec((None, HG_HEADS, HG_D, HG_D), lambda b: (b, 0, 0, 0))],
        out_shape=[jax.ShapeDtypeStruct((Bs, HG_HEADS, 1, HG_D), BF16),
                   jax.ShapeDtypeStruct((Bs, HG_HEADS, HG_D, HG_D), F32)],
        compiler_params=_params(("arbitrary",)),
        name="hgrn_step",
    )(state, qc, fc, vr, gr, lbc, gn3)
    return o.reshape(Bs, HG_WIDTH), s


def _slope(g, h):
    return float(2.0 ** (-8.0 * (g * ATT_HPG + h + 1) / ATT_HEADS))


def _band_rows(start, dil):
    return pl.ds(start, BAND, stride=dil) if dil > 1 else pl.ds(start, BAND)


ATT_UNITS = 4


def _attn_units(q_ref, k_ref, v_ref, o_scr, l_scr, starts, prev_starts, prev_offs, dil, pen, has_prev):
    n = range(len(starts))
    row = lax.broadcasted_iota(jnp.int32, (BAND, BAND), 0)
    colj = lax.broadcasted_iota(jnp.int32, (BAND, BAND), 1)
    pen_c = pen * (row - colj).astype(F32)
    rows_c = [_band_rows(s, dil) for s in starts]
    q = [q_ref[r, :].astype(BF16) for r in rows_c]
    kc = [k_ref[r, :].astype(BF16) for r in rows_c]
    vc = [v_ref[r, :].astype(BF16) for r in rows_c]
    s_c = [jnp.where(colj <= row, _dot_nt(q[i], kc[i]) * ATT_SCALE - pen_c, NEG) for i in n]
    m = [jnp.max(s, axis=-1, keepdims=True) for s in s_c]
    if has_prev:
        pen_p = pen_c + pen * float(BAND)
        rows_p = [_band_rows(s, dil) for s in prev_starts]
        kp = [k_ref[r, :].astype(BF16) for r in rows_p]
        vp = [v_ref[r, :].astype(BF16) for r in rows_p]
        s_p = [jnp.where(colj >= row, _dot_nt(q[i], kp[i]) * ATT_SCALE - pen_p, NEG) + prev_offs[i] for i in n]
        m = [jnp.maximum(m[i], jnp.max(s_p[i], axis=-1, keepdims=True)) for i in n]
    p_c = [jnp.exp(s_c[i] - m[i]) for i in n]
    l = [jnp.sum(p, axis=-1, keepdims=True) for p in p_c]
    acc = [_dot(p_c[i].astype(BF16), vc[i]) for i in n]
    if has_prev:
        p_p = [jnp.exp(s_p[i] - m[i]) for i in n]
        l = [l[i] + jnp.sum(p_p[i], axis=-1, keepdims=True) for i in n]
        acc = [acc[i] + _dot(p_p[i].astype(BF16), vp[i]) for i in n]
    for i in n:
        o_scr[rows_c[i], :] = acc[i] / l[i]
        l_scr[rows_c[i], :] = jnp.broadcast_to(m[i] + jnp.log(l[i]), (BAND, ATT_DH))


def _attn_body(slope_ref, *refs, T):
    qkv = refs[:3 * N_GROUPS]
    out_ref, o_scr, l_scr = refs[3 * N_GROUPS:]
    h = pl.program_id(1)
    for g, (_, dil) in enumerate(ATT_GROUPS):
        q_ref, k_ref, v_ref = qkv[3 * g:3 * g + 3]
        pen = slope_ref[g * ATT_HPG + h] * float(dil)
        nqb = T // dil // BAND

        def units(it, carry, q_ref=q_ref, k_ref=k_ref, v_ref=v_ref, g=g, dil=dil, pen=pen, nqb=nqb):
            starts, prev_starts, prev_offs = [], [], []
            for u in range(ATT_UNITS):
                idx = it * ATT_UNITS + u
                r = idx // nqb
                qb = idx - r * nqb
                start = r + qb * (BAND * dil)
                prev_start = jnp.maximum(start - BAND * dil, 0)
                if dil == 1:
                    start = pl.multiple_of(start, BAND)
                    prev_start = pl.multiple_of(prev_start, BAND)
                starts.append(start)
                prev_starts.append(prev_start)
                prev_offs.append(jnp.where(qb > 0, 0.0, NEG))
            _attn_units(q_ref, k_ref, v_ref, o_scr.at[g], l_scr.at[g], starts, prev_starts, prev_offs,
                        dil, pen, has_prev=nqb > 1)
            return carry

        lax.fori_loop(0, dil * nqb // ATT_UNITS, units, 0)

    def merge(i, carry):
        rs = pl.ds(pl.multiple_of(i * BAND, BAND), BAND)
        a0, a1, a2 = l_scr[0, rs, :], l_scr[1, rs, :], l_scr[2, rs, :]
        m = jnp.maximum(jnp.maximum(a0, a1), a2)
        w0, w1, w2 = jnp.exp(a0 - m), jnp.exp(a1 - m), jnp.exp(a2 - m)
        out = (w0 * o_scr[0, rs, :] + w1 * o_scr[1, rs, :] + w2 * o_scr[2, rs, :]) / (w0 + w1 + w2)
        out_ref[rs, :] = out.astype(out_ref.dtype)
        return carry

    lax.fori_loop(0, T // BAND, merge, 0)


def attn_prompt(z, B, T):
    slopes = jnp.asarray([_slope(g, h) for g in range(N_GROUPS) for h in range(ATT_HPG)], F32)

    def col(off, g):
        c0 = off // ATT_DH + g * ATT_HPG
        return pl.BlockSpec((T, ATT_DH), lambda b, h: (b, c0 + h))

    in_specs = [pl.BlockSpec(memory_space=pltpu.SMEM)]
    for g in range(N_GROUPS):
        in_specs += [col(AQ_OFF, g), col(AK_OFF, g), col(AV_OFF, g)]
    return pl.pallas_call(
        functools.partial(_attn_body, T=T),
        grid=(B, ATT_HPG),
        in_specs=in_specs,
        out_specs=pl.BlockSpec((T, ATT_DH), lambda b, h: (b, h)),
        out_shape=jax.ShapeDtypeStruct((B * T, ATT_GW), BF16),
        scratch_shapes=[pltpu.VMEM((N_GROUPS, T, ATT_DH), F32), pltpu.VMEM((N_GROUPS, T, ATT_DH), F32)],
        compiler_params=_params(("parallel", "arbitrary")),
        name="attn_prompt",
    )(slopes, *([z] * (3 * N_GROUPS)))


def _attn_step_body(z_ref, c0_ref, c1_ref, c2_ref, o_ref):
    caches = (c0_ref, c1_ref, c2_ref)
    steps = float(BAND) - lax.broadcasted_iota(jnp.int32, (BAND, 1, 1), 0).astype(F32)
    head = lax.broadcasted_iota(jnp.int32, (ATT_HPG, 1), 0).astype(F32)
    outs, lses = [], []
    for g, (_, dil) in enumerate(ATT_GROUPS):
        slope = jnp.exp2((head + float(g * ATT_HPG + 1)) * (-8.0 / ATT_HEADS))
        pen = slope * float(dil)
        r0 = g * ATT_HPG
        q = z_ref[AQ_OFF // ATT_DH + r0:AQ_OFF // ATT_DH + r0 + ATT_HPG, :]
        kn = z_ref[AK_OFF // ATT_DH + r0:AK_OFF // ATT_DH + r0 + ATT_HPG, :]
        vn = z_ref[AV_OFF // ATT_DH + r0:AV_OFF // ATT_DH + r0 + ATT_HPG, :]
        kc = caches[g][:, 0]
        vc = caches[g][:, 1]
        s_c = jnp.sum(kc * q[None], axis=-1, keepdims=True) * ATT_SCALE - pen[None] * steps
        s_n = jnp.sum(kn * q, axis=-1, keepdims=True) * ATT_SCALE
        m = jnp.maximum(jnp.max(s_c, axis=0), s_n)
        p_c = jnp.exp(s_c - m[None])
        p_n = jnp.exp(s_n - m)
        l = jnp.sum(p_c, axis=0) + p_n
        outs.append((jnp.sum(p_c * vc, axis=0) + p_n * vn) / l)
        lses.append(m + jnp.log(l))
    mm = jnp.maximum(jnp.maximum(lses[0], lses[1]), lses[2])
    ws = [jnp.exp(x - mm) for x in lses]
    out = (ws[0] * outs[0] + ws[1] * outs[1] + ws[2] * outs[2]) / (ws[0] + ws[1] + ws[2])
    o_ref[...] = out.astype(o_ref.dtype)


def attn_step(zs, caches, layer):
    Bs = zs.shape[0]
    zr = zs.reshape(Bs, N_IN // ATT_DH, ATT_DH)
    cache_specs, cache_args = [], []
    for g, (win, dil) in enumerate(ATT_GROUPS):
        c = caches[g]
        assert c.shape[2] == win and win // dil == BAND
        cache_args.append(c.reshape(DEPTH, Bs, BAND, dil, 2, ATT_HPG, ATT_DH))
        cache_specs.append(pl.BlockSpec((None, None, BAND, None, 2, ATT_HPG, ATT_DH),
                                        lambda b: (layer, b, 0, 0, 0, 0, 0)))
    o = pl.pallas_call(
        _attn_step_body,
        grid=(Bs,),
        in_specs=[pl.BlockSpec((None, N_IN // ATT_DH, ATT_DH), lambda b: (b, 0, 0))] + cache_specs,
        out_specs=pl.BlockSpec((None, ATT_HPG, ATT_DH), lambda b: (b, 0, 0)),
        out_shape=jax.ShapeDtypeStruct((Bs, ATT_HPG, ATT_DH), BF16),
        compiler_params=_params(("arbitrary",)),
        name="attn_step",
    )(zr, *cache_args)
    return o.reshape(Bs, ATT_GW)


KV_SLOTS = 2 * ATT_HPG


def _kv_rows_body(k_ref, v_ref, o_ref):
    tk = k_ref.shape[0]
    for c, ref in enumerate((k_ref, v_ref)):
        for h in range(ATT_HPG):
            o_ref[pl.ds(c * ATT_HPG + h, tk, stride=KV_SLOTS), :] = ref[:, h * ATT_DH:(h + 1) * ATT_DH]


def kv_rows(z, group, B, T):
    keep = min(ATT_GROUPS[group][0], T)
    tk = min(keep, 512)
    nt = keep // tk
    first = (T - keep) // tk

    def col(off):
        c0 = off // ATT_GW + group
        return pl.BlockSpec((tk, ATT_GW), lambda b, t: (b * (T // tk) + first + t, c0))

    return pl.pallas_call(
        _kv_rows_body,
        grid=(B, nt),
        in_specs=[col(AK_OFF), col(AV_OFF)],
        out_specs=pl.BlockSpec((tk * KV_SLOTS, ATT_DH), lambda b, t: (b * nt + t, 0)),
        out_shape=jax.ShapeDtypeStruct((B * keep * KV_SLOTS, ATT_DH), F32),
        compiler_params=_params(("arbitrary", "arbitrary")),
        name="kv_rows",
    )(z, z)


def _moe_tables(top_e):
    N = top_e.shape[0]
    A = N * TOP_K
    tm, G = MOE_TM, MOE_GROUP
    n_blocks = -(-A // tm) + N_EXPERTS
    P = n_blocks * tm
    n_tiles = -(-n_blocks // G) + N_EXPERTS + 1
    i32 = jnp.int32
    flat_e = top_e.reshape(A)
    onehot = (flat_e[:, None] == jnp.arange(N_EXPERTS)[None, :]).astype(i32)
    rank = jnp.take_along_axis(jnp.cumsum(onehot, axis=0), flat_e[:, None], axis=1)[:, 0] - 1
    counts = jnp.sum(onehot, axis=0)
    blocks = (counts + tm - 1) // tm
    bend = jnp.cumsum(blocks)
    bstart = bend - blocks
    dest = (bstart[flat_e] * tm + rank).astype(i32)
    rows = jnp.zeros((P,), i32).at[dest].set(jnp.arange(A, dtype=i32) // TOP_K)
    tiles = (blocks + G - 1) // G
    tend = jnp.cumsum(tiles)
    tstart = tend - tiles
    n_used = tend[-1]
    t = jnp.arange(n_tiles, dtype=i32)
    last_e = jnp.max(jnp.where(blocks > 0, jnp.arange(N_EXPERTS), 0))
    e_of = jnp.minimum(jnp.searchsorted(tend, t, side='right'), N_EXPERTS - 1)
    k = t - tstart[e_of]
    used = t < n_used
    tile_e = jnp.where(used, e_of, last_e)
    tile_b = jnp.where(used, bstart[e_of] + k * G, 0)
    tile_n = jnp.where(used, jnp.minimum(G, blocks[e_of] - k * G), 0)
    is_fill = t == n_used
    tile_b = jnp.where(is_fill, bend[-1], tile_b)
    tile_n = jnp.where(is_fill, n_blocks - bend[-1], tile_n)
    return (rows, dest.reshape(N, TOP_K), tile_e.astype(i32), tile_b.astype(i32), tile_n.astype(i32),
            is_fill.astype(i32))


def _mixer(xp_b, xs_b, xp_f, xs_f, i, B, T, caches, state_hgrn, w_in, lb_all, hgrn_norm,
           w_branch_a, w_branch_b, w_out, ln1_g, ln1_b):
    Np = B * T
    Bs = xs_b.shape[0]
    zp = matmul(xp_b, w_in, i, 1024, 1280, F32)
    zs = matmul(xs_b, w_in, i, Bs, 1280, F32)

    oh_p, S_p = hgrn_prompt(zp, lb_all, hgrn_norm, i, B, T)
    oa_p = attn_prompt(zp, B, T)
    mp = branch_merge(oh_p, oa_p, zp, w_branch_a, w_branch_b, i, 1024)
    hp_f, hp_b = matmul_ln(mp, w_out, i, xp_f, ln1_g, ln1_b, i, 512, 512)

    oh_s, S_s = hgrn_step(zs, state_hgrn, lb_all, hgrn_norm, i)
    oa_s = attn_step(zs, caches, i)
    ms = branch_merge(oh_s, oa_s, zs, w_branch_a, w_branch_b, i, Bs)
    hs_f, hs_b = matmul_ln(ms, w_out, i, xs_f, ln1_g, ln1_b, i, Bs, 1024)

    kv_p, kv_s = [], []
    for g in range(N_GROUPS):
        ko, vo = AK_OFF + g * ATT_GW, AV_OFF + g * ATT_GW
        kv_p.append(kv_rows(zp, g, B, T))
        ks = zs[:, ko:ko + ATT_GW].reshape(Bs, 1, ATT_HPG, ATT_DH)
        vs = zs[:, vo:vo + ATT_GW].reshape(Bs, 1, ATT_HPG, ATT_DH)
        kv_s.append(jnp.stack([ks, vs], axis=2))
    return (hp_f, hp_b, hs_f, hs_b, S_p, S_s, kv_p, kv_s)


def _dense_ffn(h_f, h_b, j, ffn_w_gate, ffn_w_up, ffn_w_down, ln2_g, ln2_b, i, tm_up, tm_dn):
    u = swiglu_up(h_b, ffn_w_gate, ffn_w_up, j, tm_up, 512)
    return matmul_ln(u, ffn_w_down, j, h_f, ln2_g, ln2_b, i, tm_dn, 512)


def _moe_ffn(hp_f, hp_b, hs_f, hs_b, j, moe_w_router, moe_b_router, moe_w_gate, moe_w_up, moe_w_down,
             ln2_g, ln2_b, i):
    Np, Bs = hp_f.shape[0], hs_f.shape[0]
    ep, gp = router(hp_f, moe_w_router, moe_b_router, j, 1024)
    es, gs = router(hs_f, moe_w_router, moe_b_router, j, Bs)
    top_e = jnp.concatenate([ep, es], axis=0)
    rows, dest, tile_e, tile_b, tile_n, tile_f = _moe_tables(top_e)
    h_b = jnp.concatenate([hp_b, hs_b], axis=0)
    xs = h_b[rows]
    yb = moe_experts(tile_e, tile_b, tile_n, tile_f, xs, moe_w_gate, moe_w_up, moe_w_down, j)
    y0 = yb[dest[:, 0]]
    y1 = yb[dest[:, 1]]
    xp_f, xp_b = add_ln(hp_f, y0[:Np], y1[:Np], gp, ln2_g, ln2_b, i, 256)
    xs_f, xs_b = add_ln(hs_f, y0[Np:], y1[Np:], gs, ln2_g, ln2_b, i, Bs)
    return xp_f, xp_b, xs_f, xs_b


def kernel(x_prompt, x_sample, cache_kv_w128, cache_kv_w512, cache_kv_w2048, state_hgrn,
           w_in, lb_param, hgrn_norm, w_branch_a, w_branch_b, w_out,
           ln1_g, ln1_b, ln2_g, ln2_b, ffn_w_gate, ffn_w_up, ffn_w_down,
           moe_w_router, moe_b_router, moe_w_gate, moe_w_up, moe_w_down):
    B, T, _ = x_prompt.shape
    Bs = x_sample.shape[0]
    lb_cum = jnp.cumsum(jax.nn.softmax(lb_param.astype(F32), axis=0), axis=0)
    lb_all = lb_cum - lb_cum[0]
    caches = (cache_kv_w128, cache_kv_w512, cache_kv_w2048)
    xp_f = x_prompt.reshape(B * T, D_MODEL)
    xs_f = x_sample.reshape(Bs, D_MODEL)
    xp_b = xp_f.astype(BF16)
    xs_b = xs_f.astype(BF16)
    kv_p = [[] for _ in ATT_GROUPS]
    kv_s = [[] for _ in ATT_GROUPS]
    hg_p, hg_s = [], []
    for i in range(DEPTH):
        hp_f, hp_b, hs_f, hs_b, S_p, S_s, kvp, kvs = _mixer(
            xp_b, xs_b, xp_f, xs_f, i, B, T, caches, state_hgrn, w_in, lb_all, hgrn_norm,
            w_branch_a, w_branch_b, w_out, ln1_g, ln1_b)
        for g in range(N_GROUPS):
            kv_p[g].append(kvp[g])
            kv_s[g].append(kvs[g])
        hg_p.append(S_p)
        hg_s.append(S_s)
        j = i // 2
        if i % 2 == 0:
            xp_f, xp_b = _dense_ffn(hp_f, hp_b, j, ffn_w_gate, ffn_w_up, ffn_w_down, ln2_g, ln2_b, i, 2048, 512)
            xs_f, xs_b = _dense_ffn(hs_f, hs_b, j, ffn_w_gate, ffn_w_up, ffn_w_down, ln2_g, ln2_b, i, Bs, Bs)
        else:
            xp_f, xp_b, xs_f, xs_b = _moe_ffn(hp_f, hp_b, hs_f, hs_b, j, moe_w_router, moe_b_router,
                                              moe_w_gate, moe_w_up, moe_w_down, ln2_g, ln2_b, i)
    def prompt_kv(g):
        keep = min(ATT_GROUPS[g][0], T)
        return jnp.stack(kv_p[g], axis=0).reshape(DEPTH, B, keep, 2, ATT_HPG, ATT_DH)

    return (xp_f.reshape(B, T, D_MODEL), xs_f.reshape(Bs, 1, D_MODEL),
            prompt_kv(0), jnp.stack(kv_s[0], axis=0),
            prompt_kv(1), jnp.stack(kv_s[1], axis=0),
            prompt_kv(2), jnp.stack(kv_s[2], axis=0),
            jnp.stack(hg_p, axis=0), jnp.stack(hg_s, axis=0))
```

```python
import functools

import numpy as np
import jax
import jax.numpy as jnp
from jax import lax
from jax.experimental import pallas as pl
from jax.experimental.pallas import tpu as pltpu

F32 = jnp.float32
BF16 = jnp.bfloat16

D_MODEL = 2048
DEPTH = 4
HG_HEADS = 8
HG_D = 128
HG_WIDTH = HG_HEADS * HG_D
ATT_GROUPS = ((128, 1), (512, 4), (2048, 16))
N_GROUPS = 3
ATT_HPG = 4
ATT_DH = 128
ATT_HEADS = N_GROUPS * ATT_HPG
ATT_GW = ATT_HPG * ATT_DH
ATT_SCALE = ATT_DH ** -0.5
BAND = 128
HQ_OFF, HF_OFF, HI_OFF, HGATE_OFF = 0, 1024, 2048, 3072
AQ_OFF, AK_OFF, AV_OFF = 4096, 5632, 7168
GA_OFF, GB_OFF = 8704, 10752
N_IN = 12800
N_EXPERTS = 8
TOP_K = 2
ALPHA = (2 * DEPTH) ** 0.25
LN_EPS = 1e-5
RMS_EPS = 1e-6
NEG = -1e30
LANES = 128
LOG2E = 1.4426950408889634

VMEM_LIMIT = 56 * 1024 * 1024


def _params(sem):
    return pltpu.CompilerParams(dimension_semantics=sem, vmem_limit_bytes=VMEM_LIMIT)


def _dot(a, b):
    return jnp.dot(a, b, preferred_element_type=F32)


def _dot_nt(a, b):
    return lax.dot_general(a, b, (((1,), (1,)), ((), ())), preferred_element_type=F32)


def _dot_tn(a, b):
    return lax.dot_general(a, b, (((0,), (0,)), ((), ())), preferred_element_type=F32)


def _sigmoid(x):
    return 1.0 / (1.0 + jnp.exp(-x))


def _layer_norm(y, g, b):
    mu = jnp.mean(y, axis=-1, keepdims=True)
    d = y - mu
    var = jnp.mean(d * d, axis=-1, keepdims=True)
    return d * lax.rsqrt(var + LN_EPS) * g + b


def _mm_body(x_ref, xs_ref, w_ref, o_ref, os_ref, wb_ref, *, nm):
    i = pl.program_id(1)

    @pl.when(i == 0)
    def _():
        wb_ref[...] = w_ref[...].astype(BF16)

    @pl.when(i < nm)
    def _():
        o_ref[...] = _dot(x_ref[...], wb_ref[...]).astype(o_ref.dtype)

    @pl.when(i == nm)
    def _():
        os_ref[...] = _dot(xs_ref[...], wb_ref[...]).astype(os_ref.dtype)


def _tail_row(nm):
    return lambda j, i: (jnp.minimum(i, nm - 1), 0)


def matmul(x, xs, w, layer, tm, tn, out_dtype):
    M, K = x.shape
    Ms = xs.shape[0]
    N = w.shape[-1]
    nm = M // tm
    return pl.pallas_call(
        functools.partial(_mm_body, nm=nm),
        grid=(N // tn, nm + 1),
        in_specs=[pl.BlockSpec((tm, K), _tail_row(nm)),
                  pl.BlockSpec((Ms, K), lambda j, i: (0, 0)),
                  pl.BlockSpec((None, K, tn), lambda j, i: (layer, 0, j))],
        out_specs=[pl.BlockSpec((tm, tn), lambda j, i: (jnp.minimum(i, nm - 1), j)),
                   pl.BlockSpec((Ms, tn), lambda j, i: (0, j))],
        out_shape=[jax.ShapeDtypeStruct((M, N), out_dtype), jax.ShapeDtypeStruct((Ms, N), out_dtype)],
        scratch_shapes=[pltpu.VMEM((K, tn), BF16)],
        compiler_params=_params(("arbitrary", "arbitrary")),
        name="mm",
    )(x, xs, w)


def _merge_body(oh_ref, oa_ref, ga_ref, gb_ref, ohs_ref, oas_ref, gas_ref, gbs_ref, wa_ref, wb_ref,
                o_ref, os_ref, wab_ref, wbb_ref, *, nm):
    i = pl.program_id(1)

    @pl.when(i == 0)
    def _():
        wab_ref[...] = wa_ref[...].astype(BF16)
        wbb_ref[...] = wb_ref[...].astype(BF16)

    def merged(oh, oa, ga, gb, out):
        ya = _dot(oh[...], wab_ref[...])
        yb = _dot(oa[...], wbb_ref[...])
        out[...] = (_sigmoid(ga[...]) * ya + _sigmoid(gb[...]) * yb).astype(out.dtype)

    @pl.when(i < nm)
    def _():
        merged(oh_ref, oa_ref, ga_ref, gb_ref, o_ref)

    @pl.when(i == nm)
    def _():
        merged(ohs_ref, oas_ref, gas_ref, gbs_ref, os_ref)


def branch_merge(oh, oa, z, ohs, oas, zs, w_a, w_b, layer, tm):
    M = oh.shape[0]
    Ms = ohs.shape[0]
    tn = 512
    nm = M // tm
    ga0, gb0 = GA_OFF // tn, GB_OFF // tn
    row = _tail_row(nm)
    return pl.pallas_call(
        functools.partial(_merge_body, nm=nm),
        grid=(D_MODEL // tn, nm + 1),
        in_specs=[pl.BlockSpec((tm, HG_WIDTH), row),
                  pl.BlockSpec((tm, ATT_GW), row),
                  pl.BlockSpec((tm, tn), lambda j, i: (jnp.minimum(i, nm - 1), ga0 + j)),
                  pl.BlockSpec((tm, tn), lambda j, i: (jnp.minimum(i, nm - 1), gb0 + j)),
                  pl.BlockSpec((Ms, HG_WIDTH), lambda j, i: (0, 0)),
                  pl.BlockSpec((Ms, ATT_GW), lambda j, i: (0, 0)),
                  pl.BlockSpec((Ms, tn), lambda j, i: (0, ga0 + j)),
                  pl.BlockSpec((Ms, tn), lambda j, i: (0, gb0 + j)),
                  pl.BlockSpec((None, HG_WIDTH, tn), lambda j, i: (layer, 0, j)),
                  pl.BlockSpec((None, ATT_GW, tn), lambda j, i: (layer, 0, j))],
        out_specs=[pl.BlockSpec((tm, tn), lambda j, i: (jnp.minimum(i, nm - 1), j)),
                   pl.BlockSpec((Ms, tn), lambda j, i: (0, j))],
        out_shape=[jax.ShapeDtypeStruct((M, D_MODEL), BF16), jax.ShapeDtypeStruct((Ms, D_MODEL), BF16)],
        scratch_shapes=[pltpu.VMEM((HG_WIDTH, tn), BF16), pltpu.VMEM((ATT_GW, tn), BF16)],
        compiler_params=_params(("arbitrary", "arbitrary")),
        name="branch_merge",
    )(oh, oa, z, z, ohs, oas, zs, zs, w_a, w_b)


def _mm_ln_body(a_ref, w_ref, x_ref, g_ref, b_ref, of_ref, ob_ref, acc_ref, wb_ref, *, nk):
    i = pl.program_id(0)
    k = pl.program_id(1)

    @pl.when(i == 0)
    def _():
        wb_ref[k] = w_ref[...].astype(BF16)

    @pl.when(k == 0)
    def _():
        acc_ref[...] = jnp.zeros_like(acc_ref)

    acc_ref[...] += _dot(a_ref[...], wb_ref[k])

    @pl.when(k == nk - 1)
    def _():
        out = _layer_norm(ALPHA * x_ref[...] + acc_ref[...], g_ref[...], b_ref[...])
        of_ref[...] = out
        ob_ref[...] = out.astype(BF16)


def matmul_ln(a, w, layer, x, g, b, ln_layer, tm, tk):
    M, K = a.shape
    nk = K // tk
    g3 = g.reshape(DEPTH, 1, D_MODEL)
    b3 = b.reshape(DEPTH, 1, D_MODEL)
    return pl.pallas_call(
        functools.partial(_mm_ln_body, nk=nk),
        grid=(M // tm, nk),
        in_specs=[pl.BlockSpec((tm, tk), lambda i, k: (i, k)),
                  pl.BlockSpec((None, tk, D_MODEL), lambda i, k: (layer, jnp.where(i == 0, k, nk - 1), 0),
                               pipeline_mode=pl.Buffered(1)),
                  pl.BlockSpec((tm, D_MODEL), lambda i, k: (i, 0)),
                  pl.BlockSpec((None, 1, D_MODEL), lambda i, k: (ln_layer, 0, 0)),
                  pl.BlockSpec((None, 1, D_MODEL), lambda i, k: (ln_layer, 0, 0))],
        out_specs=[pl.BlockSpec((tm, D_MODEL), lambda i, k: (i, 0)),
                   pl.BlockSpec((tm, D_MODEL), lambda i, k: (i, 0))],
        out_shape=[jax.ShapeDtypeStruct((M, D_MODEL), F32),
                   jax.ShapeDtypeStruct((M, D_MODEL), BF16)],
        scratch_shapes=[pltpu.VMEM((tm, D_MODEL), F32), pltpu.VMEM((nk, tk, D_MODEL), BF16)],
        compiler_params=_params(("arbitrary", "arbitrary")),
        name="mm_ln",
    )(a, w, x, g3, b3)


def _add_ln_body(x_ref, y0_ref, y1_ref, gate_ref, g_ref, b_ref, of_ref, ob_ref):
    gate = gate_ref[...]
    y = gate[:, 0:1] * y0_ref[...] + gate[:, 1:2] * y1_ref[...]
    out = _layer_norm(ALPHA * x_ref[...] + y, g_ref[...], b_ref[...])
    of_ref[...] = out
    ob_ref[...] = out.astype(BF16)


def add_ln(x, y0, y1, y_row0, gate, g, b, ln_layer, tm):
    M = x.shape[0]
    g3 = g.reshape(DEPTH, 1, D_MODEL)
    b3 = b.reshape(DEPTH, 1, D_MODEL)
    row = pl.BlockSpec((tm, D_MODEL), lambda i: (i, 0))
    yrow = pl.BlockSpec((tm, D_MODEL), lambda i: (i + y_row0 // tm, 0))
    vec = pl.BlockSpec((None, 1, D_MODEL), lambda i: (ln_layer, 0, 0))
    return pl.pallas_call(
        _add_ln_body,
        grid=(M // tm,),
        in_specs=[row, yrow, yrow, pl.BlockSpec((tm, TOP_K), lambda i: (i, 0)), vec, vec],
        out_specs=[row, row],
        out_shape=[jax.ShapeDtypeStruct((M, D_MODEL), F32),
                   jax.ShapeDtypeStruct((M, D_MODEL), BF16)],
        compiler_params=_params(("arbitrary",)),
        name="add_ln",
    )(x, y0, y1, gate, g3, b3)


def _swiglu_up_body(x_ref, xs_ref, wg_ref, wu_ref, o_ref, os_ref, wgb_ref, wub_ref, *, nm):
    i = pl.program_id(1)

    @pl.when(i == 0)
    def _():
        wgb_ref[...] = wg_ref[...].astype(BF16)
        wub_ref[...] = wu_ref[...].astype(BF16)

    def up(rows, out):
        x = rows[...]
        a = _dot(x, wgb_ref[...])
        u = _dot(x, wub_ref[...])
        out[...] = (a * _sigmoid(a) * u).astype(out.dtype)

    @pl.when(i < nm)
    def _():
        up(x_ref, o_ref)

    @pl.when(i == nm)
    def _():
        up(xs_ref, os_ref)


def swiglu_up(x, xs, wg, wu, layer, tm, tn):
    M, K = x.shape
    Ms = xs.shape[0]
    N = wg.shape[-1]
    nm = M // tm
    wspec = pl.BlockSpec((None, K, tn), lambda j, i: (layer, 0, j))
    return pl.pallas_call(
        functools.partial(_swiglu_up_body, nm=nm),
        grid=(N // tn, nm + 1),
        in_specs=[pl.BlockSpec((tm, K), _tail_row(nm)), pl.BlockSpec((Ms, K), lambda j, i: (0, 0)), wspec, wspec],
        out_specs=[pl.BlockSpec((tm, tn), lambda j, i: (jnp.minimum(i, nm - 1), j)),
                   pl.BlockSpec((Ms, tn), lambda j, i: (0, j))],
        out_shape=[jax.ShapeDtypeStruct((M, N), BF16), jax.ShapeDtypeStruct((Ms, N), BF16)],
        scratch_shapes=[pltpu.VMEM((K, tn), BF16), pltpu.VMEM((K, tn), BF16)],
        compiler_params=_params(("arbitrary", "arbitrary")),
        name="swiglu_up",
    )(x, xs, wg, wu)


MOE_TM = 128
MOE_GROUP = 20
MOE_CHUNKS = (16, 8, 4, 2, 1)
MOE_TF = 256


def _moe_body(se_ref, sb_ref, sn_ref, sf_ref, x_hbm, wg_ref, wu_ref, wd_ref, out_hbm,
              xbuf, acc, wgb, wub, wdb, sem_in, sem_out, *, nj):
    s = pl.program_id(0)
    j = pl.program_id(1)
    nb = sn_ref[s]
    fill = sf_ref[s] == 1
    active = (nb > 0) & (sf_ref[s] == 0)
    row0 = sb_ref[s] * MOE_TM

    def rows(c, n=1):
        return pl.ds(pl.multiple_of(c * MOE_TM, MOE_TM), n * MOE_TM)

    def x_copy(c):
        return pltpu.make_async_copy(x_hbm.at[pl.ds(row0 + c * MOE_TM, MOE_TM)], xbuf.at[rows(c)], sem_in)

    def out_copy(c):
        return pltpu.make_async_copy(acc.at[rows(c)], out_hbm.at[pl.ds(row0 + c * MOE_TM, MOE_TM)], sem_out)

    def each_block(fn):
        def body(c, carry):
            fn(c)
            return carry
        lax.fori_loop(0, nb, body, 0)

    def zero(c):
        acc[rows(c), :] = jnp.zeros((MOE_TM, acc.shape[-1]), F32)

    @pl.when((j == 0) & active)
    def _():
        each_block(lambda c: x_copy(c).start())
        each_block(zero)
        each_block(lambda c: x_copy(c).wait())

    def compute(c, n):
        x = xbuf[rows(c, n), :]
        a = _dot(x, wgb[...])
        u = _dot(x, wub[...])
        acc[rows(c, n), :] += _dot((a * _sigmoid(a) * u).astype(BF16), wdb[...])

    @pl.when(active)
    def _():
        wgb[...] = wg_ref[...].astype(BF16)
        wub[...] = wu_ref[...].astype(BF16)
        wdb[...] = wd_ref[...].astype(BF16)
        big = MOE_CHUNKS[0]

        def big_chunk(i, carry):
            compute(i * big, big)
            return carry

        n_big = nb // big
        lax.fori_loop(0, n_big, big_chunk, 0)
        rem = nb - n_big * big
        for n in MOE_CHUNKS[1:]:
            @pl.when((rem & n) != 0)
            def _(n=n):
                compute(n_big * big + (rem & ~(2 * n - 1)), n)

    @pl.when((j == nj - 1) & fill)
    def _():
        each_block(zero)

    @pl.when((j == nj - 1) & (nb > 0))
    def _():
        each_block(lambda c: out_copy(c).start())
        each_block(lambda c: out_copy(c).wait())


def moe_experts(tile_e, tile_b, tile_n, tile_f, xs, wg, wu, wd, layer):
    P, D = xs.shape
    F = wg.shape[-1]
    nj = F // MOE_TF
    S = tile_e.shape[0]

    def jj(s, j, sn, sf):
        return jnp.where((sn[s] > 0) & (sf[s] == 0), j, nj - 1)

    up = pl.BlockSpec((None, None, D, MOE_TF),
                      lambda s, j, se, sb, sn, sf: (layer, se[s], 0, jj(s, j, sn, sf)))
    down = pl.BlockSpec((None, None, MOE_TF, D),
                        lambda s, j, se, sb, sn, sf: (layer, se[s], jj(s, j, sn, sf), 0))
    return pl.pallas_call(
        functools.partial(_moe_body, nj=nj),
        grid_spec=pltpu.PrefetchScalarGridSpec(
            num_scalar_prefetch=4,
            grid=(S, nj),
            in_specs=[pl.BlockSpec(memory_space=pl.ANY), up, up, down],
            out_specs=pl.BlockSpec(memory_space=pl.ANY),
            scratch_shapes=[pltpu.VMEM((MOE_GROUP * MOE_TM, D), BF16),
                            pltpu.VMEM((MOE_GROUP * MOE_TM, D), F32),
                            pltpu.VMEM((D, MOE_TF), BF16), pltpu.VMEM((D, MOE_TF), BF16),
                            pltpu.VMEM((MOE_TF, D), BF16),
                            pltpu.SemaphoreType.DMA(()), pltpu.SemaphoreType.DMA(())]),
        out_shape=jax.ShapeDtypeStruct((P, D), F32),
        compiler_params=_params(("arbitrary", "arbitrary")),
        name="moe_experts",
    )(tile_e, tile_b, tile_n, tile_f, xs, wg, wu, wd)


def _router_body(x_ref, w_ref, b_ref, e_ref, g_ref):
    x = x_ref[...]
    w = w_ref[...]
    xh = x.astype(BF16)
    xl = (x - xh.astype(F32)).astype(BF16)
    wh = w.astype(BF16)
    wl = (w - wh.astype(F32)).astype(BF16)
    logits = _dot(xh, wh) + (_dot(xh, wl) + _dot(xl, wh)) + b_ref[...]
    lane = lax.broadcasted_iota(jnp.int32, logits.shape, 1).astype(F32)
    m1 = jnp.max(logits, axis=-1, keepdims=True)
    i1 = jnp.min(jnp.where(logits == m1, lane, float(LANES)), axis=-1, keepdims=True)
    rest = jnp.where(lane == i1, 2.0 * NEG, logits)
    m2 = jnp.max(rest, axis=-1, keepdims=True)
    i2 = jnp.min(jnp.where(rest == m2, lane, float(LANES)), axis=-1, keepdims=True)
    g1 = 1.0 / (1.0 + jnp.exp(m2 - m1))
    two = lax.broadcasted_iota(jnp.int32, (logits.shape[0], TOP_K), 1)
    e_ref[...] = jnp.where(two == 0, i1, i2).astype(jnp.int32)
    g_ref[...] = jnp.where(two == 0, g1, 1.0 - g1)


def router(x, w_router, b_router, j, tm):
    M = x.shape[0]
    n_moe = w_router.shape[0]
    w_pad = jnp.pad(w_router.astype(F32), ((0, 0), (0, 0), (0, LANES - N_EXPERTS)))
    b3 = jnp.pad(b_router.astype(F32).reshape(n_moe, 1, N_EXPERTS), ((0, 0), (0, 0), (0, LANES - N_EXPERTS)),
                 constant_values=NEG)
    return pl.pallas_call(
        _router_body,
        grid=(M // tm,),
        in_specs=[pl.BlockSpec((tm, D_MODEL), lambda i: (i, 0)),
                  pl.BlockSpec((None, D_MODEL, LANES), lambda i: (j, 0, 0)),
                  pl.BlockSpec((None, 1, LANES), lambda i: (j, 0, 0))],
        out_specs=[pl.BlockSpec((tm, TOP_K), lambda i: (i, 0)),
                   pl.BlockSpec((tm, TOP_K), lambda i: (i, 0))],
        out_shape=[jax.ShapeDtypeStruct((M, TOP_K), jnp.int32),
                   jax.ShapeDtypeStruct((M, TOP_K), F32)],
        compiler_params=_params(("arbitrary",)),
        name="router",
    )(x, w_pad, b3)


HG_CHUNK = 128
HG_HPS = 8
HG_LEVELS = 7
_ROW_B = HG_LEVELS * HG_CHUNK
_ROW_U = _ROW_B + HG_CHUNK


def _hgrn_tables():
    C = HG_CHUNK
    t = np.arange(C)[:, None]
    u = np.arange(C)[None, :]
    sel, low, seg = [], [], []
    for lvl in range(HG_LEVELS):
        w = C >> (lvl + 1)
        ref = (t // (2 * w)) * (2 * w) + w - 1
        lower = (t % (2 * w)) >= w
        sel.append(((u > np.minimum(t, ref)) & (u <= np.maximum(t, ref))).astype(np.float32))
        low.append(np.broadcast_to(lower, (C, HG_HPS * HG_D)).astype(np.float32))
        upper_s = (u % (2 * w)) < w
        seg.append((((t // (2 * w)) == (u // (2 * w))) & lower & upper_s).astype(np.float32))
    sel.append((u <= t).astype(np.float32))
    sel.append((u > t).astype(np.float32))
    sel = np.concatenate(sel, axis=0)
    return (jnp.asarray(np.concatenate([sel, sel], axis=1), BF16),
            jnp.asarray(np.stack(low)), jnp.asarray(np.stack(seg)))


def _hgrn_body(sel_ref, low_ref, seg_ref, hq_ref, hf_ref, hi_ref, hg_ref, lb_ref, gn_ref,
               o_ref, s_ref, st_ref, *, nc):
    c = pl.program_id(2)

    @pl.when(c == 0)
    def _():
        st_ref[...] = jnp.zeros_like(st_ref)

    C = HG_CHUNK
    heads = [slice(hh * HG_D, (hh + 1) * HG_D) for hh in range(HG_HPS)]
    q = hq_ref[...]
    hf = hf_ref[...]
    v = hi_ref[...]
    lb = lb_ref[...]
    g = jnp.log(lb + (1.0 - lb) * _sigmoid(hf)) * LOG2E
    kk = (1.0 - lb) * _sigmoid(-hf)
    g_hi = g.astype(BF16)
    g_lo = (g - g_hi.astype(F32)).astype(BF16)
    ex = jnp.exp2(_dot(sel_ref[...], jnp.concatenate([g_hi, g_lo], axis=0)))
    eb = ex[_ROW_B:_ROW_B + C]
    eu = ex[_ROW_U:_ROW_U + C]
    eb_end = eb[C - 1:C, :]
    v_b = v.astype(BF16)

    qe = (q * eb).astype(BF16)
    sts = [st_ref[hh] for hh in range(HG_HPS)]
    o = [_dot_nt(qe[:, hs], sts[hh].astype(BF16)) for hh, hs in enumerate(heads)]
    a = [jnp.zeros((C, C), F32) for _ in heads]
    for lvl in range(HG_LEVELS):
        e = ex[lvl * C:(lvl + 1) * C]
        xt = (jnp.where(low_ref[lvl] > 0.5, q, kk) * e).astype(BF16)
        seg = seg_ref[lvl]
        for hh, hs in enumerate(heads):
            a[hh] = a[hh] + _dot_nt(xt[:, hs], xt[:, hs]) * seg
    kd = (kk * eu).astype(BF16)
    qk = q * kk
    for hh, hs in enumerate(heads):
        o[hh] = o[hh] + _dot(a[hh].astype(BF16), v_b[:, hs])
        o[hh] = o[hh] + jnp.sum(qk[:, hs], axis=-1, keepdims=True) * v[:, hs]
        st_ref[hh] = sts[hh] * eb_end[:, hs] + _dot_tn(v_b[:, hs], kd[:, hs])

    hg = hg_ref[...]
    gate = hg * _sigmoid(hg)
    for hh, hs in enumerate(heads):
        on = o[hh] * lax.rsqrt(jnp.mean(o[hh] * o[hh], axis=-1, keepdims=True) + RMS_EPS) * gn_ref[...]
        o_ref[:, hs] = (on * gate[:, hs]).astype(o_ref.dtype)

    @pl.when(c == nc - 1)
    def _():
        for hh in range(HG_HPS):
            s_ref[hh] = st_ref[hh].T


def hgrn_prompt(z, lb, gnorm, layer, B, T):
    C = HG_CHUNK
    nc = T // C
    W = HG_HPS * HG_D
    sel, low, seg = _hgrn_tables()
    lb3 = lb.reshape(DEPTH, 1, HG_WIDTH)
    gn3 = gnorm.reshape(DEPTH, 1, HG_D)

    def col(off):
        return pl.BlockSpec((C, W), lambda b, h, c: (b * nc + c, off // W + h))

    return pl.pallas_call(
        functools.partial(_hgrn_body, nc=nc),
        grid=(B, HG_HEADS // HG_HPS, nc),
        in_specs=[pl.BlockSpec(sel.shape, lambda b, h, c: (0, 0)),
                  pl.BlockSpec(low.shape, lambda b, h, c: (0, 0, 0)),
                  pl.BlockSpec(seg.shape, lambda b, h, c: (0, 0, 0)),
                  col(HQ_OFF), col(HF_OFF), col(HI_OFF), col(HGATE_OFF),
                  pl.BlockSpec((None, 1, W), lambda b, h, c: (layer, 0, h)),
                  pl.BlockSpec((None, 1, HG_D), lambda b, h, c: (layer, 0, 0))],
        out_specs=[pl.BlockSpec((C, W), lambda b, h, c: (b * nc + c, h)),
                   pl.BlockSpec((None, HG_HPS, HG_D, HG_D), lambda b, h, c: (b, h, 0, 0))],
        out_shape=[jax.ShapeDtypeStruct((B * T, HG_WIDTH), BF16),
                   jax.ShapeDtypeStruct((B, HG_HEADS, HG_D, HG_D), F32)],
        scratch_shapes=[pltpu.VMEM((HG_HPS, HG_D, HG_D), F32)],
        compiler_params=_params(("parallel", "parallel", "arbitrary")),
        name="hgrn_prompt",
    )(sel, low, seg, z, z, z, z, lb3, gn3)


def _hgrn_step_body(s0_ref, qc_ref, fc_ref, vr_ref, gr_ref, lbc_ref, gn_ref, o_ref, s_ref):
    for h in range(HG_HEADS):
        s0 = s0_ref[h]
        hf = fc_ref[h]
        lb = lbc_ref[h]
        f = lb + (1.0 - lb) * _sigmoid(hf)
        kk = (1.0 - lb) * _sigmoid(-hf)
        v = vr_ref[h]
        s_new = f * s0 + kk * v
        s_ref[h] = s_new
        o = jnp.sum(qc_ref[h] * s_new, axis=0, keepdims=True)
        hg = gr_ref[h]
        on = o * lax.rsqrt(jnp.mean(o * o, axis=-1, keepdims=True) + RMS_EPS) * gn_ref[...]
        o_ref[h] = (on * (hg * _sigmoid(hg))).astype(o_ref.dtype)


def hgrn_step(zs, state, lb, gnorm, layer):
    Bs = zs.shape[0]
    heads = lambda off: zs[:, off:off + HG_WIDTH].reshape(Bs, HG_HEADS, HG_D)
    qc = heads(HQ_OFF)[..., None]
    fc = heads(HF_OFF)[..., None]
    vr = heads(HI_OFF)[:, :, None, :]
    gr = heads(HGATE_OFF)[:, :, None, :]
    lbc = lb.reshape(DEPTH, HG_HEADS, HG_D, 1)
    gn3 = gnorm.reshape(DEPTH, 1, HG_D)
    colspec = pl.BlockSpec((None, HG_HEADS, HG_D, 1), lambda b: (b, 0, 0, 0))
    rowspec = pl.BlockSpec((None, HG_HEADS, 1, HG_D), lambda b: (b, 0, 0, 0))
    o, s = pl.pallas_call(
        _hgrn_step_body,
        grid=(Bs,),
        in_specs=[pl.BlockSpec((None, None, HG_HEADS, HG_D, HG_D), lambda b: (layer, b, 0, 0, 0)),
                  colspec, colspec, rowspec, rowspec,
                  pl.BlockSpec((None, HG_HEADS, HG_D, 1), lambda b: (layer, 0, 0, 0)),
                  pl.BlockSpec((None, 1, HG_D), lambda b: (layer, 0, 0))],
        out_specs=[rowspec,
                   pl.BlockSpec((None, HG_HEADS, HG_D, HG_D), lambda b: (b, 0, 0, 0))],
        out_shape=[jax.ShapeDtypeStruct((Bs, HG_HEADS, 1, HG_D), BF16),
                   jax.ShapeDtypeStruct((Bs, HG_HEADS, HG_D, HG_D), F32)],
        compiler_params=_params(("arbitrary",)),
        name="hgrn_step",
    )(state, qc, fc, vr, gr, lbc, gn3)
    return o.reshape(Bs, HG_WIDTH), s


def _slope(g, h):
    return float(2.0 ** (-8.0 * (g * ATT_HPG + h + 1) / ATT_HEADS))


def _band_rows(start, dil):
    return pl.ds(start, BAND, stride=dil) if dil > 1 else pl.ds(start, BAND)


ATT_UNITS = 4


def _attn_units(q_ref, k_ref, v_ref, o_scr, l_scr, starts, prev_starts, prev_offs, dil, pen, has_prev):
    n = range(len(starts))
    row = lax.broadcasted_iota(jnp.int32, (BAND, BAND), 0)
    colj = lax.broadcasted_iota(jnp.int32, (BAND, BAND), 1)
    pen_c = pen * (row - colj).astype(F32)
    rows_c = [_band_rows(s, dil) for s in starts]
    q = [q_ref[r, :].astype(BF16) for r in rows_c]
    kc = [k_ref[r, :].astype(BF16) for r in rows_c]
    vc = [v_ref[r, :].astype(BF16) for r in rows_c]
    s_c = [jnp.where(colj <= row, _dot_nt(q[i], kc[i]) * ATT_SCALE - pen_c, NEG) for i in n]
    m = [jnp.max(s, axis=-1, keepdims=True) for s in s_c]
    if has_prev:
        pen_p = pen_c + pen * float(BAND)
        rows_p = [_band_rows(s, dil) for s in prev_starts]
        kp = [k_ref[r, :].astype(BF16) for r in rows_p]
        vp = [v_ref[r, :].astype(BF16) for r in rows_p]
        s_p = [jnp.where(colj >= row, _dot_nt(q[i], kp[i]) * ATT_SCALE - pen_p, NEG) + prev_offs[i] for i in n]
        m = [jnp.maximum(m[i], jnp.max(s_p[i], axis=-1, keepdims=True)) for i in n]
    p_c = [jnp.exp(s_c[i] - m[i]) for i in n]
    l = [jnp.sum(p, axis=-1, keepdims=True) for p in p_c]
    acc = [_dot(p_c[i].astype(BF16), vc[i]) for i in n]
    if has_prev:
        p_p = [jnp.exp(s_p[i] - m[i]) for i in n]
        l = [l[i] + jnp.sum(p_p[i], axis=-1, keepdims=True) for i in n]
        acc = [acc[i] + _dot(p_p[i].astype(BF16), vp[i]) for i in n]
    for i in n:
        o_scr[rows_c[i], :] = acc[i] / l[i]
        l_scr[rows_c[i], :] = jnp.broadcast_to(m[i] + jnp.log(l[i]), (BAND, ATT_DH))


def _attn_body(slope_ref, *refs, T):
    qkv = refs[:3 * N_GROUPS]
    out_ref, o_scr, l_scr = refs[3 * N_GROUPS:]
    h = pl.program_id(1)
    for g, (_, dil) in enumerate(ATT_GROUPS):
        q_ref, k_ref, v_ref = qkv[3 * g:3 * g + 3]
        pen = slope_ref[g * ATT_HPG + h] * float(dil)
        nqb = T // dil // BAND

        def units(it, carry, q_ref=q_ref, k_ref=k_ref, v_ref=v_ref, g=g, dil=dil, pen=pen, nqb=nqb):
            starts, prev_starts, prev_offs = [], [], []
            for u in range(ATT_UNITS):
                idx = it * ATT_UNITS + u
                r = idx // nqb
                qb = idx - r * nqb
                start = r + qb * (BAND * dil)
                prev_start = jnp.maximum(start - BAND * dil, 0)
                if dil == 1:
                    start = pl.multiple_of(start, BAND)
                    prev_start = pl.multiple_of(prev_start, BAND)
                starts.append(start)
                prev_starts.append(prev_start)
                prev_offs.append(jnp.where(qb > 0, 0.0, NEG))
            _attn_units(q_ref, k_ref, v_ref, o_scr.at[g], l_scr.at[g], starts, prev_starts, prev_offs,
                        dil, pen, has_prev=nqb > 1)
            return carry

        lax.fori_loop(0, dil * nqb // ATT_UNITS, units, 0)

    def merge(i, carry):
        rs = pl.ds(pl.multiple_of(i * BAND, BAND), BAND)
        a0, a1, a2 = l_scr[0, rs, :], l_scr[1, rs, :], l_scr[2, rs, :]
        m = jnp.maximum(jnp.maximum(a0, a1), a2)
        w0, w1, w2 = jnp.exp(a0 - m), jnp.exp(a1 - m), jnp.exp(a2 - m)
        out = (w0 * o_scr[0, rs, :] + w1 * o_scr[1, rs, :] + w2 * o_scr[2, rs, :]) / (w0 + w1 + w2)
        out_ref[rs, :] = out.astype(out_ref.dtype)
        return carry

    lax.fori_loop(0, T // BAND, merge, 0)


def attn_prompt(z, B, T):
    slopes = jnp.asarray([_slope(g, h) for g in range(N_GROUPS) for h in range(ATT_HPG)], F32)

    def col(off, g):
        c0 = off // ATT_DH + g * ATT_HPG
        return pl.BlockSpec((T, ATT_DH), lambda b, h: (b, c0 + h))

    in_specs = [pl.BlockSpec(memory_space=pltpu.SMEM)]
    for g in range(N_GROUPS):
        in_specs += [col(AQ_OFF, g), col(AK_OFF, g), col(AV_OFF, g)]
    return pl.pallas_call(
        functools.partial(_attn_body, T=T),
        grid=(B, ATT_HPG),
        in_specs=in_specs,
        out_specs=pl.BlockSpec((T, ATT_DH), lambda b, h: (b, h)),
        out_shape=jax.ShapeDtypeStruct((B * T, ATT_GW), BF16),
        scratch_shapes=[pltpu.VMEM((N_GROUPS, T, ATT_DH), F32), pltpu.VMEM((N_GROUPS, T, ATT_DH), F32)],
        compiler_params=_params(("parallel", "arbitrary")),
        name="attn_prompt",
    )(slopes, *([z] * (3 * N_GROUPS)))


def _attn_step_body(z_ref, c0_ref, c1_ref, c2_ref, o_ref):
    caches = (c0_ref, c1_ref, c2_ref)
    steps = float(BAND) - lax.broadcasted_iota(jnp.int32, (BAND, 1, 1), 0).astype(F32)
    head = lax.broadcasted_iota(jnp.int32, (ATT_HPG, 1), 0).astype(F32)
    outs, lses = [], []
    for g, (_, dil) in enumerate(ATT_GROUPS):
        slope = jnp.exp2((head + float(g * ATT_HPG + 1)) * (-8.0 / ATT_HEADS))
        pen = slope * float(dil)
        r0 = g * ATT_HPG
        q = z_ref[AQ_OFF // ATT_DH + r0:AQ_OFF // ATT_DH + r0 + ATT_HPG, :]
        kn = z_ref[AK_OFF // ATT_DH + r0:AK_OFF // ATT_DH + r0 + ATT_HPG, :]
        vn = z_ref[AV_OFF // ATT_DH + r0:AV_OFF // ATT_DH + r0 + ATT_HPG, :]
        kc = caches[g][:, 0]
        vc = caches[g][:, 1]
        s_c = jnp.sum(kc * q[None], axis=-1, keepdims=True) * ATT_SCALE - pen[None] * steps
        s_n = jnp.sum(kn * q, axis=-1, keepdims=True) * ATT_SCALE
        m = jnp.maximum(jnp.max(s_c, axis=0), s_n)
        p_c = jnp.exp(s_c - m[None])
        p_n = jnp.exp(s_n - m)
        l = jnp.sum(p_c, axis=0) + p_n
        outs.append((jnp.sum(p_c * vc, axis=0) + p_n * vn) / l)
        lses.append(m + jnp.log(l))
    mm = jnp.maximum(jnp.maximum(lses[0], lses[1]), lses[2])
    ws = [jnp.exp(x - mm) for x in lses]
    out = (ws[0] * outs[0] + ws[1] * outs[1] + ws[2] * outs[2]) / (ws[0] + ws[1] + ws[2])
    o_ref[...] = out.astype(o_ref.dtype)


def attn_step(zs, caches, layer):
    Bs = zs.shape[0]
    zr = zs.reshape(Bs, N_IN // ATT_DH, ATT_DH)
    cache_specs, cache_args = [], []
    for g, (win, dil) in enumerate(ATT_GROUPS):
        c = caches[g]
        assert c.shape[2] == win and win // dil == BAND
        cache_args.append(c.reshape(DEPTH, Bs, BAND, dil, 2, ATT_HPG, ATT_DH))
        cache_specs.append(pl.BlockSpec((None, None, BAND, None, 2, ATT_HPG, ATT_DH),
                                        lambda b: (layer, b, 0, 0, 0, 0, 0)))
    o = pl.pallas_call(
        _attn_step_body,
        grid=(Bs,),
        in_specs=[pl.BlockSpec((None, N_IN // ATT_DH, ATT_DH), lambda b: (b, 0, 0))] + cache_specs,
        out_specs=pl.BlockSpec((None, ATT_HPG, ATT_DH), lambda b: (b, 0, 0)),
        out_shape=jax.ShapeDtypeStruct((Bs, ATT_HPG, ATT_DH), BF16),
        compiler_params=_params(("arbitrary",)),
        name="attn_step",
    )(zr, *cache_args)
    return o.reshape(Bs, ATT_GW)


KV_SLOTS = 2 * ATT_HPG


def _kv_rows_body(k_ref, v_ref, o_ref):
    tk = k_ref.shape[0]
    for c, ref in enumerate((k_ref, v_ref)):
        for h in range(ATT_HPG):
            o_ref[pl.ds(c * ATT_HPG + h, tk, stride=KV_SLOTS), :] = ref[:, h * ATT_DH:(h + 1) * ATT_DH]


def kv_rows(z, group, B, T):
    keep = min(ATT_GROUPS[group][0], T)
    tk = min(keep, 512)
    nt = keep // tk
    first = (T - keep) // tk

    def col(off):
        c0 = off // ATT_GW + group
        return pl.BlockSpec((tk, ATT_GW), lambda b, t: (b * (T // tk) + first + t, c0))

    return pl.pallas_call(
        _kv_rows_body,
        grid=(B, nt),
        in_specs=[col(AK_OFF), col(AV_OFF)],
        out_specs=pl.BlockSpec((tk * KV_SLOTS, ATT_DH), lambda b, t: (b * nt + t, 0)),
        out_shape=jax.ShapeDtypeStruct((B * keep * KV_SLOTS, ATT_DH), F32),
        compiler_params=_params(("arbitrary", "arbitrary")),
        name="kv_rows",
    )(z, z)


def _moe_tables(top_e):
    N = top_e.shape[0]
    A = N * TOP_K
    tm, G = MOE_TM, MOE_GROUP
    n_blocks = -(-A // tm) + N_EXPERTS
    P = n_blocks * tm
    n_tiles = -(-n_blocks // G) + N_EXPERTS + 1
    i32 = jnp.int32
    flat_e = top_e.reshape(A)
    onehot = (flat_e[:, None] == jnp.arange(N_EXPERTS)[None, :]).astype(i32)
    rank = jnp.take_along_axis(jnp.cumsum(onehot, axis=0), flat_e[:, None], axis=1)[:, 0] - 1
    counts = jnp.sum(onehot, axis=0)
    blocks = (counts + tm - 1) // tm
    bend = jnp.cumsum(blocks)
    bstart = bend - blocks
    dest = (bstart[flat_e] * tm + rank).astype(i32)
    rows = jnp.zeros((P,), i32).at[dest].set(jnp.arange(A, dtype=i32) // TOP_K)
    tiles = (blocks + G - 1) // G
    tend = jnp.cumsum(tiles)
    tstart = tend - tiles
    n_used = tend[-1]
    t = jnp.arange(n_tiles, dtype=i32)
    last_e = jnp.max(jnp.where(blocks > 0, jnp.arange(N_EXPERTS), 0))
    e_of = jnp.minimum(jnp.searchsorted(tend, t, side='right'), N_EXPERTS - 1)
    k = t - tstart[e_of]
    used = t < n_used
    tile_e = jnp.where(used, e_of, last_e)
    tile_b = jnp.where(used, bstart[e_of] + k * G, 0)
    tile_n = jnp.where(used, jnp.minimum(G, blocks[e_of] - k * G), 0)
    is_fill = t == n_used
    tile_b = jnp.where(is_fill, bend[-1], tile_b)
    tile_n = jnp.where(is_fill, n_blocks - bend[-1], tile_n)
    return (rows, dest.reshape(N, TOP_K), tile_e.astype(i32), tile_b.astype(i32), tile_n.astype(i32),
            is_fill.astype(i32))


def _mixer(xp_b, xs_b, xp_f, xs_f, i, B, T, caches, state_hgrn, w_in, lb_all, hgrn_norm,
           w_branch_a, w_branch_b, w_out, ln1_g, ln1_b):
    Np = B * T
    Bs = xs_b.shape[0]
    zp, zs = matmul(xp_b, xs_b, w_in, i, 1024, 1280, F32)
    oh_p, S_p = hgrn_prompt(zp, lb_all, hgrn_norm, i, B, T)
    oa_p = attn_prompt(zp, B, T)
    oh_s, S_s = hgrn_step(zs, state_hgrn, lb_all, hgrn_norm, i)
    oa_s = attn_step(zs, caches, i)
    mp, ms = branch_merge(oh_p, oa_p, zp, oh_s, oa_s, zs, w_branch_a, w_branch_b, i, 1024)
    hp_f, hp_b = matmul_ln(mp, w_out, i, xp_f, ln1_g, ln1_b, i, 512, 2048)
    hs_f, hs_b = matmul_ln(ms, w_out, i, xs_f, ln1_g, ln1_b, i, Bs, 1024)

    kv_p, kv_s = [], []
    for g in range(N_GROUPS):
        ko, vo = AK_OFF + g * ATT_GW, AV_OFF + g * ATT_GW
        kv_p.append(kv_rows(zp, g, B, T))
        ks = zs[:, ko:ko + ATT_GW].reshape(Bs, 1, ATT_HPG, ATT_DH)
        vs = zs[:, vo:vo + ATT_GW].reshape(Bs, 1, ATT_HPG, ATT_DH)
        kv_s.append(jnp.stack([ks, vs], axis=2))
    return (hp_f, hp_b, hs_f, hs_b, S_p, S_s, kv_p, kv_s)


def _dense_ffn(hp_f, hp_b, hs_f, hs_b, j, ffn_w_gate, ffn_w_up, ffn_w_down, ln2_g, ln2_b, i):
    up, us = swiglu_up(hp_b, hs_b, ffn_w_gate, ffn_w_up, j, 2048, 512)
    xp_f, xp_b = matmul_ln(up, ffn_w_down, j, hp_f, ln2_g, ln2_b, i, 512, 512)
    xs_f, xs_b = matmul_ln(us, ffn_w_down, j, hs_f, ln2_g, ln2_b, i, hs_f.shape[0], 512)
    return xp_f, xp_b, xs_f, xs_b


def _moe_ffn(hp_f, hp_b, hs_f, hs_b, j, moe_w_router, moe_b_router, moe_w_gate, moe_w_up, moe_w_down,
             ln2_g, ln2_b, i):
    Np, Bs = hp_f.shape[0], hs_f.shape[0]
    ep, gp = router(hp_f, moe_w_router, moe_b_router, j, 1024)
    es, gs = router(hs_f, moe_w_router, moe_b_router, j, Bs)
    top_e = jnp.concatenate([ep, es], axis=0)
    rows, dest, tile_e, tile_b, tile_n, tile_f = _moe_tables(top_e)
    h_b = jnp.concatenate([hp_b, hs_b], axis=0)
    xs = h_b[rows]
    yb = moe_experts(tile_e, tile_b, tile_n, tile_f, xs, moe_w_gate, moe_w_up, moe_w_down, j)
    y0 = yb[dest[:, 0]]
    y1 = yb[dest[:, 1]]
    xp_f, xp_b = add_ln(hp_f, y0, y1, 0, gp, ln2_g, ln2_b, i, 256)
    xs_f, xs_b = add_ln(hs_f, y0, y1, Np, gs, ln2_g, ln2_b, i, Bs)
    return xp_f, xp_b, xs_f, xs_b


def kernel(x_prompt, x_sample, cache_kv_w128, cache_kv_w512, cache_kv_w2048, state_hgrn,
           w_in, lb_param, hgrn_norm, w_branch_a, w_branch_b, w_out,
           ln1_g, ln1_b, ln2_g, ln2_b, ffn_w_gate, ffn_w_up, ffn_w_down,
           moe_w_router, moe_b_router, moe_w_gate, moe_w_up, moe_w_down):
    B, T, _ = x_prompt.shape
    Bs = x_sample.shape[0]
    lb_cum = jnp.cumsum(jax.nn.softmax(lb_param.astype(F32), axis=0), axis=0)
    lb_all = lb_cum - lb_cum[0]
    caches = (cache_kv_w128, cache_kv_w512, cache_kv_w2048)
    xp_f = x_prompt.reshape(B * T, D_MODEL)
    xs_f = x_sample.reshape(Bs, D_MODEL)
    xp_b = xp_f.astype(BF16)
    xs_b = xs_f.astype(BF16)
    kv_p = [[] for _ in ATT_GROUPS]
    kv_s = [[] for _ in ATT_GROUPS]
    hg_p, hg_s = [], []
    for i in range(DEPTH):
        hp_f, hp_b, hs_f, hs_b, S_p, S_s, kvp, kvs = _mixer(
            xp_b, xs_b, xp_f, xs_f, i, B, T, caches, state_hgrn, w_in, lb_all, hgrn_norm,
            w_branch_a, w_branch_b, w_out, ln1_g, ln1_b)
        for g in range(N_GROUPS):
            kv_p[g].append(kvp[g])
            kv_s[g].append(kvs[g])
        hg_p.append(S_p)
        hg_s.append(S_s)
        j = i // 2
        if i % 2 == 0:
            xp_f, xp_b, xs_f, xs_b = _dense_ffn(hp_f, hp_b, hs_f, hs_b, j, ffn_w_gate, ffn_w_up, ffn_w_down,
                                                ln2_g, ln2_b, i)
        else:
            xp_f, xp_b, xs_f, xs_b = _moe_ffn(hp_f, hp_b, hs_f, hs_b, j, moe_w_router, moe_b_router,
                                              moe_w_gate, moe_w_up, moe_w_down, ln2_g, ln2_b, i)
    def prompt_kv(g):
        keep = min(ATT_GROUPS[g][0], T)
        return jnp.stack(kv_p[g], axis=0).reshape(DEPTH, B, keep, 2, ATT_HPG, ATT_DH)

    return (xp_f.reshape(B, T, D_MODEL), xs_f.reshape(Bs, 1, D_MODEL),
            prompt_kv(0), jnp.stack(kv_s[0], axis=0),
            prompt_kv(1), jnp.stack(kv_s[1], axis=0),
            prompt_kv(2), jnp.stack(kv_s[2], axis=0),
            jnp.stack(hg_p, axis=0), jnp.stack(hg_s, axis=0))
```

```python
import functools

import numpy as np
import jax
import jax.numpy as jnp
from jax import lax
from jax.experimental import pallas as pl
from jax.experimental.pallas import tpu as pltpu

F32 = jnp.float32
BF16 = jnp.bfloat16

D_MODEL = 2048
DEPTH = 4
HG_HEADS = 8
HG_D = 128
HG_WIDTH = HG_HEADS * HG_D
ATT_GROUPS = ((128, 1), (512, 4), (2048, 16))
N_GROUPS = 3
ATT_HPG = 4
ATT_DH = 128
ATT_HEADS = N_GROUPS * ATT_HPG
ATT_GW = ATT_HPG * ATT_DH
ATT_SCALE = ATT_DH ** -0.5
BAND = 128
HQ_OFF, HF_OFF, HI_OFF, HGATE_OFF = 0, 1024, 2048, 3072
AQ_OFF, AK_OFF, AV_OFF = 4096, 5632, 7168
GA_OFF, GB_OFF = 8704, 10752
N_IN = 12800
N_EXPERTS = 8
TOP_K = 2
ALPHA = (2 * DEPTH) ** 0.25
LN_EPS = 1e-5
RMS_EPS = 1e-6
NEG = -1e30
LANES = 128
LOG2E = 1.4426950408889634

VMEM_LIMIT = 56 * 1024 * 1024


def _params(sem):
    return pltpu.CompilerParams(dimension_semantics=sem, vmem_limit_bytes=VMEM_LIMIT)


def _dot(a, b):
    return jnp.dot(a, b, preferred_element_type=F32)


def _dot_nt(a, b):
    return lax.dot_general(a, b, (((1,), (1,)), ((), ())), preferred_element_type=F32)


def _dot_tn(a, b):
    return lax.dot_general(a, b, (((0,), (0,)), ((), ())), preferred_element_type=F32)


def _sigmoid(x):
    return 1.0 / (1.0 + jnp.exp(-x))


def _layer_norm(y, g, b):
    mu = jnp.mean(y, axis=-1, keepdims=True)
    d = y - mu
    var = jnp.mean(d * d, axis=-1, keepdims=True)
    return d * lax.rsqrt(var + LN_EPS) * g + b


def _mm_body(x_ref, w_ref, o_ref, wb_ref):
    @pl.when(pl.program_id(1) == 0)
    def _():
        wb_ref[...] = w_ref[...].astype(BF16)

    o_ref[...] = _dot(x_ref[...], wb_ref[...]).astype(o_ref.dtype)


def matmul(x, w, layer, tm, tn, out_dtype):
    M, K = x.shape
    N = w.shape[-1]
    return pl.pallas_call(
        _mm_body,
        grid=(N // tn, M // tm),
        in_specs=[pl.BlockSpec((tm, K), lambda j, i: (i, 0)),
                  pl.BlockSpec((None, K, tn), lambda j, i: (layer, 0, j))],
        out_specs=pl.BlockSpec((tm, tn), lambda j, i: (i, j)),
        out_shape=jax.ShapeDtypeStruct((M, N), out_dtype),
        scratch_shapes=[pltpu.VMEM((K, tn), BF16)],
        compiler_params=_params(("arbitrary", "arbitrary")),
        name="mm",
    )(x, w)


def _tail_row(nm):
    return lambda j, i: (jnp.minimum(i, nm - 1), 0)


def _merge_body(oh_ref, oa_ref, ga_ref, gb_ref, ohs_ref, oas_ref, gas_ref, gbs_ref, wa_ref, wb_ref,
                o_ref, os_ref, wab_ref, wbb_ref, *, nm):
    i = pl.program_id(1)

    @pl.when(i == 0)
    def _():
        wab_ref[...] = wa_ref[...].astype(BF16)
        wbb_ref[...] = wb_ref[...].astype(BF16)

    def merged(oh, oa, ga, gb, out):
        ya = _dot(oh[...], wab_ref[...])
        yb = _dot(oa[...], wbb_ref[...])
        out[...] = (_sigmoid(ga[...]) * ya + _sigmoid(gb[...]) * yb).astype(out.dtype)

    @pl.when(i < nm)
    def _():
        merged(oh_ref, oa_ref, ga_ref, gb_ref, o_ref)

    @pl.when(i == nm)
    def _():
        merged(ohs_ref, oas_ref, gas_ref, gbs_ref, os_ref)


def branch_merge(oh, oa, z, ohs, oas, zs, w_a, w_b, layer, tm):
    M = oh.shape[0]
    Ms = ohs.shape[0]
    tn = 512
    nm = M // tm
    ga0, gb0 = GA_OFF // tn, GB_OFF // tn
    row = _tail_row(nm)
    return pl.pallas_call(
        functools.partial(_merge_body, nm=nm),
        grid=(D_MODEL // tn, nm + 1),
        in_specs=[pl.BlockSpec((tm, HG_WIDTH), row),
                  pl.BlockSpec((tm, ATT_GW), row),
                  pl.BlockSpec((tm, tn), lambda j, i: (jnp.minimum(i, nm - 1), ga0 + j)),
                  pl.BlockSpec((tm, tn), lambda j, i: (jnp.minimum(i, nm - 1), gb0 + j)),
                  pl.BlockSpec((Ms, HG_WIDTH), lambda j, i: (0, 0)),
                  pl.BlockSpec((Ms, ATT_GW), lambda j, i: (0, 0)),
                  pl.BlockSpec((Ms, tn), lambda j, i: (0, ga0 + j)),
                  pl.BlockSpec((Ms, tn), lambda j, i: (0, gb0 + j)),
                  pl.BlockSpec((None, HG_WIDTH, tn), lambda j, i: (layer, 0, j)),
                  pl.BlockSpec((None, ATT_GW, tn), lambda j, i: (layer, 0, j))],
        out_specs=[pl.BlockSpec((tm, tn), lambda j, i: (jnp.minimum(i, nm - 1), j)),
                   pl.BlockSpec((Ms, tn), lambda j, i: (0, j))],
        out_shape=[jax.ShapeDtypeStruct((M, D_MODEL), BF16), jax.ShapeDtypeStruct((Ms, D_MODEL), BF16)],
        scratch_shapes=[pltpu.VMEM((HG_WIDTH, tn), BF16), pltpu.VMEM((ATT_GW, tn), BF16)],
        compiler_params=_params(("arbitrary", "arbitrary")),
        name="branch_merge",
    )(oh, oa, z, z, ohs, oas, zs, zs, w_a, w_b)


def _mm_ln_body(a_ref, w_ref, x_ref, g_ref, b_ref, of_ref, ob_ref, acc_ref, wb_ref, *, nk):
    i = pl.program_id(0)
    k = pl.program_id(1)

    @pl.when(i == 0)
    def _():
        wb_ref[k] = w_ref[...].astype(BF16)

    @pl.when(k == 0)
    def _():
        acc_ref[...] = jnp.zeros_like(acc_ref)

    acc_ref[...] += _dot(a_ref[...], wb_ref[k])

    @pl.when(k == nk - 1)
    def _():
        out = _layer_norm(ALPHA * x_ref[...] + acc_ref[...], g_ref[...], b_ref[...])
        of_ref[...] = out
        ob_ref[...] = out.astype(BF16)


def matmul_ln(a, w, layer, x, g, b, ln_layer, tm, tk):
    M, K = a.shape
    nk = K // tk
    g3 = g.reshape(DEPTH, 1, D_MODEL)
    b3 = b.reshape(DEPTH, 1, D_MODEL)
    return pl.pallas_call(
        functools.partial(_mm_ln_body, nk=nk),
        grid=(M // tm, nk),
        in_specs=[pl.BlockSpec((tm, tk), lambda i, k: (i, k)),
                  pl.BlockSpec((None, tk, D_MODEL), lambda i, k: (layer, jnp.where(i == 0, k, nk - 1), 0),
                               pipeline_mode=pl.Buffered(1)),
                  pl.BlockSpec((tm, D_MODEL), lambda i, k: (i, 0)),
                  pl.BlockSpec((None, 1, D_MODEL), lambda i, k: (ln_layer, 0, 0)),
                  pl.BlockSpec((None, 1, D_MODEL), lambda i, k: (ln_layer, 0, 0))],
        out_specs=[pl.BlockSpec((tm, D_MODEL), lambda i, k: (i, 0)),
                   pl.BlockSpec((tm, D_MODEL), lambda i, k: (i, 0))],
        out_shape=[jax.ShapeDtypeStruct((M, D_MODEL), F32),
                   jax.ShapeDtypeStruct((M, D_MODEL), BF16)],
        scratch_shapes=[pltpu.VMEM((tm, D_MODEL), F32), pltpu.VMEM((nk, tk, D_MODEL), BF16)],
        compiler_params=_params(("arbitrary", "arbitrary")),
        name="mm_ln",
    )(a, w, x, g3, b3)


def _add_ln_body(x_ref, y0_ref, y1_ref, gate_ref, g_ref, b_ref, of_ref, ob_ref):
    gate = gate_ref[...]
    y = gate[:, 0:1] * y0_ref[...] + gate[:, 1:2] * y1_ref[...]
    out = _layer_norm(ALPHA * x_ref[...] + y, g_ref[...], b_ref[...])
    of_ref[...] = out
    ob_ref[...] = out.astype(BF16)


def add_ln(x, y0, y1, y_row0, gate, g, b, ln_layer, tm):
    M = x.shape[0]
    g3 = g.reshape(DEPTH, 1, D_MODEL)
    b3 = b.reshape(DEPTH, 1, D_MODEL)
    row = pl.BlockSpec((tm, D_MODEL), lambda i: (i, 0))
    yrow = pl.BlockSpec((tm, D_MODEL), lambda i: (i + y_row0 // tm, 0))
    vec = pl.BlockSpec((None, 1, D_MODEL), lambda i: (ln_layer, 0, 0))
    return pl.pallas_call(
        _add_ln_body,
        grid=(M // tm,),
        in_specs=[row, yrow, yrow, pl.BlockSpec((tm, TOP_K), lambda i: (i, 0)), vec, vec],
        out_specs=[row, row],
        out_shape=[jax.ShapeDtypeStruct((M, D_MODEL), F32),
                   jax.ShapeDtypeStruct((M, D_MODEL), BF16)],
        compiler_params=_params(("arbitrary",)),
        name="add_ln",
    )(x, y0, y1, gate, g3, b3)


def _swiglu_up_body(x_ref, wg_ref, wu_ref, o_ref, wgb_ref, wub_ref):
    @pl.when(pl.program_id(1) == 0)
    def _():
        wgb_ref[...] = wg_ref[...].astype(BF16)
        wub_ref[...] = wu_ref[...].astype(BF16)

    x = x_ref[...]
    a = _dot(x, wgb_ref[...])
    u = _dot(x, wub_ref[...])
    o_ref[...] = (a * _sigmoid(a) * u).astype(o_ref.dtype)


def swiglu_up(x, wg, wu, layer, tm, tn):
    M, K = x.shape
    N = wg.shape[-1]
    wspec = pl.BlockSpec((None, K, tn), lambda j, i: (layer, 0, j))
    return pl.pallas_call(
        _swiglu_up_body,
        grid=(N // tn, M // tm),
        in_specs=[pl.BlockSpec((tm, K), lambda j, i: (i, 0)), wspec, wspec],
        out_specs=pl.BlockSpec((tm, tn), lambda j, i: (i, j)),
        out_shape=jax.ShapeDtypeStruct((M, N), BF16),
        scratch_shapes=[pltpu.VMEM((K, tn), BF16), pltpu.VMEM((K, tn), BF16)],
        compiler_params=_params(("arbitrary", "arbitrary")),
        name="swiglu_up",
    )(x, wg, wu)


MOE_TM = 128
MOE_GROUP = 20
MOE_CHUNKS = (16, 8, 4, 2, 1)
MOE_TF = 256


def _moe_body(se_ref, sb_ref, sn_ref, sf_ref, x_hbm, wg_ref, wu_ref, wd_ref, out_hbm,
              xbuf, acc, wgb, wub, wdb, sem_in, sem_out, *, nj):
    s = pl.program_id(0)
    j = pl.program_id(1)
    nb = sn_ref[s]
    fill = sf_ref[s] == 1
    active = (nb > 0) & (sf_ref[s] == 0)
    row0 = sb_ref[s] * MOE_TM

    def rows(c, n=1):
        return pl.ds(pl.multiple_of(c * MOE_TM, MOE_TM), n * MOE_TM)

    def x_copy(c):
        return pltpu.make_async_copy(x_hbm.at[pl.ds(row0 + c * MOE_TM, MOE_TM)], xbuf.at[rows(c)], sem_in)

    def out_copy(c):
        return pltpu.make_async_copy(acc.at[rows(c)], out_hbm.at[pl.ds(row0 + c * MOE_TM, MOE_TM)], sem_out)

    def each_block(fn):
        def body(c, carry):
            fn(c)
            return carry
        lax.fori_loop(0, nb, body, 0)

    def zero(c):
        acc[rows(c), :] = jnp.zeros((MOE_TM, acc.shape[-1]), F32)

    @pl.when((j == 0) & active)
    def _():
        each_block(lambda c: x_copy(c).start())
        each_block(zero)
        each_block(lambda c: x_copy(c).wait())

    def compute(c, n):
        x = xbuf[rows(c, n), :]
        a = _dot(x, wgb[...])
        u = _dot(x, wub[...])
        acc[rows(c, n), :] += _dot((a * _sigmoid(a) * u).astype(BF16), wdb[...])

    @pl.when(active)
    def _():
        wgb[...] = wg_ref[...].astype(BF16)
        wub[...] = wu_ref[...].astype(BF16)
        wdb[...] = wd_ref[...].astype(BF16)
        big = MOE_CHUNKS[0]

        def big_chunk(i, carry):
            compute(i * big, big)
            return carry

        n_big = nb // big
        lax.fori_loop(0, n_big, big_chunk, 0)
        rem = nb - n_big * big
        for n in MOE_CHUNKS[1:]:
            @pl.when((rem & n) != 0)
            def _(n=n):
                compute(n_big * big + (rem & ~(2 * n - 1)), n)

    @pl.when((j == nj - 1) & fill)
    def _():
        each_block(zero)

    @pl.when((j == nj - 1) & (nb > 0))
    def _():
        each_block(lambda c: out_copy(c).start())
        each_block(lambda c: out_copy(c).wait())


def moe_experts(tile_e, tile_b, tile_n, tile_f, xs, wg, wu, wd, layer):
    P, D = xs.shape
    F = wg.shape[-1]
    nj = F // MOE_TF
    S = tile_e.shape[0]

    def jj(s, j, sn, sf):
        return jnp.where((sn[s] > 0) & (sf[s] == 0), j, nj - 1)

    up = pl.BlockSpec((None, None, D, MOE_TF),
                      lambda s, j, se, sb, sn, sf: (layer, se[s], 0, jj(s, j, sn, sf)))
    down = pl.BlockSpec((None, None, MOE_TF, D),
                        lambda s, j, se, sb, sn, sf: (layer, se[s], jj(s, j, sn, sf), 0))
    return pl.pallas_call(
        functools.partial(_moe_body, nj=nj),
        grid_spec=pltpu.PrefetchScalarGridSpec(
            num_scalar_prefetch=4,
            grid=(S, nj),
            in_specs=[pl.BlockSpec(memory_space=pl.ANY), up, up, down],
            out_specs=pl.BlockSpec(memory_space=pl.ANY),
            scratch_shapes=[pltpu.VMEM((MOE_GROUP * MOE_TM, D), BF16),
                            pltpu.VMEM((MOE_GROUP * MOE_TM, D), F32),
                            pltpu.VMEM((D, MOE_TF), BF16), pltpu.VMEM((D, MOE_TF), BF16),
                            pltpu.VMEM((MOE_TF, D), BF16),
                            pltpu.SemaphoreType.DMA(()), pltpu.SemaphoreType.DMA(())]),
        out_shape=jax.ShapeDtypeStruct((P, D), F32),
        compiler_params=_params(("arbitrary", "arbitrary")),
        name="moe_experts",
    )(tile_e, tile_b, tile_n, tile_f, xs, wg, wu, wd)


def _router_body(x_ref, w_ref, b_ref, e_ref, g_ref):
    x = x_ref[...]
    w = w_ref[...]
    xh = x.astype(BF16)
    xl = (x - xh.astype(F32)).astype(BF16)
    wh = w.astype(BF16)
    wl = (w - wh.astype(F32)).astype(BF16)
    logits = _dot(xh, wh) + (_dot(xh, wl) + _dot(xl, wh)) + b_ref[...]
    lane = lax.broadcasted_iota(jnp.int32, logits.shape, 1).astype(F32)
    m1 = jnp.max(logits, axis=-1, keepdims=True)
    i1 = jnp.min(jnp.where(logits == m1, lane, float(LANES)), axis=-1, keepdims=True)
    rest = jnp.where(lane == i1, 2.0 * NEG, logits)
    m2 = jnp.max(rest, axis=-1, keepdims=True)
    i2 = jnp.min(jnp.where(rest == m2, lane, float(LANES)), axis=-1, keepdims=True)
    g1 = 1.0 / (1.0 + jnp.exp(m2 - m1))
    two = lax.broadcasted_iota(jnp.int32, (logits.shape[0], TOP_K), 1)
    e_ref[...] = jnp.where(two == 0, i1, i2).astype(jnp.int32)
    g_ref[...] = jnp.where(two == 0, g1, 1.0 - g1)


def router(x, w_router, b_router, j, tm):
    M = x.shape[0]
    n_moe = w_router.shape[0]
    w_pad = jnp.pad(w_router.astype(F32), ((0, 0), (0, 0), (0, LANES - N_EXPERTS)))
    b3 = jnp.pad(b_router.astype(F32).reshape(n_moe, 1, N_EXPERTS), ((0, 0), (0, 0), (0, LANES - N_EXPERTS)),
                 constant_values=NEG)
    return pl.pallas_call(
        _router_body,
        grid=(M // tm,),
        in_specs=[pl.BlockSpec((tm, D_MODEL), lambda i: (i, 0)),
                  pl.BlockSpec((None, D_MODEL, LANES), lambda i: (j, 0, 0)),
                  pl.BlockSpec((None, 1, LANES), lambda i: (j, 0, 0))],
        out_specs=[pl.BlockSpec((tm, TOP_K), lambda i: (i, 0)),
                   pl.BlockSpec((tm, TOP_K), lambda i: (i, 0))],
        out_shape=[jax.ShapeDtypeStruct((M, TOP_K), jnp.int32),
                   jax.ShapeDtypeStruct((M, TOP_K), F32)],
        compiler_params=_params(("arbitrary",)),
        name="router",
    )(x, w_pad, b3)


HG_CHUNK = 128
HG_HPS = 8
HG_LEVELS = 7
_ROW_B = HG_LEVELS * HG_CHUNK
_ROW_U = _ROW_B + HG_CHUNK


def _hgrn_tables():
    C = HG_CHUNK
    t = np.arange(C)[:, None]
    u = np.arange(C)[None, :]
    sel, low, seg = [], [], []
    for lvl in range(HG_LEVELS):
        w = C >> (lvl + 1)
        ref = (t // (2 * w)) * (2 * w) + w - 1
        lower = (t % (2 * w)) >= w
        sel.append(((u > np.minimum(t, ref)) & (u <= np.maximum(t, ref))).astype(np.float32))
        low.append(np.broadcast_to(lower, (C, HG_HPS * HG_D)).astype(np.float32))
        upper_s = (u % (2 * w)) < w
        seg.append((((t // (2 * w)) == (u // (2 * w))) & lower & upper_s).astype(np.float32))
    sel.append((u <= t).astype(np.float32))
    sel.append((u > t).astype(np.float32))
    sel = np.concatenate(sel, axis=0)
    return (jnp.asarray(np.concatenate([sel, sel], axis=1), BF16),
            jnp.asarray(np.stack(low)), jnp.asarray(np.stack(seg)))


def _hgrn_body(sel_ref, low_ref, seg_ref, hq_ref, hf_ref, hi_ref, hg_ref, lb_ref, gn_ref,
               o_ref, s_ref, st_ref, *, nc):
    c = pl.program_id(2)

    @pl.when(c == 0)
    def _():
        st_ref[...] = jnp.zeros_like(st_ref)

    C = HG_CHUNK
    heads = [slice(hh * HG_D, (hh + 1) * HG_D) for hh in range(HG_HPS)]
    q = hq_ref[...]
    hf = hf_ref[...]
    v = hi_ref[...]
    lb = lb_ref[...]
    g = jnp.log(lb + (1.0 - lb) * _sigmoid(hf)) * LOG2E
    kk = (1.0 - lb) * _sigmoid(-hf)
    g_hi = g.astype(BF16)
    g_lo = (g - g_hi.astype(F32)).astype(BF16)
    ex = jnp.exp2(_dot(sel_ref[...], jnp.concatenate([g_hi, g_lo], axis=0)))
    eb = ex[_ROW_B:_ROW_B + C]
    eu = ex[_ROW_U:_ROW_U + C]
    eb_end = eb[C - 1:C, :]
    v_b = v.astype(BF16)

    qe = (q * eb).astype(BF16)
    sts = [st_ref[hh] for hh in range(HG_HPS)]
    o = [_dot_nt(qe[:, hs], sts[hh].astype(BF16)) for hh, hs in enumerate(heads)]
    a = [jnp.zeros((C, C), F32) for _ in heads]
    for lvl in range(HG_LEVELS):
        e = ex[lvl * C:(lvl + 1) * C]
        xt = (jnp.where(low_ref[lvl] > 0.5, q, kk) * e).astype(BF16)
        seg = seg_ref[lvl]
        for hh, hs in enumerate(heads):
            a[hh] = a[hh] + _dot_nt(xt[:, hs], xt[:, hs]) * seg
    kd = (kk * eu).astype(BF16)
    qk = q * kk
    for hh, hs in enumerate(heads):
        o[hh] = o[hh] + _dot(a[hh].astype(BF16), v_b[:, hs])
        o[hh] = o[hh] + jnp.sum(qk[:, hs], axis=-1, keepdims=True) * v[:, hs]
        st_ref[hh] = sts[hh] * eb_end[:, hs] + _dot_tn(v_b[:, hs], kd[:, hs])

    hg = hg_ref[...]
    gate = hg * _sigmoid(hg)
    for hh, hs in enumerate(heads):
        on = o[hh] * lax.rsqrt(jnp.mean(o[hh] * o[hh], axis=-1, keepdims=True) + RMS_EPS) * gn_ref[...]
        o_ref[:, hs] = (on * gate[:, hs]).astype(o_ref.dtype)

    @pl.when(c == nc - 1)
    def _():
        for hh in range(HG_HPS):
            s_ref[hh] = st_ref[hh].T


def hgrn_prompt(z, lb, gnorm, layer, B, T):
    C = HG_CHUNK
    nc = T // C
    W = HG_HPS * HG_D
    sel, low, seg = _hgrn_tables()
    lb3 = lb.reshape(DEPTH, 1, HG_WIDTH)
    gn3 = gnorm.reshape(DEPTH, 1, HG_D)

    def col(off):
        return pl.BlockSpec((C, W), lambda b, h, c: (b * nc + c, off // W + h))

    return pl.pallas_call(
        functools.partial(_hgrn_body, nc=nc),
        grid=(B, HG_HEADS // HG_HPS, nc),
        in_specs=[pl.BlockSpec(sel.shape, lambda b, h, c: (0, 0)),
                  pl.BlockSpec(low.shape, lambda b, h, c: (0, 0, 0)),
                  pl.BlockSpec(seg.shape, lambda b, h, c: (0, 0, 0)),
                  col(HQ_OFF), col(HF_OFF), col(HI_OFF), col(HGATE_OFF),
                  pl.BlockSpec((None, 1, W), lambda b, h, c: (layer, 0, h)),
                  pl.BlockSpec((None, 1, HG_D), lambda b, h, c: (layer, 0, 0))],
        out_specs=[pl.BlockSpec((C, W), lambda b, h, c: (b * nc + c, h)),
                   pl.BlockSpec((None, HG_HPS, HG_D, HG_D), lambda b, h, c: (b, h, 0, 0))],
        out_shape=[jax.ShapeDtypeStruct((B * T, HG_WIDTH), BF16),
                   jax.ShapeDtypeStruct((B, HG_HEADS, HG_D, HG_D), F32)],
        scratch_shapes=[pltpu.VMEM((HG_HPS, HG_D, HG_D), F32)],
        compiler_params=_params(("parallel", "parallel", "arbitrary")),
        name="hgrn_prompt",
    )(sel, low, seg, z, z, z, z, lb3, gn3)


def _hgrn_step_body(s0_ref, qc_ref, fc_ref, vr_ref, gr_ref, lbc_ref, gn_ref, o_ref, s_ref):
    for h in range(HG_HEADS):
        s0 = s0_ref[h]
        hf = fc_ref[h]
        lb = lbc_ref[h]
        f = lb + (1.0 - lb) * _sigmoid(hf)
        kk = (1.0 - lb) * _sigmoid(-hf)
        v = vr_ref[h]
        s_new = f * s0 + kk * v
        s_ref[h] = s_new
        o = jnp.sum(qc_ref[h] * s_new, axis=0, keepdims=True)
        hg = gr_ref[h]
        on = o * lax.rsqrt(jnp.mean(o * o, axis=-1, keepdims=True) + RMS_EPS) * gn_ref[...]
        o_ref[h] = (on * (hg * _sigmoid(hg))).astype(o_ref.dtype)


def hgrn_step(zs, state, lb, gnorm, layer):
    Bs = zs.shape[0]
    heads = lambda off: zs[:, off:off + HG_WIDTH].reshape(Bs, HG_HEADS, HG_D)
    qc = heads(HQ_OFF)[..., None]
    fc = heads(HF_OFF)[..., None]
    vr = heads(HI_OFF)[:, :, None, :]
    gr = heads(HGATE_OFF)[:, :, None, :]
    lbc = lb.reshape(DEPTH, HG_HEADS, HG_D, 1)
    gn3 = gnorm.reshape(DEPTH, 1, HG_D)
    colspec = pl.BlockSpec((None, HG_HEADS, HG_D, 1), lambda b: (b, 0, 0, 0))
    rowspec = pl.BlockSpec((None, HG_HEADS, 1, HG_D), lambda b: (b, 0, 0, 0))
    o, s = pl.pallas_call(
        _hgrn_step_body,
        grid=(Bs,),
        in_specs=[pl.BlockSpec((None, None, HG_HEADS, HG_D, HG_D), lambda b: (layer, b, 0, 0, 0)),
                  colspec, colspec, rowspec, rowspec,
                  pl.BlockSpec((None, HG_HEADS, HG_D, 1), lambda b: (layer, 0, 0, 0)),
                  pl.BlockSpec((None, 1, HG_D), lambda b: (layer, 0, 0))],
        out_specs=[rowspec,
                   pl.BlockSpec((None, HG_HEADS, HG_D, HG_D), lambda b: (b, 0, 0, 0))],
        out_shape=[jax.ShapeDtypeStruct((Bs, HG_HEADS, 1, HG_D), BF16),
                   jax.ShapeDtypeStruct((Bs, HG_HEADS, HG_D, HG_D), F32)],
        compiler_params=_params(("arbitrary",)),
        name="hgrn_step",
    )(state, qc, fc, vr, gr, lbc, gn3)
    return o.reshape(Bs, HG_WIDTH), s


def _slope(g, h):
    return float(2.0 ** (-8.0 * (g * ATT_HPG + h + 1) / ATT_HEADS))


def _band_rows(start, dil):
    return pl.ds(start, BAND, stride=dil) if dil > 1 else pl.ds(start, BAND)


ATT_UNITS = 8


def _attn_units(q_ref, k_ref, v_ref, o_scr, l_scr, starts, prev_starts, prev_offs, dil, pen, has_prev):
    n = range(len(starts))
    row = lax.broadcasted_iota(jnp.int32, (BAND, BAND), 0)
    colj = lax.broadcasted_iota(jnp.int32, (BAND, BAND), 1)
    pen_c = pen * (row - colj).astype(F32)
    rows_c = [_band_rows(s, dil) for s in starts]
    q = [q_ref[r, :].astype(BF16) for r in rows_c]
    kc = [k_ref[r, :].astype(BF16) for r in rows_c]
    vc = [v_ref[r, :].astype(BF16) for r in rows_c]
    s_c = [jnp.where(colj <= row, _dot_nt(q[i], kc[i]) * ATT_SCALE - pen_c, NEG) for i in n]
    m = [jnp.max(s, axis=-1, keepdims=True) for s in s_c]
    if has_prev:
        pen_p = pen_c + pen * float(BAND)
        rows_p = [_band_rows(s, dil) for s in prev_starts]
        kp = [k_ref[r, :].astype(BF16) for r in rows_p]
        vp = [v_ref[r, :].astype(BF16) for r in rows_p]
        s_p = [jnp.where(colj >= row, _dot_nt(q[i], kp[i]) * ATT_SCALE - pen_p, NEG) + prev_offs[i] for i in n]
        m = [jnp.maximum(m[i], jnp.max(s_p[i], axis=-1, keepdims=True)) for i in n]
    p_c = [jnp.exp(s_c[i] - m[i]) for i in n]
    l = [jnp.sum(p, axis=-1, keepdims=True) for p in p_c]
    acc = [_dot(p_c[i].astype(BF16), vc[i]) for i in n]
    if has_prev:
        p_p = [jnp.exp(s_p[i] - m[i]) for i in n]
        l = [l[i] + jnp.sum(p_p[i], axis=-1, keepdims=True) for i in n]
        acc = [acc[i] + _dot(p_p[i].astype(BF16), vp[i]) for i in n]
    for i in n:
        o_scr[rows_c[i], :] = acc[i] / l[i]
        l_scr[rows_c[i], :] = jnp.broadcast_to(m[i] + jnp.log(l[i]), (BAND, ATT_DH))


def _attn_body(slope_ref, *refs, T):
    qkv = refs[:3 * N_GROUPS]
    out_ref, o_scr, l_scr = refs[3 * N_GROUPS:]
    h = pl.program_id(1)
    for g, (_, dil) in enumerate(ATT_GROUPS):
        q_ref, k_ref, v_ref = qkv[3 * g:3 * g + 3]
        pen = slope_ref[g * ATT_HPG + h] * float(dil)
        nqb = T // dil // BAND

        def units(it, carry, q_ref=q_ref, k_ref=k_ref, v_ref=v_ref, g=g, dil=dil, pen=pen, nqb=nqb):
            starts, prev_starts, prev_offs = [], [], []
            for u in range(ATT_UNITS):
                idx = it * ATT_UNITS + u
                r = idx // nqb
                qb = idx - r * nqb
                start = r + qb * (BAND * dil)
                prev_start = jnp.maximum(start - BAND * dil, 0)
                if dil == 1:
                    start = pl.multiple_of(start, BAND)
                    prev_start = pl.multiple_of(prev_start, BAND)
                starts.append(start)
                prev_starts.append(prev_start)
                prev_offs.append(jnp.where(qb > 0, 0.0, NEG))
            _attn_units(q_ref, k_ref, v_ref, o_scr.at[g], l_scr.at[g], starts, prev_starts, prev_offs,
                        dil, pen, has_prev=nqb > 1)
            return carry

        lax.fori_loop(0, dil * nqb // ATT_UNITS, units, 0)

    def merge(i, carry):
        rs = pl.ds(pl.multiple_of(i * BAND, BAND), BAND)
        a0, a1, a2 = l_scr[0, rs, :], l_scr[1, rs, :], l_scr[2, rs, :]
        m = jnp.maximum(jnp.maximum(a0, a1), a2)
        w0, w1, w2 = jnp.exp(a0 - m), jnp.exp(a1 - m), jnp.exp(a2 - m)
        out = (w0 * o_scr[0, rs, :] + w1 * o_scr[1, rs, :] + w2 * o_scr[2, rs, :]) / (w0 + w1 + w2)
        out_ref[rs, :] = out.astype(out_ref.dtype)
        return carry

    lax.fori_loop(0, T // BAND, merge, 0)


def attn_prompt(z, B, T):
    slopes = jnp.asarray([_slope(g, h) for g in range(N_GROUPS) for h in range(ATT_HPG)], F32)

    def col(off, g):
        c0 = off // ATT_DH + g * ATT_HPG
        return pl.BlockSpec((T, ATT_DH), lambda b, h: (b, c0 + h))

    in_specs = [pl.BlockSpec(memory_space=pltpu.SMEM)]
    for g in range(N_GROUPS):
        in_specs += [col(AQ_OFF, g), col(AK_OFF, g), col(AV_OFF, g)]
    return pl.pallas_call(
        functools.partial(_attn_body, T=T),
        grid=(B, ATT_HPG),
        in_specs=in_specs,
        out_specs=pl.BlockSpec((T, ATT_DH), lambda b, h: (b, h)),
        out_shape=jax.ShapeDtypeStruct((B * T, ATT_GW), BF16),
        scratch_shapes=[pltpu.VMEM((N_GROUPS, T, ATT_DH), F32), pltpu.VMEM((N_GROUPS, T, ATT_DH), F32)],
        compiler_params=_params(("parallel", "arbitrary")),
        name="attn_prompt",
    )(slopes, *([z] * (3 * N_GROUPS)))


def _attn_step_body(z_ref, c0_ref, c1_ref, c2_ref, o_ref):
    caches = (c0_ref, c1_ref, c2_ref)
    steps = float(BAND) - lax.broadcasted_iota(jnp.int32, (BAND, 1, 1), 0).astype(F32)
    head = lax.broadcasted_iota(jnp.int32, (ATT_HPG, 1), 0).astype(F32)
    outs, lses = [], []
    for g, (_, dil) in enumerate(ATT_GROUPS):
        slope = jnp.exp2((head + float(g * ATT_HPG + 1)) * (-8.0 / ATT_HEADS))
        pen = slope * float(dil)
        r0 = g * ATT_HPG
        q = z_ref[AQ_OFF // ATT_DH + r0:AQ_OFF // ATT_DH + r0 + ATT_HPG, :]
        kn = z_ref[AK_OFF // ATT_DH + r0:AK_OFF // ATT_DH + r0 + ATT_HPG, :]
        vn = z_ref[AV_OFF // ATT_DH + r0:AV_OFF // ATT_DH + r0 + ATT_HPG, :]
        kc = caches[g][:, 0]
        vc = caches[g][:, 1]
        s_c = jnp.sum(kc * q[None], axis=-1, keepdims=True) * ATT_SCALE - pen[None] * steps
        s_n = jnp.sum(kn * q, axis=-1, keepdims=True) * ATT_SCALE
        m = jnp.maximum(jnp.max(s_c, axis=0), s_n)
        p_c = jnp.exp(s_c - m[None])
        p_n = jnp.exp(s_n - m)
        l = jnp.sum(p_c, axis=0) + p_n
        outs.append((jnp.sum(p_c * vc, axis=0) + p_n * vn) / l)
        lses.append(m + jnp.log(l))
    mm = jnp.maximum(jnp.maximum(lses[0], lses[1]), lses[2])
    ws = [jnp.exp(x - mm) for x in lses]
    out = (ws[0] * outs[0] + ws[1] * outs[1] + ws[2] * outs[2]) / (ws[0] + ws[1] + ws[2])
    o_ref[...] = out.astype(o_ref.dtype)


def attn_step(zs, caches, layer):
    Bs = zs.shape[0]
    zr = zs.reshape(Bs, N_IN // ATT_DH, ATT_DH)
    cache_specs, cache_args = [], []
    for g, (win, dil) in enumerate(ATT_GROUPS):
        c = caches[g]
        assert c.shape[2] == win and win // dil == BAND
        cache_args.append(c.reshape(DEPTH, Bs, BAND, dil, 2, ATT_HPG, ATT_DH))
        cache_specs.append(pl.BlockSpec((None, None, BAND, None, 2, ATT_HPG, ATT_DH),
                                        lambda b: (layer, b, 0, 0, 0, 0, 0)))
    o = pl.pallas_call(
        _attn_step_body,
        grid=(Bs,),
        in_specs=[pl.BlockSpec((None, N_IN // ATT_DH, ATT_DH), lambda b: (b, 0, 0))] + cache_specs,
        out_specs=pl.BlockSpec((None, ATT_HPG, ATT_DH), lambda b: (b, 0, 0)),
        out_shape=jax.ShapeDtypeStruct((Bs, ATT_HPG, ATT_DH), BF16),
        compiler_params=_params(("arbitrary",)),
        name="attn_step",
    )(zr, *cache_args)
    return o.reshape(Bs, ATT_GW)


KV_SLOTS = 2 * ATT_HPG


def _kv_rows_body(k_ref, v_ref, o_ref):
    tk = k_ref.shape[0]
    for c, ref in enumerate((k_ref, v_ref)):
        for h in range(ATT_HPG):
            o_ref[pl.ds(c * ATT_HPG + h, tk, stride=KV_SLOTS), :] = ref[:, h * ATT_DH:(h + 1) * ATT_DH]


def kv_rows(z, group, B, T):
    keep = min(ATT_GROUPS[group][0], T)
    tk = min(keep, 512)
    nt = keep // tk
    first = (T - keep) // tk

    def col(off):
        c0 = off // ATT_GW + group
        return pl.BlockSpec((tk, ATT_GW), lambda b, t: (b * (T // tk) + first + t, c0))

    return pl.pallas_call(
        _kv_rows_body,
        grid=(B, nt),
        in_specs=[col(AK_OFF), col(AV_OFF)],
        out_specs=pl.BlockSpec((tk * KV_SLOTS, ATT_DH), lambda b, t: (b * nt + t, 0)),
        out_shape=jax.ShapeDtypeStruct((B * keep * KV_SLOTS, ATT_DH), F32),
        compiler_params=_params(("arbitrary", "arbitrary")),
        name="kv_rows",
    )(z, z)


def _moe_tables(top_e):
    N = top_e.shape[0]
    A = N * TOP_K
    tm, G = MOE_TM, MOE_GROUP
    n_blocks = -(-A // tm) + N_EXPERTS
    P = n_blocks * tm
    n_tiles = -(-n_blocks // G) + N_EXPERTS + 1
    i32 = jnp.int32
    flat_e = top_e.reshape(A)
    onehot = (flat_e[:, None] == jnp.arange(N_EXPERTS)[None, :]).astype(i32)
    rank = jnp.take_along_axis(jnp.cumsum(onehot, axis=0), flat_e[:, None], axis=1)[:, 0] - 1
    counts = jnp.sum(onehot, axis=0)
    blocks = (counts + tm - 1) // tm
    bend = jnp.cumsum(blocks)
    bstart = bend - blocks
    dest = (bstart[flat_e] * tm + rank).astype(i32)
    rows = jnp.zeros((P,), i32).at[dest].set(jnp.arange(A, dtype=i32) // TOP_K)
    tiles = (blocks + G - 1) // G
    tend = jnp.cumsum(tiles)
    tstart = tend - tiles
    n_used = tend[-1]
    t = jnp.arange(n_tiles, dtype=i32)
    last_e = jnp.max(jnp.where(blocks > 0, jnp.arange(N_EXPERTS), 0))
    e_of = jnp.minimum(jnp.searchsorted(tend, t, side='right'), N_EXPERTS - 1)
    k = t - tstart[e_of]
    used = t < n_used
    tile_e = jnp.where(used, e_of, last_e)
    tile_b = jnp.where(used, bstart[e_of] + k * G, 0)
    tile_n = jnp.where(used, jnp.minimum(G, blocks[e_of] - k * G), 0)
    is_fill = t == n_used
    tile_b = jnp.where(is_fill, bend[-1], tile_b)
    tile_n = jnp.where(is_fill, n_blocks - bend[-1], tile_n)
    return (rows, dest.reshape(N, TOP_K), tile_e.astype(i32), tile_b.astype(i32), tile_n.astype(i32),
            is_fill.astype(i32))


def _mixer(xp_b, xs_b, xp_f, xs_f, i, B, T, caches, state_hgrn, w_in, lb_all, hgrn_norm,
           w_branch_a, w_branch_b, w_out, ln1_g, ln1_b):
    Np = B * T
    Bs = xs_b.shape[0]
    zp = matmul(xp_b, w_in, i, 1024, 1280, F32)
    zs = matmul(xs_b, w_in, i, Bs, 1280, F32)
    oh_p, S_p = hgrn_prompt(zp, lb_all, hgrn_norm, i, B, T)
    oa_p = attn_prompt(zp, B, T)
    oh_s, S_s = hgrn_step(zs, state_hgrn, lb_all, hgrn_norm, i)
    oa_s = attn_step(zs, caches, i)
    mp, ms = branch_merge(oh_p, oa_p, zp, oh_s, oa_s, zs, w_branch_a, w_branch_b, i, 1024)
    hp_f, hp_b = matmul_ln(mp, w_out, i, xp_f, ln1_g, ln1_b, i, 512, 2048)
    hs_f, hs_b = matmul_ln(ms, w_out, i, xs_f, ln1_g, ln1_b, i, Bs, 1024)

    kv_p, kv_s = [], []
    for g in range(N_GROUPS):
        ko, vo = AK_OFF + g * ATT_GW, AV_OFF + g * ATT_GW
        kv_p.append(kv_rows(zp, g, B, T))
        ks = zs[:, ko:ko + ATT_GW].reshape(Bs, 1, ATT_HPG, ATT_DH)
        vs = zs[:, vo:vo + ATT_GW].reshape(Bs, 1, ATT_HPG, ATT_DH)
        kv_s.append(jnp.stack([ks, vs], axis=2))
    return (hp_f, hp_b, hs_f, hs_b, S_p, S_s, kv_p, kv_s)


def _dense_ffn(hp_f, hp_b, hs_f, hs_b, j, ffn_w_gate, ffn_w_up, ffn_w_down, ln2_g, ln2_b, i):
    up = swiglu_up(hp_b, ffn_w_gate, ffn_w_up, j, 2048, 512)
    us = swiglu_up(hs_b, ffn_w_gate, ffn_w_up, j, hs_b.shape[0], 512)
    xp_f, xp_b = matmul_ln(up, ffn_w_down, j, hp_f, ln2_g, ln2_b, i, 512, 512)
    xs_f, xs_b = matmul_ln(us, ffn_w_down, j, hs_f, ln2_g, ln2_b, i, hs_f.shape[0], 512)
    return xp_f, xp_b, xs_f, xs_b


def _moe_ffn(hp_f, hp_b, hs_f, hs_b, j, moe_w_router, moe_b_router, moe_w_gate, moe_w_up, moe_w_down,
             ln2_g, ln2_b, i):
    Np, Bs = hp_f.shape[0], hs_f.shape[0]
    ep, gp = router(hp_f, moe_w_router, moe_b_router, j, 1024)
    es, gs = router(hs_f, moe_w_router, moe_b_router, j, Bs)
    top_e = jnp.concatenate([ep, es], axis=0)
    rows, dest, tile_e, tile_b, tile_n, tile_f = _moe_tables(top_e)
    h_b = jnp.concatenate([hp_b, hs_b], axis=0)
    xs = h_b[rows]
    yb = moe_experts(tile_e, tile_b, tile_n, tile_f, xs, moe_w_gate, moe_w_up, moe_w_down, j)
    y0 = yb[dest[:, 0]]
    y1 = yb[dest[:, 1]]
    xp_f, xp_b = add_ln(hp_f, y0, y1, 0, gp, ln2_g, ln2_b, i, 256)
    xs_f, xs_b = add_ln(hs_f, y0, y1, Np, gs, ln2_g, ln2_b, i, Bs)
    return xp_f, xp_b, xs_f, xs_b


def kernel(x_prompt, x_sample, cache_kv_w128, cache_kv_w512, cache_kv_w2048, state_hgrn,
           w_in, lb_param, hgrn_norm, w_branch_a, w_branch_b, w_out,
           ln1_g, ln1_b, ln2_g, ln2_b, ffn_w_gate, ffn_w_up, ffn_w_down,
           moe_w_router, moe_b_router, moe_w_gate, moe_w_up, moe_w_down):
    B, T, _ = x_prompt.shape
    Bs = x_sample.shape[0]
    lb_cum = jnp.cumsum(jax.nn.softmax(lb_param.astype(F32), axis=0), axis=0)
    lb_all = lb_cum - lb_cum[0]
    caches = (cache_kv_w128, cache_kv_w512, cache_kv_w2048)
    xp_f = x_prompt.reshape(B * T, D_MODEL)
    xs_f = x_sample.reshape(Bs, D_MODEL)
    xp_b = xp_f.astype(BF16)
    xs_b = xs_f.astype(BF16)
    kv_p = [[] for _ in ATT_GROUPS]
    kv_s = [[] for _ in ATT_GROUPS]
    hg_p, hg_s = [], []
    for i in range(DEPTH):
        hp_f, hp_b, hs_f, hs_b, S_p, S_s, kvp, kvs = _mixer(
            xp_b, xs_b, xp_f, xs_f, i, B, T, caches, state_hgrn, w_in, lb_all, hgrn_norm,
            w_branch_a, w_branch_b, w_out, ln1_g, ln1_b)
        for g in range(N_GROUPS):
            kv_p[g].append(kvp[g])
            kv_s[g].append(kvs[g])
        hg_p.append(S_p)
        hg_s.append(S_s)
        j = i // 2
        if i % 2 == 0:
            xp_f, xp_b, xs_f, xs_b = _dense_ffn(hp_f, hp_b, hs_f, hs_b, j, ffn_w_gate, ffn_w_up, ffn_w_down,
                                                ln2_g, ln2_b, i)
        else:
            xp_f, xp_b, xs_f, xs_b = _moe_ffn(hp_f, hp_b, hs_f, hs_b, j, moe_w_router, moe_b_router,
                                              moe_w_gate, moe_w_up, moe_w_down, ln2_g, ln2_b, i)
    def prompt_kv(g):
        keep = min(ATT_GROUPS[g][0], T)
        return jnp.stack(kv_p[g], axis=0).reshape(DEPTH, B, keep, 2, ATT_HPG, ATT_DH)

    return (xp_f.reshape(B, T, D_MODEL), xs_f.reshape(Bs, 1, D_MODEL),
            prompt_kv(0), jnp.stack(kv_s[0], axis=0),
            prompt_kv(1), jnp.stack(kv_s[1], axis=0),
            prompt_kv(2), jnp.stack(kv_s[2], axis=0),
            jnp.stack(hg_p, axis=0), jnp.stack(hg_s, axis=0))
```

```python
import functools

import numpy as np
import jax
import jax.numpy as jnp
from jax import lax
from jax.experimental import pallas as pl
from jax.experimental.pallas import tpu as pltpu

F32 = jnp.float32
BF16 = jnp.bfloat16

D_MODEL = 2048
DEPTH = 4
HG_HEADS = 8
HG_D = 128
HG_WIDTH = HG_HEADS * HG_D
ATT_GROUPS = ((128, 1), (512, 4), (2048, 16))
N_GROUPS = 3
ATT_HPG = 4
ATT_DH = 128
ATT_HEADS = N_GROUPS * ATT_HPG
ATT_GW = ATT_HPG * ATT_DH
ATT_SCALE = ATT_DH ** -0.5
BAND = 128
HQ_OFF, HF_OFF, HI_OFF, HGATE_OFF = 0, 1024, 2048, 3072
AQ_OFF, AK_OFF, AV_OFF = 4096, 5632, 7168
GA_OFF, GB_OFF = 8704, 10752
N_IN = 12800
N_EXPERTS = 8
TOP_K = 2
ALPHA = (2 * DEPTH) ** 0.25
LN_EPS = 1e-5
RMS_EPS = 1e-6
NEG = -1e30
LANES = 128
LOG2E = 1.4426950408889634

VMEM_LIMIT = 56 * 1024 * 1024


def _params(sem):
    return pltpu.CompilerParams(dimension_semantics=sem, vmem_limit_bytes=VMEM_LIMIT)


def _dot(a, b):
    return jnp.dot(a, b, preferred_element_type=F32)


def _dot_nt(a, b):
    return lax.dot_general(a, b, (((1,), (1,)), ((), ())), preferred_element_type=F32)


def _dot_tn(a, b):
    return lax.dot_general(a, b, (((0,), (0,)), ((), ())), preferred_element_type=F32)


def _sigmoid(x):
    return 1.0 / (1.0 + jnp.exp(-x))


def _layer_norm(y, g, b):
    mu = jnp.mean(y, axis=-1, keepdims=True)
    d = y - mu
    var = jnp.mean(d * d, axis=-1, keepdims=True)
    return d * lax.rsqrt(var + LN_EPS) * g + b


def _mm_body(x_ref, w_ref, o_ref, wb_ref):
    @pl.when(pl.program_id(1) == 0)
    def _():
        wb_ref[...] = w_ref[...].astype(BF16)

    o_ref[...] = _dot(x_ref[...], wb_ref[...]).astype(o_ref.dtype)


def matmul(x, w, layer, tm, tn, out_dtype):
    M, K = x.shape
    N = w.shape[-1]
    return pl.pallas_call(
        _mm_body,
        grid=(N // tn, M // tm),
        in_specs=[pl.BlockSpec((tm, K), lambda j, i: (i, 0)),
                  pl.BlockSpec((None, K, tn), lambda j, i: (layer, 0, j))],
        out_specs=pl.BlockSpec((tm, tn), lambda j, i: (i, j)),
        out_shape=jax.ShapeDtypeStruct((M, N), out_dtype),
        scratch_shapes=[pltpu.VMEM((K, tn), BF16)],
        compiler_params=_params(("arbitrary", "arbitrary")),
        name="mm",
    )(x, w)


def _tail_row(nm):
    return lambda j, i: (jnp.minimum(i, nm - 1), 0)


def _merge_body(oh_ref, oa_ref, ga_ref, gb_ref, ohs_ref, oas_ref, gas_ref, gbs_ref, wa_ref, wb_ref,
                o_ref, os_ref, wab_ref, wbb_ref, *, nm):
    i = pl.program_id(1)

    @pl.when(i == 0)
    def _():
        wab_ref[...] = wa_ref[...].astype(BF16)
        wbb_ref[...] = wb_ref[...].astype(BF16)

    def merged(oh, oa, ga, gb, out):
        ya = _dot(oh[...], wab_ref[...])
        yb = _dot(oa[...], wbb_ref[...])
        out[...] = (_sigmoid(ga[...]) * ya + _sigmoid(gb[...]) * yb).astype(out.dtype)

    @pl.when(i < nm)
    def _():
        merged(oh_ref, oa_ref, ga_ref, gb_ref, o_ref)

    @pl.when(i == nm)
    def _():
        merged(ohs_ref, oas_ref, gas_ref, gbs_ref, os_ref)


def branch_merge(oh, oa, z, ohs, oas, zs, w_a, w_b, layer, tm):
    M = oh.shape[0]
    Ms = ohs.shape[0]
    tn = 512
    nm = M // tm
    ga0, gb0 = GA_OFF // tn, GB_OFF // tn
    row = _tail_row(nm)
    return pl.pallas_call(
        functools.partial(_merge_body, nm=nm),
        grid=(D_MODEL // tn, nm + 1),
        in_specs=[pl.BlockSpec((tm, HG_WIDTH), row),
                  pl.BlockSpec((tm, ATT_GW), row),
                  pl.BlockSpec((tm, tn), lambda j, i: (jnp.minimum(i, nm - 1), ga0 + j)),
                  pl.BlockSpec((tm, tn), lambda j, i: (jnp.minimum(i, nm - 1), gb0 + j)),
                  pl.BlockSpec((Ms, HG_WIDTH), lambda j, i: (0, 0)),
                  pl.BlockSpec((Ms, ATT_GW), lambda j, i: (0, 0)),
                  pl.BlockSpec((Ms, tn), lambda j, i: (0, ga0 + j)),
                  pl.BlockSpec((Ms, tn), lambda j, i: (0, gb0 + j)),
                  pl.BlockSpec((None, HG_WIDTH, tn), lambda j, i: (layer, 0, j)),
                  pl.BlockSpec((None, ATT_GW, tn), lambda j, i: (layer, 0, j))],
        out_specs=[pl.BlockSpec((tm, tn), lambda j, i: (jnp.minimum(i, nm - 1), j)),
                   pl.BlockSpec((Ms, tn), lambda j, i: (0, j))],
        out_shape=[jax.ShapeDtypeStruct((M, D_MODEL), BF16), jax.ShapeDtypeStruct((Ms, D_MODEL), BF16)],
        scratch_shapes=[pltpu.VMEM((HG_WIDTH, tn), BF16), pltpu.VMEM((ATT_GW, tn), BF16)],
        compiler_params=_params(("arbitrary", "arbitrary")),
        name="branch_merge",
    )(oh, oa, z, z, ohs, oas, zs, zs, w_a, w_b)


def _mm_ln_body(a_ref, w_ref, x_ref, g_ref, b_ref, of_ref, ob_ref, acc_ref, wb_ref, *, nk):
    i = pl.program_id(0)
    k = pl.program_id(1)

    @pl.when(i == 0)
    def _():
        wb_ref[k] = w_ref[...].astype(BF16)

    @pl.when(k == 0)
    def _():
        acc_ref[...] = jnp.zeros_like(acc_ref)

    acc_ref[...] += _dot(a_ref[...], wb_ref[k])

    @pl.when(k == nk - 1)
    def _():
        out = _layer_norm(ALPHA * x_ref[...] + acc_ref[...], g_ref[...], b_ref[...])
        of_ref[...] = out
        ob_ref[...] = out.astype(BF16)


def matmul_ln(a, w, layer, x, g, b, ln_layer, tm, tk):
    M, K = a.shape
    nk = K // tk
    g3 = g.reshape(DEPTH, 1, D_MODEL)
    b3 = b.reshape(DEPTH, 1, D_MODEL)
    return pl.pallas_call(
        functools.partial(_mm_ln_body, nk=nk),
        grid=(M // tm, nk),
        in_specs=[pl.BlockSpec((tm, tk), lambda i, k: (i, k)),
                  pl.BlockSpec((None, tk, D_MODEL), lambda i, k: (layer, jnp.where(i == 0, k, nk - 1), 0),
                               pipeline_mode=pl.Buffered(1)),
                  pl.BlockSpec((tm, D_MODEL), lambda i, k: (i, 0)),
                  pl.BlockSpec((None, 1, D_MODEL), lambda i, k: (ln_layer, 0, 0)),
                  pl.BlockSpec((None, 1, D_MODEL), lambda i, k: (ln_layer, 0, 0))],
        out_specs=[pl.BlockSpec((tm, D_MODEL), lambda i, k: (i, 0)),
                   pl.BlockSpec((tm, D_MODEL), lambda i, k: (i, 0))],
        out_shape=[jax.ShapeDtypeStruct((M, D_MODEL), F32),
                   jax.ShapeDtypeStruct((M, D_MODEL), BF16)],
        scratch_shapes=[pltpu.VMEM((tm, D_MODEL), F32), pltpu.VMEM((nk, tk, D_MODEL), BF16)],
        compiler_params=_params(("arbitrary", "arbitrary")),
        name="mm_ln",
    )(a, w, x, g3, b3)


def _add_ln_body(x_ref, y0_ref, y1_ref, gate_ref, g_ref, b_ref, of_ref, ob_ref):
    gate = gate_ref[...]
    y = gate[:, 0:1] * y0_ref[...] + gate[:, 1:2] * y1_ref[...]
    out = _layer_norm(ALPHA * x_ref[...] + y, g_ref[...], b_ref[...])
    of_ref[...] = out
    ob_ref[...] = out.astype(BF16)


def add_ln(x, y0, y1, y_row0, gate, g, b, ln_layer, tm):
    M = x.shape[0]
    g3 = g.reshape(DEPTH, 1, D_MODEL)
    b3 = b.reshape(DEPTH, 1, D_MODEL)
    row = pl.BlockSpec((tm, D_MODEL), lambda i: (i, 0))
    yrow = pl.BlockSpec((tm, D_MODEL), lambda i: (i + y_row0 // tm, 0))
    vec = pl.BlockSpec((None, 1, D_MODEL), lambda i: (ln_layer, 0, 0))
    return pl.pallas_call(
        _add_ln_body,
        grid=(M // tm,),
        in_specs=[row, yrow, yrow, pl.BlockSpec((tm, TOP_K), lambda i: (i, 0)), vec, vec],
        out_specs=[row, row],
        out_shape=[jax.ShapeDtypeStruct((M, D_MODEL), F32),
                   jax.ShapeDtypeStruct((M, D_MODEL), BF16)],
        compiler_params=_params(("arbitrary",)),
        name="add_ln",
    )(x, y0, y1, gate, g3, b3)


def _swiglu_up_body(x_ref, wg_ref, wu_ref, o_ref, wgb_ref, wub_ref):
    @pl.when(pl.program_id(1) == 0)
    def _():
        wgb_ref[...] = wg_ref[...].astype(BF16)
        wub_ref[...] = wu_ref[...].astype(BF16)

    x = x_ref[...]
    a = _dot(x, wgb_ref[...])
    u = _dot(x, wub_ref[...])
    o_ref[...] = (a * _sigmoid(a) * u).astype(o_ref.dtype)


def swiglu_up(x, wg, wu, layer, tm, tn):
    M, K = x.shape
    N = wg.shape[-1]
    wspec = pl.BlockSpec((None, K, tn), lambda j, i: (layer, 0, j))
    return pl.pallas_call(
        _swiglu_up_body,
        grid=(N // tn, M // tm),
        in_specs=[pl.BlockSpec((tm, K), lambda j, i: (i, 0)), wspec, wspec],
        out_specs=pl.BlockSpec((tm, tn), lambda j, i: (i, j)),
        out_shape=jax.ShapeDtypeStruct((M, N), BF16),
        scratch_shapes=[pltpu.VMEM((K, tn), BF16), pltpu.VMEM((K, tn), BF16)],
        compiler_params=_params(("arbitrary", "arbitrary")),
        name="swiglu_up",
    )(x, wg, wu)


MOE_TM = 128
MOE_GROUP = 20
MOE_CHUNKS = (16, 8, 4, 2, 1)
MOE_TF = 256


def _moe_body(se_ref, sb_ref, sn_ref, sf_ref, x_hbm, wg_ref, wu_ref, wd_ref, out_hbm,
              xbuf, acc, wgb, wub, wdb, sem_in, sem_out, *, nj):
    s = pl.program_id(0)
    j = pl.program_id(1)
    nb = sn_ref[s]
    fill = sf_ref[s] == 1
    active = (nb > 0) & (sf_ref[s] == 0)
    row0 = sb_ref[s] * MOE_TM

    def rows(c, n=1):
        return pl.ds(pl.multiple_of(c * MOE_TM, MOE_TM), n * MOE_TM)

    def x_copy(c):
        return pltpu.make_async_copy(x_hbm.at[pl.ds(row0 + c * MOE_TM, MOE_TM)], xbuf.at[rows(c)], sem_in)

    def out_copy(c):
        return pltpu.make_async_copy(acc.at[rows(c)], out_hbm.at[pl.ds(row0 + c * MOE_TM, MOE_TM)], sem_out)

    def each_block(fn):
        def body(c, carry):
            fn(c)
            return carry
        lax.fori_loop(0, nb, body, 0)

    def zero(c):
        acc[rows(c), :] = jnp.zeros((MOE_TM, acc.shape[-1]), F32)

    @pl.when((j == 0) & active)
    def _():
        each_block(lambda c: x_copy(c).start())
        each_block(zero)
        each_block(lambda c: x_copy(c).wait())

    def compute(c, n):
        x = xbuf[rows(c, n), :]
        a = _dot(x, wgb[...])
        u = _dot(x, wub[...])
        acc[rows(c, n), :] += _dot((a * _sigmoid(a) * u).astype(BF16), wdb[...])

    @pl.when(active)
    def _():
        wgb[...] = wg_ref[...].astype(BF16)
        wub[...] = wu_ref[...].astype(BF16)
        wdb[...] = wd_ref[...].astype(BF16)
        big = MOE_CHUNKS[0]

        def big_chunk(i, carry):
            compute(i * big, big)
            return carry

        n_big = nb // big
        lax.fori_loop(0, n_big, big_chunk, 0)
        rem = nb - n_big * big
        for n in MOE_CHUNKS[1:]:
            @pl.when((rem & n) != 0)
            def _(n=n):
                compute(n_big * big + (rem & ~(2 * n - 1)), n)

    @pl.when((j == nj - 1) & fill)
    def _():
        each_block(zero)

    @pl.when((j == nj - 1) & (nb > 0))
    def _():
        each_block(lambda c: out_copy(c).start())
        each_block(lambda c: out_copy(c).wait())


def moe_experts(tile_e, tile_b, tile_n, tile_f, xs, wg, wu, wd, layer):
    P, D = xs.shape
    F = wg.shape[-1]
    nj = F // MOE_TF
    S = tile_e.shape[0]

    def jj(s, j, sn, sf):
        return jnp.where((sn[s] > 0) & (sf[s] == 0), j, nj - 1)

    up = pl.BlockSpec((None, None, D, MOE_TF),
                      lambda s, j, se, sb, sn, sf: (layer, se[s], 0, jj(s, j, sn, sf)))
    down = pl.BlockSpec((None, None, MOE_TF, D),
                        lambda s, j, se, sb, sn, sf: (layer, se[s], jj(s, j, sn, sf), 0))
    return pl.pallas_call(
        functools.partial(_moe_body, nj=nj),
        grid_spec=pltpu.PrefetchScalarGridSpec(
            num_scalar_prefetch=4,
            grid=(S, nj),
            in_specs=[pl.BlockSpec(memory_space=pl.ANY), up, up, down],
            out_specs=pl.BlockSpec(memory_space=pl.ANY),
            scratch_shapes=[pltpu.VMEM((MOE_GROUP * MOE_TM, D), BF16),
                            pltpu.VMEM((MOE_GROUP * MOE_TM, D), F32),
                            pltpu.VMEM((D, MOE_TF), BF16), pltpu.VMEM((D, MOE_TF), BF16),
                            pltpu.VMEM((MOE_TF, D), BF16),
                            pltpu.SemaphoreType.DMA(()), pltpu.SemaphoreType.DMA(())]),
        out_shape=jax.ShapeDtypeStruct((P, D), F32),
        compiler_params=_params(("arbitrary", "arbitrary")),
        name="moe_experts",
    )(tile_e, tile_b, tile_n, tile_f, xs, wg, wu, wd)


def _router_body(x_ref, w_ref, b_ref, e_ref, g_ref):
    x = x_ref[...]
    w = w_ref[...]
    xh = x.astype(BF16)
    xl = (x - xh.astype(F32)).astype(BF16)
    wh = w.astype(BF16)
    wl = (w - wh.astype(F32)).astype(BF16)
    logits = _dot(xh, wh) + (_dot(xh, wl) + _dot(xl, wh)) + b_ref[...]
    lane = lax.broadcasted_iota(jnp.int32, logits.shape, 1).astype(F32)
    m1 = jnp.max(logits, axis=-1, keepdims=True)
    i1 = jnp.min(jnp.where(logits == m1, lane, float(LANES)), axis=-1, keepdims=True)
    rest = jnp.where(lane == i1, 2.0 * NEG, logits)
    m2 = jnp.max(rest, axis=-1, keepdims=True)
    i2 = jnp.min(jnp.where(rest == m2, lane, float(LANES)), axis=-1, keepdims=True)
    g1 = 1.0 / (1.0 + jnp.exp(m2 - m1))
    two = lax.broadcasted_iota(jnp.int32, (logits.shape[0], TOP_K), 1)
    e_ref[...] = jnp.where(two == 0, i1, i2).astype(jnp.int32)
    g_ref[...] = jnp.where(two == 0, g1, 1.0 - g1)


def router(x, w_router, b_router, j, tm):
    M = x.shape[0]
    n_moe = w_router.shape[0]
    w_pad = jnp.pad(w_router.astype(F32), ((0, 0), (0, 0), (0, LANES - N_EXPERTS)))
    b3 = jnp.pad(b_router.astype(F32).reshape(n_moe, 1, N_EXPERTS), ((0, 0), (0, 0), (0, LANES - N_EXPERTS)),
                 constant_values=NEG)
    return pl.pallas_call(
        _router_body,
        grid=(M // tm,),
        in_specs=[pl.BlockSpec((tm, D_MODEL), lambda i: (i, 0)),
                  pl.BlockSpec((None, D_MODEL, LANES), lambda i: (j, 0, 0)),
                  pl.BlockSpec((None, 1, LANES), lambda i: (j, 0, 0))],
        out_specs=[pl.BlockSpec((tm, TOP_K), lambda i: (i, 0)),
                   pl.BlockSpec((tm, TOP_K), lambda i: (i, 0))],
        out_shape=[jax.ShapeDtypeStruct((M, TOP_K), jnp.int32),
                   jax.ShapeDtypeStruct((M, TOP_K), F32)],
        compiler_params=_params(("arbitrary",)),
        name="router",
    )(x, w_pad, b3)


HG_CHUNK = 128
HG_HPS = 8
HG_LEVELS = 7
_ROW_B = HG_LEVELS * HG_CHUNK
_ROW_U = _ROW_B + HG_CHUNK


def _hgrn_tables():
    C = HG_CHUNK
    t = np.arange(C)[:, None]
    u = np.arange(C)[None, :]
    sel, low, seg = [], [], []
    for lvl in range(HG_LEVELS):
        w = C >> (lvl + 1)
        ref = (t // (2 * w)) * (2 * w) + w - 1
        lower = (t % (2 * w)) >= w
        sel.append(((u > np.minimum(t, ref)) & (u <= np.maximum(t, ref))).astype(np.float32))
        low.append(np.broadcast_to(lower, (C, HG_HPS * HG_D)).astype(np.float32))
        upper_s = (u % (2 * w)) < w
        seg.append((((t // (2 * w)) == (u // (2 * w))) & lower & upper_s).astype(np.float32))
    sel.append((u <= t).astype(np.float32))
    sel.append((u > t).astype(np.float32))
    sel = np.concatenate(sel, axis=0)
    return (jnp.asarray(np.concatenate([sel, sel], axis=1), BF16),
            jnp.asarray(np.stack(low)), jnp.asarray(np.stack(seg)))


def _hgrn_body(sel_ref, low_ref, seg_ref, hq_ref, hf_ref, hi_ref, hg_ref, lb_ref, gn_ref,
               o_ref, s_ref, st_ref, *, nc):
    c = pl.program_id(2)

    @pl.when(c == 0)
    def _():
        st_ref[...] = jnp.zeros_like(st_ref)

    C = HG_CHUNK
    heads = [slice(hh * HG_D, (hh + 1) * HG_D) for hh in range(HG_HPS)]
    q = hq_ref[...]
    hf = hf_ref[...]
    v = hi_ref[...]
    lb = lb_ref[...]
    g = jnp.log(lb + (1.0 - lb) * _sigmoid(hf)) * LOG2E
    kk = (1.0 - lb) * _sigmoid(-hf)
    g_hi = g.astype(BF16)
    g_lo = (g - g_hi.astype(F32)).astype(BF16)
    ex = jnp.exp2(_dot(sel_ref[...], jnp.concatenate([g_hi, g_lo], axis=0)))
    eb = ex[_ROW_B:_ROW_B + C]
    eu = ex[_ROW_U:_ROW_U + C]
    eb_end = eb[C - 1:C, :]
    v_b = v.astype(BF16)

    qe = (q * eb).astype(BF16)
    sts = [st_ref[hh] for hh in range(HG_HPS)]
    o = [_dot_nt(qe[:, hs], sts[hh].astype(BF16)) for hh, hs in enumerate(heads)]
    a = [jnp.zeros((C, C), F32) for _ in heads]
    for lvl in range(HG_LEVELS):
        e = ex[lvl * C:(lvl + 1) * C]
        xt = (jnp.where(low_ref[lvl] > 0.5, q, kk) * e).astype(BF16)
        seg = seg_ref[lvl]
        for hh, hs in enumerate(heads):
            a[hh] = a[hh] + _dot_nt(xt[:, hs], xt[:, hs]) * seg
    kd = (kk * eu).astype(BF16)
    qk = q * kk
    for hh, hs in enumerate(heads):
        o[hh] = o[hh] + _dot(a[hh].astype(BF16), v_b[:, hs])
        o[hh] = o[hh] + jnp.sum(qk[:, hs], axis=-1, keepdims=True) * v[:, hs]
        st_ref[hh] = sts[hh] * eb_end[:, hs] + _dot_tn(v_b[:, hs], kd[:, hs])

    hg = hg_ref[...]
    gate = hg * _sigmoid(hg)
    for hh, hs in enumerate(heads):
        on = o[hh] * lax.rsqrt(jnp.mean(o[hh] * o[hh], axis=-1, keepdims=True) + RMS_EPS) * gn_ref[...]
        o_ref[:, hs] = (on * gate[:, hs]).astype(o_ref.dtype)

    @pl.when(c == nc - 1)
    def _():
        for hh in range(HG_HPS):
            s_ref[hh] = st_ref[hh].T


def hgrn_prompt(z, lb, gnorm, layer, B, T):
    C = HG_CHUNK
    nc = T // C
    W = HG_HPS * HG_D
    sel, low, seg = _hgrn_tables()
    lb3 = lb.reshape(DEPTH, 1, HG_WIDTH)
    gn3 = gnorm.reshape(DEPTH, 1, HG_D)

    def col(off):
        return pl.BlockSpec((C, W), lambda b, h, c: (b * nc + c, off // W + h))

    return pl.pallas_call(
        functools.partial(_hgrn_body, nc=nc),
        grid=(B, HG_HEADS // HG_HPS, nc),
        in_specs=[pl.BlockSpec(sel.shape, lambda b, h, c: (0, 0)),
                  pl.BlockSpec(low.shape, lambda b, h, c: (0, 0, 0)),
                  pl.BlockSpec(seg.shape, lambda b, h, c: (0, 0, 0)),
                  col(HQ_OFF), col(HF_OFF), col(HI_OFF), col(HGATE_OFF),
                  pl.BlockSpec((None, 1, W), lambda b, h, c: (layer, 0, h)),
                  pl.BlockSpec((None, 1, HG_D), lambda b, h, c: (layer, 0, 0))],
        out_specs=[pl.BlockSpec((C, W), lambda b, h, c: (b * nc + c, h)),
                   pl.BlockSpec((None, HG_HPS, HG_D, HG_D), lambda b, h, c: (b, h, 0, 0))],
        out_shape=[jax.ShapeDtypeStruct((B * T, HG_WIDTH), BF16),
                   jax.ShapeDtypeStruct((B, HG_HEADS, HG_D, HG_D), F32)],
        scratch_shapes=[pltpu.VMEM((HG_HPS, HG_D, HG_D), F32)],
        compiler_params=_params(("parallel", "parallel", "arbitrary")),
        name="hgrn_prompt",
    )(sel, low, seg, z, z, z, z, lb3, gn3)


def _hgrn_step_body(s0_ref, qc_ref, fc_ref, vr_ref, gr_ref, lbc_ref, gn_ref, o_ref, s_ref):
    for h in range(HG_HEADS):
        s0 = s0_ref[h]
        hf = fc_ref[h]
        lb = lbc_ref[h]
        f = lb + (1.0 - lb) * _sigmoid(hf)
        kk = (1.0 - lb) * _sigmoid(-hf)
        v = vr_ref[h]
        s_new = f * s0 + kk * v
        s_ref[h] = s_new
        o = jnp.sum(qc_ref[h] * s_new, axis=0, keepdims=True)
        hg = gr_ref[h]
        on = o * lax.rsqrt(jnp.mean(o * o, axis=-1, keepdims=True) + RMS_EPS) * gn_ref[...]
        o_ref[h] = (on * (hg * _sigmoid(hg))).astype(o_ref.dtype)


def hgrn_step(zs, state, lb, gnorm, layer):
    Bs = zs.shape[0]
    heads = lambda off: zs[:, off:off + HG_WIDTH].reshape(Bs, HG_HEADS, HG_D)
    qc = heads(HQ_OFF)[..., None]
    fc = heads(HF_OFF)[..., None]
    vr = heads(HI_OFF)[:, :, None, :]
    gr = heads(HGATE_OFF)[:, :, None, :]
    lbc = lb.reshape(DEPTH, HG_HEADS, HG_D, 1)
    gn3 = gnorm.reshape(DEPTH, 1, HG_D)
    colspec = pl.BlockSpec((None, HG_HEADS, HG_D, 1), lambda b: (b, 0, 0, 0))
    rowspec = pl.BlockSpec((None, HG_HEADS, 1, HG_D), lambda b: (b, 0, 0, 0))
    o, s = pl.pallas_call(
        _hgrn_step_body,
        grid=(Bs,),
        in_specs=[pl.BlockSpec((None, None, HG_HEADS, HG_D, HG_D), lambda b: (layer, b, 0, 0, 0)),
                  colspec, colspec, rowspec, rowspec,
                  pl.BlockSpec((None, HG_HEADS, HG_D, 1), lambda b: (layer, 0, 0, 0)),
                  pl.BlockSpec((None, 1, HG_D), lambda b: (layer, 0, 0))],
        out_specs=[rowspec,
                   pl.BlockSpec((None, HG_HEADS, HG_D, HG_D), lambda b: (b, 0, 0, 0))],
        out_shape=[jax.ShapeDtypeStruct((Bs, HG_HEADS, 1, HG_D), BF16),
                   jax.ShapeDtypeStruct((Bs, HG_HEADS, HG_D, HG_D), F32)],
        compiler_params=_params(("arbitrary",)),
        name="hgrn_step",
    )(state, qc, fc, vr, gr, lbc, gn3)
    return o.reshape(Bs, HG_WIDTH), s


def _slope(g, h):
    return float(2.0 ** (-8.0 * (g * ATT_HPG + h + 1) / ATT_HEADS))


def _band_rows(start, dil):
    return pl.ds(start, BAND, stride=dil) if dil > 1 else pl.ds(start, BAND)


ATT_UNITS = 8


def _attn_units(q_ref, k_ref, v_ref, o_scr, l_scr, starts, prev_starts, prev_offs, dil, pen, has_prev):
    n = range(len(starts))
    row = lax.broadcasted_iota(jnp.int32, (BAND, BAND), 0)
    colj = lax.broadcasted_iota(jnp.int32, (BAND, BAND), 1)
    pen_c = pen * (row - colj).astype(F32)
    rows_c = [_band_rows(s, dil) for s in starts]
    q = [q_ref[r, :].astype(BF16) for r in rows_c]
    kc = [k_ref[r, :].astype(BF16) for r in rows_c]
    vc = [v_ref[r, :].astype(BF16) for r in rows_c]
    s_c = [jnp.where(colj <= row, _dot_nt(q[i], kc[i]) * ATT_SCALE - pen_c, NEG) for i in n]
    m = [jnp.max(s, axis=-1, keepdims=True) for s in s_c]
    if has_prev:
        pen_p = pen_c + pen * float(BAND)
        rows_p = [_band_rows(s, dil) for s in prev_starts]
        kp = [k_ref[r, :].astype(BF16) for r in rows_p]
        vp = [v_ref[r, :].astype(BF16) for r in rows_p]
        s_p = [jnp.where(colj >= row, _dot_nt(q[i], kp[i]) * ATT_SCALE - pen_p, NEG) + prev_offs[i] for i in n]
        m = [jnp.maximum(m[i], jnp.max(s_p[i], axis=-1, keepdims=True)) for i in n]
    p_c = [jnp.exp(s_c[i] - m[i]) for i in n]
    l = [jnp.sum(p, axis=-1, keepdims=True) for p in p_c]
    acc = [_dot(p_c[i].astype(BF16), vc[i]) for i in n]
    if has_prev:
        p_p = [jnp.exp(s_p[i] - m[i]) for i in n]
        l = [l[i] + jnp.sum(p_p[i], axis=-1, keepdims=True) for i in n]
        acc = [acc[i] + _dot(p_p[i].astype(BF16), vp[i]) for i in n]
    for i in n:
        o_scr[rows_c[i], :] = acc[i] / l[i]
        l_scr[rows_c[i], :] = jnp.broadcast_to(m[i] + jnp.log(l[i]), (BAND, ATT_DH))


def _attn_body(slope_ref, *refs, T):
    qkv = refs[:3 * N_GROUPS]
    out_ref, o_scr, l_scr = refs[3 * N_GROUPS:]
    h = pl.program_id(1)
    for g, (_, dil) in enumerate(ATT_GROUPS):
        q_ref, k_ref, v_ref = qkv[3 * g:3 * g + 3]
        pen = slope_ref[g * ATT_HPG + h] * float(dil)
        nqb = T // dil // BAND

        def units(it, carry, q_ref=q_ref, k_ref=k_ref, v_ref=v_ref, g=g, dil=dil, pen=pen, nqb=nqb):
            starts, prev_starts, prev_offs = [], [], []
            for u in range(ATT_UNITS):
                idx = it * ATT_UNITS + u
                r = idx // nqb
                qb = idx - r * nqb
                start = r + qb * (BAND * dil)
                prev_start = jnp.maximum(start - BAND * dil, 0)
                if dil == 1:
                    start = pl.multiple_of(start, BAND)
                    prev_start = pl.multiple_of(prev_start, BAND)
                starts.append(start)
                prev_starts.append(prev_start)
                prev_offs.append(jnp.where(qb > 0, 0.0, NEG))
            _attn_units(q_ref, k_ref, v_ref, o_scr.at[g], l_scr.at[g], starts, prev_starts, prev_offs,
                        dil, pen, has_prev=nqb > 1)
            return carry

        lax.fori_loop(0, dil * nqb // ATT_UNITS, units, 0)

    def merge(i, carry):
        rs = pl.ds(pl.multiple_of(i * BAND, BAND), BAND)
        a0, a1, a2 = l_scr[0, rs, :], l_scr[1, rs, :], l_scr[2, rs, :]
        m = jnp.maximum(jnp.maximum(a0, a1), a2)
        w0, w1, w2 = jnp.exp(a0 - m), jnp.exp(a1 - m), jnp.exp(a2 - m)
        out = (w0 * o_scr[0, rs, :] + w1 * o_scr[1, rs, :] + w2 * o_scr[2, rs, :]) / (w0 + w1 + w2)
        out_ref[rs, :] = out.astype(out_ref.dtype)
        return carry

    lax.fori_loop(0, T // BAND, merge, 0)


def attn_prompt(z, B, T):
    slopes = jnp.asarray([_slope(g, h) for g in range(N_GROUPS) for h in range(ATT_HPG)], F32)

    def col(off, g):
        c0 = off // ATT_DH + g * ATT_HPG
        return pl.BlockSpec((T, ATT_DH), lambda b, h: (b, c0 + h))

    in_specs = [pl.BlockSpec(memory_space=pltpu.SMEM)]
    for g in range(N_GROUPS):
        in_specs += [col(AQ_OFF, g), col(AK_OFF, g), col(AV_OFF, g)]
    return pl.pallas_call(
        functools.partial(_attn_body, T=T),
        grid=(B, ATT_HPG),
        in_specs=in_specs,
        out_specs=pl.BlockSpec((T, ATT_DH), lambda b, h: (b, h)),
        out_shape=jax.ShapeDtypeStruct((B * T, ATT_GW), BF16),
        scratch_shapes=[pltpu.VMEM((N_GROUPS, T, ATT_DH), F32), pltpu.VMEM((N_GROUPS, T, ATT_DH), F32)],
        compiler_params=_params(("parallel", "arbitrary")),
        name="attn_prompt",
    )(slopes, *([z] * (3 * N_GROUPS)))


def _attn_step_body(z_ref, c0_ref, c1_ref, c2_ref, o_ref):
    caches = (c0_ref, c1_ref, c2_ref)
    steps = float(BAND) - lax.broadcasted_iota(jnp.int32, (BAND, 1, 1), 0).astype(F32)
    head = lax.broadcasted_iota(jnp.int32, (ATT_HPG, 1), 0).astype(F32)
    outs, lses = [], []
    for g, (_, dil) in enumerate(ATT_GROUPS):
        slope = jnp.exp2((head + float(g * ATT_HPG + 1)) * (-8.0 / ATT_HEADS))
        pen = slope * float(dil)
        r0 = g * ATT_HPG
        q = z_ref[AQ_OFF // ATT_DH + r0:AQ_OFF // ATT_DH + r0 + ATT_HPG, :]
        kn = z_ref[AK_OFF // ATT_DH + r0:AK_OFF // ATT_DH + r0 + ATT_HPG, :]
        vn = z_ref[AV_OFF // ATT_DH + r0:AV_OFF // ATT_DH + r0 + ATT_HPG, :]
        kc = caches[g][:, 0]
        vc = caches[g][:, 1]
        s_c = jnp.sum(kc * q[None], axis=-1, keepdims=True) * ATT_SCALE - pen[None] * steps
        s_n = jnp.sum(kn * q, axis=-1, keepdims=True) * ATT_SCALE
        m = jnp.maximum(jnp.max(s_c, axis=0), s_n)
        p_c = jnp.exp(s_c - m[None])
        p_n = jnp.exp(s_n - m)
        l = jnp.sum(p_c, axis=0) + p_n
        outs.append((jnp.sum(p_c * vc, axis=0) + p_n * vn) / l)
        lses.append(m + jnp.log(l))
    mm = jnp.maximum(jnp.maximum(lses[0], lses[1]), lses[2])
    ws = [jnp.exp(x - mm) for x in lses]
    out = (ws[0] * outs[0] + ws[1] * outs[1] + ws[2] * outs[2]) / (ws[0] + ws[1] + ws[2])
    o_ref[...] = out.astype(o_ref.dtype)


def attn_step(zs, caches, layer):
    Bs = zs.shape[0]
    zr = zs.reshape(Bs, N_IN // ATT_DH, ATT_DH)
    cache_specs, cache_args = [], []
    for g, (win, dil) in enumerate(ATT_GROUPS):
        c = caches[g]
        assert c.shape[2] == win and win // dil == BAND
        cache_args.append(c.reshape(DEPTH, Bs, BAND, dil, 2, ATT_HPG, ATT_DH))
        cache_specs.append(pl.BlockSpec((None, None, BAND, None, 2, ATT_HPG, ATT_DH),
                                        lambda b: (layer, b, 0, 0, 0, 0, 0)))
    o = pl.pallas_call(
        _attn_step_body,
        grid=(Bs,),
        in_specs=[pl.BlockSpec((None, N_IN // ATT_DH, ATT_DH), lambda b: (b, 0, 0))] + cache_specs,
        out_specs=pl.BlockSpec((None, ATT_HPG, ATT_DH), lambda b: (b, 0, 0)),
        out_shape=jax.ShapeDtypeStruct((Bs, ATT_HPG, ATT_DH), BF16),
        compiler_params=_params(("arbitrary",)),
        name="attn_step",
    )(zr, *cache_args)
    return o.reshape(Bs, ATT_GW)


KV_SLOTS = 2 * ATT_HPG


def _kv_rows_body(*refs):
    o_ref = refs[-1]
    d = pl.program_id(0)
    for layer in range(len(refs) // 2):
        @pl.when(d == layer)
        def _(k_ref=refs[2 * layer], v_ref=refs[2 * layer + 1]):
            tk = k_ref.shape[0]
            for c, ref in enumerate((k_ref, v_ref)):
                for h in range(ATT_HPG):
                    o_ref[pl.ds(c * ATT_HPG + h, tk, stride=KV_SLOTS), :] = ref[:, h * ATT_DH:(h + 1) * ATT_DH]


def kv_rows(zs, group, B, T):
    keep = min(ATT_GROUPS[group][0], T)
    tk = min(keep, 512)
    nt = keep // tk
    first = (T - keep) // tk
    n_layers = len(zs)

    def col(off, layer):
        c0 = off // ATT_GW + group

        def index(d, b, t):
            bb = jnp.where(d == layer, b, jnp.where(d < layer, 0, B - 1))
            tt = jnp.where(d == layer, t, jnp.where(d < layer, 0, nt - 1))
            return (bb * (T // tk) + first + tt, c0)

        return pl.BlockSpec((tk, ATT_GW), index)

    in_specs, args = [], []
    for layer, z in enumerate(zs):
        in_specs += [col(AK_OFF, layer), col(AV_OFF, layer)]
        args += [z, z]
    return pl.pallas_call(
        _kv_rows_body,
        grid=(n_layers, B, nt),
        in_specs=in_specs,
        out_specs=pl.BlockSpec((tk * KV_SLOTS, ATT_DH), lambda d, b, t: ((d * B + b) * nt + t, 0)),
        out_shape=jax.ShapeDtypeStruct((n_layers * B * keep * KV_SLOTS, ATT_DH), F32),
        compiler_params=_params(("arbitrary", "arbitrary", "arbitrary")),
        name="kv_rows",
    )(*args)


def _moe_tables(top_e):
    N = top_e.shape[0]
    A = N * TOP_K
    tm, G = MOE_TM, MOE_GROUP
    n_blocks = -(-A // tm) + N_EXPERTS
    P = n_blocks * tm
    n_tiles = -(-n_blocks // G) + N_EXPERTS + 1
    i32 = jnp.int32
    flat_e = top_e.reshape(A)
    onehot = (flat_e[:, None] == jnp.arange(N_EXPERTS)[None, :]).astype(i32)
    rank = jnp.take_along_axis(jnp.cumsum(onehot, axis=0), flat_e[:, None], axis=1)[:, 0] - 1
    counts = jnp.sum(onehot, axis=0)
    blocks = (counts + tm - 1) // tm
    bend = jnp.cumsum(blocks)
    bstart = bend - blocks
    dest = (bstart[flat_e] * tm + rank).astype(i32)
    rows = jnp.zeros((P,), i32).at[dest].set(jnp.arange(A, dtype=i32) // TOP_K)
    tiles = (blocks + G - 1) // G
    tend = jnp.cumsum(tiles)
    tstart = tend - tiles
    n_used = tend[-1]
    t = jnp.arange(n_tiles, dtype=i32)
    last_e = jnp.max(jnp.where(blocks > 0, jnp.arange(N_EXPERTS), 0))
    e_of = jnp.minimum(jnp.searchsorted(tend, t, side='right'), N_EXPERTS - 1)
    k = t - tstart[e_of]
    used = t < n_used
    tile_e = jnp.where(used, e_of, last_e)
    tile_b = jnp.where(used, bstart[e_of] + k * G, 0)
    tile_n = jnp.where(used, jnp.minimum(G, blocks[e_of] - k * G), 0)
    is_fill = t == n_used
    tile_b = jnp.where(is_fill, bend[-1], tile_b)
    tile_n = jnp.where(is_fill, n_blocks - bend[-1], tile_n)
    return (rows, dest.reshape(N, TOP_K), tile_e.astype(i32), tile_b.astype(i32), tile_n.astype(i32),
            is_fill.astype(i32))


def _mixer(xp_b, xs_b, xp_f, xs_f, i, B, T, caches, state_hgrn, w_in, lb_all, hgrn_norm,
           w_branch_a, w_branch_b, w_out, ln1_g, ln1_b):
    Np = B * T
    Bs = xs_b.shape[0]
    zp = matmul(xp_b, w_in, i, 1024, 1280, F32)
    zs = matmul(xs_b, w_in, i, Bs, 1280, F32)
    oh_p, S_p = hgrn_prompt(zp, lb_all, hgrn_norm, i, B, T)
    oa_p = attn_prompt(zp, B, T)
    oh_s, S_s = hgrn_step(zs, state_hgrn, lb_all, hgrn_norm, i)
    oa_s = attn_step(zs, caches, i)
    mp, ms = branch_merge(oh_p, oa_p, zp, oh_s, oa_s, zs, w_branch_a, w_branch_b, i, 1024)
    hp_f, hp_b = matmul_ln(mp, w_out, i, xp_f, ln1_g, ln1_b, i, 512, 2048)
    hs_f, hs_b = matmul_ln(ms, w_out, i, xs_f, ln1_g, ln1_b, i, Bs, 1024)

    kv_p, kv_s = zp, []
    for g in range(N_GROUPS):
        ko, vo = AK_OFF + g * ATT_GW, AV_OFF + g * ATT_GW
        ks = zs[:, ko:ko + ATT_GW].reshape(Bs, 1, ATT_HPG, ATT_DH)
        vs = zs[:, vo:vo + ATT_GW].reshape(Bs, 1, ATT_HPG, ATT_DH)
        kv_s.append(jnp.stack([ks, vs], axis=2))
    return (hp_f, hp_b, hs_f, hs_b, S_p, S_s, kv_p, kv_s)


def _dense_ffn(hp_f, hp_b, hs_f, hs_b, j, ffn_w_gate, ffn_w_up, ffn_w_down, ln2_g, ln2_b, i):
    up = swiglu_up(hp_b, ffn_w_gate, ffn_w_up, j, 2048, 512)
    us = swiglu_up(hs_b, ffn_w_gate, ffn_w_up, j, hs_b.shape[0], 512)
    xp_f, xp_b = matmul_ln(up, ffn_w_down, j, hp_f, ln2_g, ln2_b, i, 512, 512)
    xs_f, xs_b = matmul_ln(us, ffn_w_down, j, hs_f, ln2_g, ln2_b, i, hs_f.shape[0], 512)
    return xp_f, xp_b, xs_f, xs_b


def _moe_ffn(hp_f, hp_b, hs_f, hs_b, j, moe_w_router, moe_b_router, moe_w_gate, moe_w_up, moe_w_down,
             ln2_g, ln2_b, i):
    Np, Bs = hp_f.shape[0], hs_f.shape[0]
    ep, gp = router(hp_f, moe_w_router, moe_b_router, j, 1024)
    es, gs = router(hs_f, moe_w_router, moe_b_router, j, Bs)
    top_e = jnp.concatenate([ep, es], axis=0)
    rows, dest, tile_e, tile_b, tile_n, tile_f = _moe_tables(top_e)
    h_b = jnp.concatenate([hp_b, hs_b], axis=0)
    xs = h_b[rows]
    yb = moe_experts(tile_e, tile_b, tile_n, tile_f, xs, moe_w_gate, moe_w_up, moe_w_down, j)
    y0 = yb[dest[:, 0]]
    y1 = yb[dest[:, 1]]
    xp_f, xp_b = add_ln(hp_f, y0, y1, 0, gp, ln2_g, ln2_b, i, 256)
    xs_f, xs_b = add_ln(hs_f, y0, y1, Np, gs, ln2_g, ln2_b, i, Bs)
    return xp_f, xp_b, xs_f, xs_b


def kernel(x_prompt, x_sample, cache_kv_w128, cache_kv_w512, cache_kv_w2048, state_hgrn,
           w_in, lb_param, hgrn_norm, w_branch_a, w_branch_b, w_out,
           ln1_g, ln1_b, ln2_g, ln2_b, ffn_w_gate, ffn_w_up, ffn_w_down,
           moe_w_router, moe_b_router, moe_w_gate, moe_w_up, moe_w_down):
    B, T, _ = x_prompt.shape
    Bs = x_sample.shape[0]
    lb_cum = jnp.cumsum(jax.nn.softmax(lb_param.astype(F32), axis=0), axis=0)
    lb_all = lb_cum - lb_cum[0]
    caches = (cache_kv_w128, cache_kv_w512, cache_kv_w2048)
    xp_f = x_prompt.reshape(B * T, D_MODEL)
    xs_f = x_sample.reshape(Bs, D_MODEL)
    xp_b = xp_f.astype(BF16)
    xs_b = xs_f.astype(BF16)
    zps = []
    kv_s = [[] for _ in ATT_GROUPS]
    hg_p, hg_s = [], []
    for i in range(DEPTH):
        hp_f, hp_b, hs_f, hs_b, S_p, S_s, zp, kvs = _mixer(
            xp_b, xs_b, xp_f, xs_f, i, B, T, caches, state_hgrn, w_in, lb_all, hgrn_norm,
            w_branch_a, w_branch_b, w_out, ln1_g, ln1_b)
        zps.append(zp)
        for g in range(N_GROUPS):
            kv_s[g].append(kvs[g])
        hg_p.append(S_p)
        hg_s.append(S_s)
        j = i // 2
        if i % 2 == 0:
            xp_f, xp_b, xs_f, xs_b = _dense_ffn(hp_f, hp_b, hs_f, hs_b, j, ffn_w_gate, ffn_w_up, ffn_w_down,
                                                ln2_g, ln2_b, i)
        else:
            xp_f, xp_b, xs_f, xs_b = _moe_ffn(hp_f, hp_b, hs_f, hs_b, j, moe_w_router, moe_b_router,
                                              moe_w_gate, moe_w_up, moe_w_down, ln2_g, ln2_b, i)
    def prompt_kv(g):
        keep = min(ATT_GROUPS[g][0], T)
        return kv_rows(zps, g, B, T).reshape(DEPTH, B, keep, 2, ATT_HPG, ATT_DH)

    return (xp_f.reshape(B, T, D_MODEL), xs_f.reshape(Bs, 1, D_MODEL),
            prompt_kv(0), jnp.stack(kv_s[0], axis=0),
            prompt_kv(1), jnp.stack(kv_s[1], axis=0),
            prompt_kv(2), jnp.stack(kv_s[2], axis=0),
            jnp.stack(hg_p, axis=0), jnp.stack(hg_s, axis=0))
```

```python
import functools

import numpy as np
import jax
import jax.numpy as jnp
from jax import lax
from jax.experimental import pallas as pl
from jax.experimental.pallas import tpu as pltpu

F32 = jnp.float32
BF16 = jnp.bfloat16

D_MODEL = 2048
DEPTH = 4
HG_HEADS = 8
HG_D = 128
HG_WIDTH = HG_HEADS * HG_D
ATT_GROUPS = ((128, 1), (512, 4), (2048, 16))
N_GROUPS = 3
ATT_HPG = 4
ATT_DH = 128
ATT_HEADS = N_GROUPS * ATT_HPG
ATT_GW = ATT_HPG * ATT_DH
ATT_SCALE = ATT_DH ** -0.5
BAND = 128
HQ_OFF, HF_OFF, HI_OFF, HGATE_OFF = 0, 1024, 2048, 3072
AQ_OFF, AK_OFF, AV_OFF = 4096, 5632, 7168
GA_OFF, GB_OFF = 8704, 10752
N_IN = 12800
N_EXPERTS = 8
TOP_K = 2
ALPHA = (2 * DEPTH) ** 0.25
LN_EPS = 1e-5
RMS_EPS = 1e-6
NEG = -1e30
LANES = 128
LOG2E = 1.4426950408889634

VMEM_LIMIT = 56 * 1024 * 1024


def _params(sem):
    return pltpu.CompilerParams(dimension_semantics=sem, vmem_limit_bytes=VMEM_LIMIT)


def _dot(a, b):
    return jnp.dot(a, b, preferred_element_type=F32)


def _dot_nt(a, b):
    return lax.dot_general(a, b, (((1,), (1,)), ((), ())), preferred_element_type=F32)


def _dot_tn(a, b):
    return lax.dot_general(a, b, (((0,), (0,)), ((), ())), preferred_element_type=F32)


def _sigmoid(x):
    return 1.0 / (1.0 + jnp.exp(-x))


def _layer_norm(y, g, b):
    mu = jnp.mean(y, axis=-1, keepdims=True)
    d = y - mu
    var = jnp.mean(d * d, axis=-1, keepdims=True)
    return d * lax.rsqrt(var + LN_EPS) * g + b


def _mm_body(x_ref, w_ref, o_ref, wb_ref):
    @pl.when(pl.program_id(1) == 0)
    def _():
        wb_ref[...] = w_ref[...].astype(BF16)

    o_ref[...] = _dot(x_ref[...], wb_ref[...]).astype(o_ref.dtype)


def matmul(x, w, layer, tm, tn, out_dtype):
    M, K = x.shape
    N = w.shape[-1]
    return pl.pallas_call(
        _mm_body,
        grid=(N // tn, M // tm),
        in_specs=[pl.BlockSpec((tm, K), lambda j, i: (i, 0)),
                  pl.BlockSpec((None, K, tn), lambda j, i: (layer, 0, j))],
        out_specs=pl.BlockSpec((tm, tn), lambda j, i: (i, j)),
        out_shape=jax.ShapeDtypeStruct((M, N), out_dtype),
        scratch_shapes=[pltpu.VMEM((K, tn), BF16)],
        compiler_params=_params(("arbitrary", "arbitrary")),
        name="mm",
    )(x, w)


def _tail_row(nm):
    return lambda j, i: (jnp.minimum(i, nm - 1), 0)


def _merge_body(oh_ref, oa_ref, ga_ref, gb_ref, ohs_ref, oas_ref, gas_ref, gbs_ref, wa_ref, wb_ref,
                o_ref, os_ref, wab_ref, wbb_ref, *, nm):
    i = pl.program_id(1)

    @pl.when(i == 0)
    def _():
        wab_ref[...] = wa_ref[...].astype(BF16)
        wbb_ref[...] = wb_ref[...].astype(BF16)

    def merged(oh, oa, ga, gb, out):
        ya = _dot(oh[...], wab_ref[...])
        yb = _dot(oa[...], wbb_ref[...])
        out[...] = (_sigmoid(ga[...]) * ya + _sigmoid(gb[...]) * yb).astype(out.dtype)

    @pl.when(i < nm)
    def _():
        merged(oh_ref, oa_ref, ga_ref, gb_ref, o_ref)

    @pl.when(i == nm)
    def _():
        merged(ohs_ref, oas_ref, gas_ref, gbs_ref, os_ref)


def branch_merge(oh, oa, z, ohs, oas, zs, w_a, w_b, layer, tm):
    M = oh.shape[0]
    Ms = ohs.shape[0]
    tn = 512
    nm = M // tm
    ga0, gb0 = GA_OFF // tn, GB_OFF // tn
    row = _tail_row(nm)
    return pl.pallas_call(
        functools.partial(_merge_body, nm=nm),
        grid=(D_MODEL // tn, nm + 1),
        in_specs=[pl.BlockSpec((tm, HG_WIDTH), row),
                  pl.BlockSpec((tm, ATT_GW), row),
                  pl.BlockSpec((tm, tn), lambda j, i: (jnp.minimum(i, nm - 1), ga0 + j)),
                  pl.BlockSpec((tm, tn), lambda j, i: (jnp.minimum(i, nm - 1), gb0 + j)),
                  pl.BlockSpec((Ms, HG_WIDTH), lambda j, i: (0, 0)),
                  pl.BlockSpec((Ms, ATT_GW), lambda j, i: (0, 0)),
                  pl.BlockSpec((Ms, tn), lambda j, i: (0, ga0 + j)),
                  pl.BlockSpec((Ms, tn), lambda j, i: (0, gb0 + j)),
                  pl.BlockSpec((None, HG_WIDTH, tn), lambda j, i: (layer, 0, j)),
                  pl.BlockSpec((None, ATT_GW, tn), lambda j, i: (layer, 0, j))],
        out_specs=[pl.BlockSpec((tm, tn), lambda j, i: (jnp.minimum(i, nm - 1), j)),
                   pl.BlockSpec((Ms, tn), lambda j, i: (0, j))],
        out_shape=[jax.ShapeDtypeStruct((M, D_MODEL), BF16), jax.ShapeDtypeStruct((Ms, D_MODEL), BF16)],
        scratch_shapes=[pltpu.VMEM((HG_WIDTH, tn), BF16), pltpu.VMEM((ATT_GW, tn), BF16)],
        compiler_params=_params(("arbitrary", "arbitrary")),
        name="branch_merge",
    )(oh, oa, z, z, ohs, oas, zs, zs, w_a, w_b)


def _mm_ln_body(a_ref, w_ref, x_ref, g_ref, b_ref, of_ref, ob_ref, acc_ref, wb_ref, *, nk):
    i = pl.program_id(0)
    k = pl.program_id(1)

    @pl.when(i == 0)
    def _():
        wb_ref[k] = w_ref[...].astype(BF16)

    @pl.when(k == 0)
    def _():
        acc_ref[...] = jnp.zeros_like(acc_ref)

    acc_ref[...] += _dot(a_ref[...], wb_ref[k])

    @pl.when(k == nk - 1)
    def _():
        out = _layer_norm(ALPHA * x_ref[...] + acc_ref[...], g_ref[...], b_ref[...])
        of_ref[...] = out
        ob_ref[...] = out.astype(BF16)


def matmul_ln(a, w, layer, x, g, b, ln_layer, tm, tk):
    M, K = a.shape
    nk = K // tk
    g3 = g.reshape(DEPTH, 1, D_MODEL)
    b3 = b.reshape(DEPTH, 1, D_MODEL)
    return pl.pallas_call(
        functools.partial(_mm_ln_body, nk=nk),
        grid=(M // tm, nk),
        in_specs=[pl.BlockSpec((tm, tk), lambda i, k: (i, k)),
                  pl.BlockSpec((None, tk, D_MODEL), lambda i, k: (layer, jnp.where(i == 0, k, nk - 1), 0),
                               pipeline_mode=pl.Buffered(1)),
                  pl.BlockSpec((tm, D_MODEL), lambda i, k: (i, 0)),
                  pl.BlockSpec((None, 1, D_MODEL), lambda i, k: (ln_layer, 0, 0)),
                  pl.BlockSpec((None, 1, D_MODEL), lambda i, k: (ln_layer, 0, 0))],
        out_specs=[pl.BlockSpec((tm, D_MODEL), lambda i, k: (i, 0)),
                   pl.BlockSpec((tm, D_MODEL), lambda i, k: (i, 0))],
        out_shape=[jax.ShapeDtypeStruct((M, D_MODEL), F32),
                   jax.ShapeDtypeStruct((M, D_MODEL), BF16)],
        scratch_shapes=[pltpu.VMEM((tm, D_MODEL), F32), pltpu.VMEM((nk, tk, D_MODEL), BF16)],
        compiler_params=_params(("arbitrary", "arbitrary")),
        name="mm_ln",
    )(a, w, x, g3, b3)


def _add_ln_body(x_ref, y0_ref, y1_ref, gate_ref, g_ref, b_ref, of_ref, ob_ref):
    gate = gate_ref[...]
    y = gate[:, 0:1] * y0_ref[...] + gate[:, 1:2] * y1_ref[...]
    out = _layer_norm(ALPHA * x_ref[...] + y, g_ref[...], b_ref[...])
    of_ref[...] = out
    ob_ref[...] = out.astype(BF16)


def add_ln(x, y0, y1, y_row0, gate, g, b, ln_layer, tm):
    M = x.shape[0]
    g3 = g.reshape(DEPTH, 1, D_MODEL)
    b3 = b.reshape(DEPTH, 1, D_MODEL)
    row = pl.BlockSpec((tm, D_MODEL), lambda i: (i, 0))
    yrow = pl.BlockSpec((tm, D_MODEL), lambda i: (i + y_row0 // tm, 0))
    vec = pl.BlockSpec((None, 1, D_MODEL), lambda i: (ln_layer, 0, 0))
    return pl.pallas_call(
        _add_ln_body,
        grid=(M // tm,),
        in_specs=[row, yrow, yrow, pl.BlockSpec((tm, TOP_K), lambda i: (i, 0)), vec, vec],
        out_specs=[row, row],
        out_shape=[jax.ShapeDtypeStruct((M, D_MODEL), F32),
                   jax.ShapeDtypeStruct((M, D_MODEL), BF16)],
        compiler_params=_params(("arbitrary",)),
        name="add_ln",
    )(x, y0, y1, gate, g3, b3)


def _swiglu_up_body(x_ref, wg_ref, wu_ref, o_ref, wgb_ref, wub_ref):
    @pl.when(pl.program_id(1) == 0)
    def _():
        wgb_ref[...] = wg_ref[...].astype(BF16)
        wub_ref[...] = wu_ref[...].astype(BF16)

    x = x_ref[...]
    a = _dot(x, wgb_ref[...])
    u = _dot(x, wub_ref[...])
    o_ref[...] = (a * _sigmoid(a) * u).astype(o_ref.dtype)


def swiglu_up(x, wg, wu, layer, tm, tn):
    M, K = x.shape
    N = wg.shape[-1]
    wspec = pl.BlockSpec((None, K, tn), lambda j, i: (layer, 0, j))
    return pl.pallas_call(
        _swiglu_up_body,
        grid=(N // tn, M // tm),
        in_specs=[pl.BlockSpec((tm, K), lambda j, i: (i, 0)), wspec, wspec],
        out_specs=pl.BlockSpec((tm, tn), lambda j, i: (i, j)),
        out_shape=jax.ShapeDtypeStruct((M, N), BF16),
        scratch_shapes=[pltpu.VMEM((K, tn), BF16), pltpu.VMEM((K, tn), BF16)],
        compiler_params=_params(("arbitrary", "arbitrary")),
        name="swiglu_up",
    )(x, wg, wu)


MOE_TM = 128
MOE_GROUP = 20
MOE_CHUNKS = (16, 8, 4, 2, 1)
MOE_TF = 256


def _moe_body(se_ref, sb_ref, sn_ref, sf_ref, x_hbm, wg_ref, wu_ref, wd_ref, out_hbm,
              xbuf, acc, wgb, wub, wdb, sem_in, sem_out, *, nj):
    s = pl.program_id(0)
    j = pl.program_id(1)
    nb = sn_ref[s]
    fill = sf_ref[s] == 1
    active = (nb > 0) & (sf_ref[s] == 0)
    row0 = sb_ref[s] * MOE_TM

    def rows(c, n=1):
        return pl.ds(pl.multiple_of(c * MOE_TM, MOE_TM), n * MOE_TM)

    def x_copy(c):
        return pltpu.make_async_copy(x_hbm.at[pl.ds(row0 + c * MOE_TM, MOE_TM)], xbuf.at[rows(c)], sem_in)

    def out_copy(c):
        return pltpu.make_async_copy(acc.at[rows(c)], out_hbm.at[pl.ds(row0 + c * MOE_TM, MOE_TM)], sem_out)

    def each_block(fn):
        def body(c, carry):
            fn(c)
            return carry
        lax.fori_loop(0, nb, body, 0)

    def zero(c):
        acc[rows(c), :] = jnp.zeros((MOE_TM, acc.shape[-1]), F32)

    @pl.when((j == 0) & active)
    def _():
        each_block(lambda c: x_copy(c).start())
        each_block(zero)
        each_block(lambda c: x_copy(c).wait())

    def compute(c, n):
        x = xbuf[rows(c, n), :]
        a = _dot(x, wgb[...])
        u = _dot(x, wub[...])
        acc[rows(c, n), :] += _dot((a * _sigmoid(a) * u).astype(BF16), wdb[...])

    @pl.when(active)
    def _():
        wgb[...] = wg_ref[...].astype(BF16)
        wub[...] = wu_ref[...].astype(BF16)
        wdb[...] = wd_ref[...].astype(BF16)
        big = MOE_CHUNKS[0]

        def big_chunk(i, carry):
            compute(i * big, big)
            return carry

        n_big = nb // big
        lax.fori_loop(0, n_big, big_chunk, 0)
        rem = nb - n_big * big
        for n in MOE_CHUNKS[1:]:
            @pl.when((rem & n) != 0)
            def _(n=n):
                compute(n_big * big + (rem & ~(2 * n - 1)), n)

    @pl.when((j == nj - 1) & fill)
    def _():
        each_block(zero)

    @pl.when((j == nj - 1) & (nb > 0))
    def _():
        each_block(lambda c: out_copy(c).start())
        each_block(lambda c: out_copy(c).wait())


def moe_experts(tile_e, tile_b, tile_n, tile_f, xs, wg, wu, wd, layer):
    P, D = xs.shape
    F = wg.shape[-1]
    nj = F // MOE_TF
    S = tile_e.shape[0]

    def jj(s, j, sn, sf):
        return jnp.where((sn[s] > 0) & (sf[s] == 0), j, nj - 1)

    up = pl.BlockSpec((None, None, D, MOE_TF),
                      lambda s, j, se, sb, sn, sf: (layer, se[s], 0, jj(s, j, sn, sf)))
    down = pl.BlockSpec((None, None, MOE_TF, D),
                        lambda s, j, se, sb, sn, sf: (layer, se[s], jj(s, j, sn, sf), 0))
    return pl.pallas_call(
        functools.partial(_moe_body, nj=nj),
        grid_spec=pltpu.PrefetchScalarGridSpec(
            num_scalar_prefetch=4,
            grid=(S, nj),
            in_specs=[pl.BlockSpec(memory_space=pl.ANY), up, up, down],
            out_specs=pl.BlockSpec(memory_space=pl.ANY),
            scratch_shapes=[pltpu.VMEM((MOE_GROUP * MOE_TM, D), BF16),
                            pltpu.VMEM((MOE_GROUP * MOE_TM, D), F32),
                            pltpu.VMEM((D, MOE_TF), BF16), pltpu.VMEM((D, MOE_TF), BF16),
                            pltpu.VMEM((MOE_TF, D), BF16),
                            pltpu.SemaphoreType.DMA(()), pltpu.SemaphoreType.DMA(())]),
        out_shape=jax.ShapeDtypeStruct((P, D), F32),
        compiler_params=_params(("arbitrary", "arbitrary")),
        name="moe_experts",
    )(tile_e, tile_b, tile_n, tile_f, xs, wg, wu, wd)


def _router_body(x_ref, w_ref, b_ref, e_ref, g_ref):
    x = x_ref[...]
    w = w_ref[...]
    xh = x.astype(BF16)
    xl = (x - xh.astype(F32)).astype(BF16)
    wh = w.astype(BF16)
    wl = (w - wh.astype(F32)).astype(BF16)
    logits = _dot(xh, wh) + (_dot(xh, wl) + _dot(xl, wh)) + b_ref[...]
    lane = lax.broadcasted_iota(jnp.int32, logits.shape, 1).astype(F32)
    m1 = jnp.max(logits, axis=-1, keepdims=True)
    i1 = jnp.min(jnp.where(logits == m1, lane, float(LANES)), axis=-1, keepdims=True)
    rest = jnp.where(lane == i1, 2.0 * NEG, logits)
    m2 = jnp.max(rest, axis=-1, keepdims=True)
    i2 = jnp.min(jnp.where(rest == m2, lane, float(LANES)), axis=-1, keepdims=True)
    g1 = 1.0 / (1.0 + jnp.exp(m2 - m1))
    two = lax.broadcasted_iota(jnp.int32, (logits.shape[0], TOP_K), 1)
    e_ref[...] = jnp.where(two == 0, i1, i2).astype(jnp.int32)
    g_ref[...] = jnp.where(two == 0, g1, 1.0 - g1)


def router(x, w_router, b_router, j, tm):
    M = x.shape[0]
    n_moe = w_router.shape[0]
    w_pad = jnp.pad(w_router.astype(F32), ((0, 0), (0, 0), (0, LANES - N_EXPERTS)))
    b3 = jnp.pad(b_router.astype(F32).reshape(n_moe, 1, N_EXPERTS), ((0, 0), (0, 0), (0, LANES - N_EXPERTS)),
                 constant_values=NEG)
    return pl.pallas_call(
        _router_body,
        grid=(M // tm,),
        in_specs=[pl.BlockSpec((tm, D_MODEL), lambda i: (i, 0)),
                  pl.BlockSpec((None, D_MODEL, LANES), lambda i: (j, 0, 0)),
                  pl.BlockSpec((None, 1, LANES), lambda i: (j, 0, 0))],
        out_specs=[pl.BlockSpec((tm, TOP_K), lambda i: (i, 0)),
                   pl.BlockSpec((tm, TOP_K), lambda i: (i, 0))],
        out_shape=[jax.ShapeDtypeStruct((M, TOP_K), jnp.int32),
                   jax.ShapeDtypeStruct((M, TOP_K), F32)],
        compiler_params=_params(("arbitrary",)),
        name="router",
    )(x, w_pad, b3)


HG_CHUNK = 128
HG_HPS = 8
HG_LEVELS = 7
_ROW_B = HG_LEVELS * HG_CHUNK
_ROW_U = _ROW_B + HG_CHUNK


def _hgrn_tables():
    C = HG_CHUNK
    t = np.arange(C)[:, None]
    u = np.arange(C)[None, :]
    sel, low, seg = [], [], []
    for lvl in range(HG_LEVELS):
        w = C >> (lvl + 1)
        ref = (t // (2 * w)) * (2 * w) + w - 1
        lower = (t % (2 * w)) >= w
        sel.append(((u > np.minimum(t, ref)) & (u <= np.maximum(t, ref))).astype(np.float32))
        low.append(np.broadcast_to(lower, (C, HG_HPS * HG_D)).astype(np.float32))
        upper_s = (u % (2 * w)) < w
        seg.append((((t // (2 * w)) == (u // (2 * w))) & lower & upper_s).astype(np.float32))
    sel.append((u <= t).astype(np.float32))
    sel.append((u > t).astype(np.float32))
    sel = np.concatenate(sel, axis=0)
    return (jnp.asarray(np.concatenate([sel, sel], axis=1), BF16),
            jnp.asarray(np.stack(low)), jnp.asarray(np.stack(seg)))


def _hgrn_body(sel_ref, low_ref, seg_ref, hq_ref, hf_ref, hi_ref, hg_ref, lb_ref, gn_ref,
               o_ref, s_ref, st_ref, *, nc):
    c = pl.program_id(2)

    @pl.when(c == 0)
    def _():
        st_ref[...] = jnp.zeros_like(st_ref)

    C = HG_CHUNK
    heads = [slice(hh * HG_D, (hh + 1) * HG_D) for hh in range(HG_HPS)]
    q = hq_ref[...]
    hf = hf_ref[...]
    v = hi_ref[...]
    lb = lb_ref[...]
    g = jnp.log(lb + (1.0 - lb) * _sigmoid(hf)) * LOG2E
    kk = (1.0 - lb) * _sigmoid(-hf)
    g_hi = g.astype(BF16)
    g_lo = (g - g_hi.astype(F32)).astype(BF16)
    ex = jnp.exp2(_dot(sel_ref[...], jnp.concatenate([g_hi, g_lo], axis=0)))
    eb = ex[_ROW_B:_ROW_B + C]
    eu = ex[_ROW_U:_ROW_U + C]
    eb_end = eb[C - 1:C, :]
    v_b = v.astype(BF16)

    qe = (q * eb).astype(BF16)
    sts = [st_ref[hh] for hh in range(HG_HPS)]
    o = [_dot_nt(qe[:, hs], sts[hh].astype(BF16)) for hh, hs in enumerate(heads)]
    a = [jnp.zeros((C, C), F32) for _ in heads]
    for lvl in range(HG_LEVELS):
        e = ex[lvl * C:(lvl + 1) * C]
        xt = (jnp.where(low_ref[lvl] > 0.5, q, kk) * e).astype(BF16)
        seg = seg_ref[lvl]
        for hh, hs in enumerate(heads):
            a[hh] = a[hh] + _dot_nt(xt[:, hs], xt[:, hs]) * seg
    kd = (kk * eu).astype(BF16)
    qk = q * kk
    for hh, hs in enumerate(heads):
        o[hh] = o[hh] + _dot(a[hh].astype(BF16), v_b[:, hs])
        o[hh] = o[hh] + jnp.sum(qk[:, hs], axis=-1, keepdims=True) * v[:, hs]
        st_ref[hh] = sts[hh] * eb_end[:, hs] + _dot_tn(v_b[:, hs], kd[:, hs])

    hg = hg_ref[...]
    gate = hg * _sigmoid(hg)
    for hh, hs in enumerate(heads):
        on = o[hh] * lax.rsqrt(jnp.mean(o[hh] * o[hh], axis=-1, keepdims=True) + RMS_EPS) * gn_ref[...]
        o_ref[:, hs] = (on * gate[:, hs]).astype(o_ref.dtype)

    @pl.when(c == nc - 1)
    def _():
        for hh in range(HG_HPS):
            s_ref[hh] = st_ref[hh].T


def hgrn_prompt(z, lb, gnorm, layer, B, T):
    C = HG_CHUNK
    nc = T // C
    W = HG_HPS * HG_D
    sel, low, seg = _hgrn_tables()
    lb3 = lb.reshape(DEPTH, 1, HG_WIDTH)
    gn3 = gnorm.reshape(DEPTH, 1, HG_D)

    def col(off):
        return pl.BlockSpec((C, W), lambda b, h, c: (b * nc + c, off // W + h))

    return pl.pallas_call(
        functools.partial(_hgrn_body, nc=nc),
        grid=(B, HG_HEADS // HG_HPS, nc),
        in_specs=[pl.BlockSpec(sel.shape, lambda b, h, c: (0, 0)),
                  pl.BlockSpec(low.shape, lambda b, h, c: (0, 0, 0)),
                  pl.BlockSpec(seg.shape, lambda b, h, c: (0, 0, 0)),
                  col(HQ_OFF), col(HF_OFF), col(HI_OFF), col(HGATE_OFF),
                  pl.BlockSpec((None, 1, W), lambda b, h, c: (layer, 0, h)),
                  pl.BlockSpec((None, 1, HG_D), lambda b, h, c: (layer, 0, 0))],
        out_specs=[pl.BlockSpec((C, W), lambda b, h, c: (b * nc + c, h)),
                   pl.BlockSpec((None, HG_HPS, HG_D, HG_D), lambda b, h, c: (b, h, 0, 0))],
        out_shape=[jax.ShapeDtypeStruct((B * T, HG_WIDTH), BF16),
                   jax.ShapeDtypeStruct((B, HG_HEADS, HG_D, HG_D), F32)],
        scratch_shapes=[pltpu.VMEM((HG_HPS, HG_D, HG_D), F32)],
        compiler_params=_params(("parallel", "parallel", "arbitrary")),
        name="hgrn_prompt",
    )(sel, low, seg, z, z, z, z, lb3, gn3)


def _hgrn_step_body(s0_ref, qc_ref, fc_ref, vr_ref, gr_ref, lbc_ref, gn_ref, o_ref, s_ref):
    for h in range(HG_HEADS):
        s0 = s0_ref[h]
        hf = fc_ref[h]
        lb = lbc_ref[h]
        f = lb + (1.0 - lb) * _sigmoid(hf)
        kk = (1.0 - lb) * _sigmoid(-hf)
        v = vr_ref[h]
        s_new = f * s0 + kk * v
        s_ref[h] = s_new
        o = jnp.sum(qc_ref[h] * s_new, axis=0, keepdims=True)
        hg = gr_ref[h]
        on = o * lax.rsqrt(jnp.mean(o * o, axis=-1, keepdims=True) + RMS_EPS) * gn_ref[...]
        o_ref[h] = (on * (hg * _sigmoid(hg))).astype(o_ref.dtype)


def hgrn_step(zs, state, lb, gnorm, layer):
    Bs = zs.shape[0]
    heads = lambda off: zs[:, off:off + HG_WIDTH].reshape(Bs, HG_HEADS, HG_D)
    qc = heads(HQ_OFF)[..., None]
    fc = heads(HF_OFF)[..., None]
    vr = heads(HI_OFF)[:, :, None, :]
    gr = heads(HGATE_OFF)[:, :, None, :]
    lbc = lb.reshape(DEPTH, HG_HEADS, HG_D, 1)
    gn3 = gnorm.reshape(DEPTH, 1, HG_D)
    colspec = pl.BlockSpec((None, HG_HEADS, HG_D, 1), lambda b: (b, 0, 0, 0))
    rowspec = pl.BlockSpec((None, HG_HEADS, 1, HG_D), lambda b: (b, 0, 0, 0))
    o, s = pl.pallas_call(
        _hgrn_step_body,
        grid=(Bs,),
        in_specs=[pl.BlockSpec((None, None, HG_HEADS, HG_D, HG_D), lambda b: (layer, b, 0, 0, 0)),
                  colspec, colspec, rowspec, rowspec,
                  pl.BlockSpec((None, HG_HEADS, HG_D, 1), lambda b: (layer, 0, 0, 0)),
                  pl.BlockSpec((None, 1, HG_D), lambda b: (layer, 0, 0))],
        out_specs=[rowspec,
                   pl.BlockSpec((None, HG_HEADS, HG_D, HG_D), lambda b: (b, 0, 0, 0))],
        out_shape=[jax.ShapeDtypeStruct((Bs, HG_HEADS, 1, HG_D), BF16),
                   jax.ShapeDtypeStruct((Bs, HG_HEADS, HG_D, HG_D), F32)],
        compiler_params=_params(("arbitrary",)),
        name="hgrn_step",
    )(state, qc, fc, vr, gr, lbc, gn3)
    return o.reshape(Bs, HG_WIDTH), s


def _slope(g, h):
    return float(2.0 ** (-8.0 * (g * ATT_HPG + h + 1) / ATT_HEADS))


def _band_rows(start, dil):
    return pl.ds(start, BAND, stride=dil) if dil > 1 else pl.ds(start, BAND)


ATT_UNITS = 8


def _attn_units(q_ref, k_ref, v_ref, o_scr, l_scr, starts, prev_starts, prev_offs, dil, pen, has_prev):
    n = range(len(starts))
    row = lax.broadcasted_iota(jnp.int32, (BAND, BAND), 0)
    colj = lax.broadcasted_iota(jnp.int32, (BAND, BAND), 1)
    pen_c = pen * (row - colj).astype(F32)
    rows_c = [_band_rows(s, dil) for s in starts]
    q = [q_ref[r, :].astype(BF16) for r in rows_c]
    kc = [k_ref[r, :].astype(BF16) for r in rows_c]
    vc = [v_ref[r, :].astype(BF16) for r in rows_c]
    s_c = [jnp.where(colj <= row, _dot_nt(q[i], kc[i]) * ATT_SCALE - pen_c, NEG) for i in n]
    m = [jnp.max(s, axis=-1, keepdims=True) for s in s_c]
    if has_prev:
        pen_p = pen_c + pen * float(BAND)
        rows_p = [_band_rows(s, dil) for s in prev_starts]
        kp = [k_ref[r, :].astype(BF16) for r in rows_p]
        vp = [v_ref[r, :].astype(BF16) for r in rows_p]
        s_p = [jnp.where(colj >= row, _dot_nt(q[i], kp[i]) * ATT_SCALE - pen_p, NEG) + prev_offs[i] for i in n]
        m = [jnp.maximum(m[i], jnp.max(s_p[i], axis=-1, keepdims=True)) for i in n]
    ones = jnp.ones((BAND, ATT_DH), BF16)
    p_c = [jnp.exp(s_c[i] - m[i]) for i in n]
    acc = [_dot(p_c[i].astype(BF16), jnp.concatenate([vc[i], ones], axis=1)) for i in n]
    if has_prev:
        p_p = [jnp.exp(s_p[i] - m[i]) for i in n]
        acc = [acc[i] + _dot(p_p[i].astype(BF16), jnp.concatenate([vp[i], ones], axis=1)) for i in n]
    for i in n:
        l = acc[i][:, ATT_DH:]
        o_scr[rows_c[i], :] = acc[i][:, :ATT_DH] / l
        l_scr[rows_c[i], :] = m[i] + jnp.log(l)


def _attn_body(slope_ref, *refs, T):
    qkv = refs[:3 * N_GROUPS]
    out_ref, o_scr, l_scr = refs[3 * N_GROUPS:]
    h = pl.program_id(1)
    for g, (_, dil) in enumerate(ATT_GROUPS):
        q_ref, k_ref, v_ref = qkv[3 * g:3 * g + 3]
        pen = slope_ref[g * ATT_HPG + h] * float(dil)
        nqb = T // dil // BAND

        def units(it, carry, q_ref=q_ref, k_ref=k_ref, v_ref=v_ref, g=g, dil=dil, pen=pen, nqb=nqb):
            starts, prev_starts, prev_offs = [], [], []
            for u in range(ATT_UNITS):
                idx = it * ATT_UNITS + u
                r = idx // nqb
                qb = idx - r * nqb
                start = r + qb * (BAND * dil)
                prev_start = jnp.maximum(start - BAND * dil, 0)
                if dil == 1:
                    start = pl.multiple_of(start, BAND)
                    prev_start = pl.multiple_of(prev_start, BAND)
                starts.append(start)
                prev_starts.append(prev_start)
                prev_offs.append(jnp.where(qb > 0, 0.0, NEG))
            _attn_units(q_ref, k_ref, v_ref, o_scr.at[g], l_scr.at[g], starts, prev_starts, prev_offs,
                        dil, pen, has_prev=nqb > 1)
            return carry

        lax.fori_loop(0, dil * nqb // ATT_UNITS, units, 0)

    def merge(i, carry):
        rs = pl.ds(pl.multiple_of(i * BAND, BAND), BAND)
        a0, a1, a2 = l_scr[0, rs, :], l_scr[1, rs, :], l_scr[2, rs, :]
        m = jnp.maximum(jnp.maximum(a0, a1), a2)
        w0, w1, w2 = jnp.exp(a0 - m), jnp.exp(a1 - m), jnp.exp(a2 - m)
        out = (w0 * o_scr[0, rs, :] + w1 * o_scr[1, rs, :] + w2 * o_scr[2, rs, :]) / (w0 + w1 + w2)
        out_ref[rs, :] = out.astype(out_ref.dtype)
        return carry

    lax.fori_loop(0, T // BAND, merge, 0)


def attn_prompt(z, B, T):
    slopes = jnp.asarray([_slope(g, h) for g in range(N_GROUPS) for h in range(ATT_HPG)], F32)

    def col(off, g):
        c0 = off // ATT_DH + g * ATT_HPG
        return pl.BlockSpec((T, ATT_DH), lambda b, h: (b, c0 + h))

    in_specs = [pl.BlockSpec(memory_space=pltpu.SMEM)]
    for g in range(N_GROUPS):
        in_specs += [col(AQ_OFF, g), col(AK_OFF, g), col(AV_OFF, g)]
    return pl.pallas_call(
        functools.partial(_attn_body, T=T),
        grid=(B, ATT_HPG),
        in_specs=in_specs,
        out_specs=pl.BlockSpec((T, ATT_DH), lambda b, h: (b, h)),
        out_shape=jax.ShapeDtypeStruct((B * T, ATT_GW), BF16),
        scratch_shapes=[pltpu.VMEM((N_GROUPS, T, ATT_DH), F32), pltpu.VMEM((N_GROUPS, T, ATT_DH), F32)],
        compiler_params=_params(("parallel", "arbitrary")),
        name="attn_prompt",
    )(slopes, *([z] * (3 * N_GROUPS)))


def _attn_step_body(z_ref, c0_ref, c1_ref, c2_ref, o_ref):
    caches = (c0_ref, c1_ref, c2_ref)
    steps = float(BAND) - lax.broadcasted_iota(jnp.int32, (BAND, 1, 1), 0).astype(F32)
    head = lax.broadcasted_iota(jnp.int32, (ATT_HPG, 1), 0).astype(F32)
    outs, lses = [], []
    for g, (_, dil) in enumerate(ATT_GROUPS):
        slope = jnp.exp2((head + float(g * ATT_HPG + 1)) * (-8.0 / ATT_HEADS))
        pen = slope * float(dil)
        r0 = g * ATT_HPG
        q = z_ref[AQ_OFF // ATT_DH + r0:AQ_OFF // ATT_DH + r0 + ATT_HPG, :]
        kn = z_ref[AK_OFF // ATT_DH + r0:AK_OFF // ATT_DH + r0 + ATT_HPG, :]
        vn = z_ref[AV_OFF // ATT_DH + r0:AV_OFF // ATT_DH + r0 + ATT_HPG, :]
        kc = caches[g][:, 0]
        vc = caches[g][:, 1]
        s_c = jnp.sum(kc * q[None], axis=-1, keepdims=True) * ATT_SCALE - pen[None] * steps
        s_n = jnp.sum(kn * q, axis=-1, keepdims=True) * ATT_SCALE
        m = jnp.maximum(jnp.max(s_c, axis=0), s_n)
        p_c = jnp.exp(s_c - m[None])
        p_n = jnp.exp(s_n - m)
        l = jnp.sum(p_c, axis=0) + p_n
        outs.append((jnp.sum(p_c * vc, axis=0) + p_n * vn) / l)
        lses.append(m + jnp.log(l))
    mm = jnp.maximum(jnp.maximum(lses[0], lses[1]), lses[2])
    ws = [jnp.exp(x - mm) for x in lses]
    out = (ws[0] * outs[0] + ws[1] * outs[1] + ws[2] * outs[2]) / (ws[0] + ws[1] + ws[2])
    o_ref[...] = out.astype(o_ref.dtype)


def attn_step(zs, caches, layer):
    Bs = zs.shape[0]
    zr = zs.reshape(Bs, N_IN // ATT_DH, ATT_DH)
    cache_specs, cache_args = [], []
    for g, (win, dil) in enumerate(ATT_GROUPS):
        c = caches[g]
        assert c.shape[2] == win and win // dil == BAND
        cache_args.append(c.reshape(DEPTH, Bs, BAND, dil, 2, ATT_HPG, ATT_DH))
        cache_specs.append(pl.BlockSpec((None, None, BAND, None, 2, ATT_HPG, ATT_DH),
                                        lambda b: (layer, b, 0, 0, 0, 0, 0)))
    o = pl.pallas_call(
        _attn_step_body,
        grid=(Bs,),
        in_specs=[pl.BlockSpec((None, N_IN // ATT_DH, ATT_DH), lambda b: (b, 0, 0))] + cache_specs,
        out_specs=pl.BlockSpec((None, ATT_HPG, ATT_DH), lambda b: (b, 0, 0)),
        out_shape=jax.ShapeDtypeStruct((Bs, ATT_HPG, ATT_DH), BF16),
        compiler_params=_params(("arbitrary",)),
        name="attn_step",
    )(zr, *cache_args)
    return o.reshape(Bs, ATT_GW)


KV_SLOTS = 2 * ATT_HPG


def _kv_rows_body(*refs):
    o_ref = refs[-1]
    d = pl.program_id(0)
    for layer in range(len(refs) // 2):
        @pl.when(d == layer)
        def _(k_ref=refs[2 * layer], v_ref=refs[2 * layer + 1]):
            tk = k_ref.shape[0]
            for c, ref in enumerate((k_ref, v_ref)):
                for h in range(ATT_HPG):
                    o_ref[pl.ds(c * ATT_HPG + h, tk, stride=KV_SLOTS), :] = ref[:, h * ATT_DH:(h + 1) * ATT_DH]


def kv_rows(zs, group, B, T):
    keep = min(ATT_GROUPS[group][0], T)
    tk = min(keep, 512)
    nt = keep // tk
    first = (T - keep) // tk
    n_layers = len(zs)

    def col(off, layer):
        c0 = off // ATT_GW + group

        def index(d, b, t):
            bb = jnp.where(d == layer, b, jnp.where(d < layer, 0, B - 1))
            tt = jnp.where(d == layer, t, jnp.where(d < layer, 0, nt - 1))
            return (bb * (T // tk) + first + tt, c0)

        return pl.BlockSpec((tk, ATT_GW), index)

    in_specs, args = [], []
    for layer, z in enumerate(zs):
        in_specs += [col(AK_OFF, layer), col(AV_OFF, layer)]
        args += [z, z]
    return pl.pallas_call(
        _kv_rows_body,
        grid=(n_layers, B, nt),
        in_specs=in_specs,
        out_specs=pl.BlockSpec((tk * KV_SLOTS, ATT_DH), lambda d, b, t: ((d * B + b) * nt + t, 0)),
        out_shape=jax.ShapeDtypeStruct((n_layers * B * keep * KV_SLOTS, ATT_DH), F32),
        compiler_params=_params(("arbitrary", "arbitrary", "arbitrary")),
        name="kv_rows",
    )(*args)


def _moe_tables(top_e):
    N = top_e.shape[0]
    A = N * TOP_K
    tm, G = MOE_TM, MOE_GROUP
    n_blocks = -(-A // tm) + N_EXPERTS
    P = n_blocks * tm
    n_tiles = -(-n_blocks // G) + N_EXPERTS + 1
    i32 = jnp.int32
    flat_e = top_e.reshape(A)
    onehot = (flat_e[:, None] == jnp.arange(N_EXPERTS)[None, :]).astype(i32)
    rank = jnp.take_along_axis(jnp.cumsum(onehot, axis=0), flat_e[:, None], axis=1)[:, 0] - 1
    counts = jnp.sum(onehot, axis=0)
    blocks = (counts + tm - 1) // tm
    bend = jnp.cumsum(blocks)
    bstart = bend - blocks
    dest = (bstart[flat_e] * tm + rank).astype(i32)
    rows = jnp.zeros((P,), i32).at[dest].set(jnp.arange(A, dtype=i32) // TOP_K)
    tiles = (blocks + G - 1) // G
    tend = jnp.cumsum(tiles)
    tstart = tend - tiles
    n_used = tend[-1]
    t = jnp.arange(n_tiles, dtype=i32)
    last_e = jnp.max(jnp.where(blocks > 0, jnp.arange(N_EXPERTS), 0))
    e_of = jnp.minimum(jnp.searchsorted(tend, t, side='right'), N_EXPERTS - 1)
    k = t - tstart[e_of]
    used = t < n_used
    tile_e = jnp.where(used, e_of, last_e)
    tile_b = jnp.where(used, bstart[e_of] + k * G, 0)
    tile_n = jnp.where(used, jnp.minimum(G, blocks[e_of] - k * G), 0)
    is_fill = t == n_used
    tile_b = jnp.where(is_fill, bend[-1], tile_b)
    tile_n = jnp.where(is_fill, n_blocks - bend[-1], tile_n)
    return (rows, dest.reshape(N, TOP_K), tile_e.astype(i32), tile_b.astype(i32), tile_n.astype(i32),
            is_fill.astype(i32))


def _mixer(xp_b, xs_b, xp_f, xs_f, i, B, T, caches, state_hgrn, w_in, lb_all, hgrn_norm,
           w_branch_a, w_branch_b, w_out, ln1_g, ln1_b):
    Np = B * T
    Bs = xs_b.shape[0]
    zp = matmul(xp_b, w_in, i, 1024, 1280, F32)
    zs = matmul(xs_b, w_in, i, Bs, 1280, F32)
    oh_p, S_p = hgrn_prompt(zp, lb_all, hgrn_norm, i, B, T)
    oa_p = attn_prompt(zp, B, T)
    oh_s, S_s = hgrn_step(zs, state_hgrn, lb_all, hgrn_norm, i)
    oa_s = attn_step(zs, caches, i)
    mp, ms = branch_merge(oh_p, oa_p, zp, oh_s, oa_s, zs, w_branch_a, w_branch_b, i, 1024)
    hp_f, hp_b = matmul_ln(mp, w_out, i, xp_f, ln1_g, ln1_b, i, 512, 2048)
    hs_f, hs_b = matmul_ln(ms, w_out, i, xs_f, ln1_g, ln1_b, i, Bs, 1024)

    kv_p, kv_s = zp, []
    for g in range(N_GROUPS):
        ko, vo = AK_OFF + g * ATT_GW, AV_OFF + g * ATT_GW
        ks = zs[:, ko:ko + ATT_GW].reshape(Bs, 1, ATT_HPG, ATT_DH)
        vs = zs[:, vo:vo + ATT_GW].reshape(Bs, 1, ATT_HPG, ATT_DH)
        kv_s.append(jnp.stack([ks, vs], axis=2))
    return (hp_f, hp_b, hs_f, hs_b, S_p, S_s, kv_p, kv_s)


def _dense_ffn(hp_f, hp_b, hs_f, hs_b, j, ffn_w_gate, ffn_w_up, ffn_w_down, ln2_g, ln2_b, i):
    up = swiglu_up(hp_b, ffn_w_gate, ffn_w_up, j, 2048, 512)
    us = swiglu_up(hs_b, ffn_w_gate, ffn_w_up, j, hs_b.shape[0], 512)
    xp_f, xp_b = matmul_ln(up, ffn_w_down, j, hp_f, ln2_g, ln2_b, i, 512, 512)
    xs_f, xs_b = matmul_ln(us, ffn_w_down, j, hs_f, ln2_g, ln2_b, i, hs_f.shape[0], 512)
    return xp_f, xp_b, xs_f, xs_b


def _moe_ffn(hp_f, hp_b, hs_f, hs_b, j, moe_w_router, moe_b_router, moe_w_gate, moe_w_up, moe_w_down,
             ln2_g, ln2_b, i):
    Np, Bs = hp_f.shape[0], hs_f.shape[0]
    ep, gp = router(hp_f, moe_w_router, moe_b_router, j, 1024)
    es, gs = router(hs_f, moe_w_router, moe_b_router, j, Bs)
    top_e = jnp.concatenate([ep, es], axis=0)
    rows, dest, tile_e, tile_b, tile_n, tile_f = _moe_tables(top_e)
    h_b = jnp.concatenate([hp_b, hs_b], axis=0)
    xs = h_b[rows]
    yb = moe_experts(tile_e, tile_b, tile_n, tile_f, xs, moe_w_gate, moe_w_up, moe_w_down, j)
    y0 = yb[dest[:, 0]]
    y1 = yb[dest[:, 1]]
    xp_f, xp_b = add_ln(hp_f, y0, y1, 0, gp, ln2_g, ln2_b, i, 256)
    xs_f, xs_b = add_ln(hs_f, y0, y1, Np, gs, ln2_g, ln2_b, i, Bs)
    return xp_f, xp_b, xs_f, xs_b


def kernel(x_prompt, x_sample, cache_kv_w128, cache_kv_w512, cache_kv_w2048, state_hgrn,
           w_in, lb_param, hgrn_norm, w_branch_a, w_branch_b, w_out,
           ln1_g, ln1_b, ln2_g, ln2_b, ffn_w_gate, ffn_w_up, ffn_w_down,
           moe_w_router, moe_b_router, moe_w_gate, moe_w_up, moe_w_down):
    B, T, _ = x_prompt.shape
    Bs = x_sample.shape[0]
    lb_cum = jnp.cumsum(jax.nn.softmax(lb_param.astype(F32), axis=0), axis=0)
    lb_all = lb_cum - lb_cum[0]
    caches = (cache_kv_w128, cache_kv_w512, cache_kv_w2048)
    xp_f = x_prompt.reshape(B * T, D_MODEL)
    xs_f = x_sample.reshape(Bs, D_MODEL)
    xp_b = xp_f.astype(BF16)
    xs_b = xs_f.astype(BF16)
    zps = []
    kv_s = [[] for _ in ATT_GROUPS]
    hg_p, hg_s = [], []
    for i in range(DEPTH):
        hp_f, hp_b, hs_f, hs_b, S_p, S_s, zp, kvs = _mixer(
            xp_b, xs_b, xp_f, xs_f, i, B, T, caches, state_hgrn, w_in, lb_all, hgrn_norm,
            w_branch_a, w_branch_b, w_out, ln1_g, ln1_b)
        zps.append(zp)
        for g in range(N_GROUPS):
            kv_s[g].append(kvs[g])
        hg_p.append(S_p)
        hg_s.append(S_s)
        j = i // 2
        if i % 2 == 0:
            xp_f, xp_b, xs_f, xs_b = _dense_ffn(hp_f, hp_b, hs_f, hs_b, j, ffn_w_gate, ffn_w_up, ffn_w_down,
                                                ln2_g, ln2_b, i)
        else:
            xp_f, xp_b, xs_f, xs_b = _moe_ffn(hp_f, hp_b, hs_f, hs_b, j, moe_w_router, moe_b_router,
                                              moe_w_gate, moe_w_up, moe_w_down, ln2_g, ln2_b, i)
    def prompt_kv(g):
        keep = min(ATT_GROUPS[g][0], T)
        return kv_rows(zps, g, B, T).reshape(DEPTH, B, keep, 2, ATT_HPG, ATT_DH)

    return (xp_f.reshape(B, T, D_MODEL), xs_f.reshape(Bs, 1, D_MODEL),
            prompt_kv(0), jnp.stack(kv_s[0], axis=0),
            prompt_kv(1), jnp.stack(kv_s[1], axis=0),
            prompt_kv(2), jnp.stack(kv_s[2], axis=0),
            jnp.stack(hg_p, axis=0), jnp.stack(hg_s, axis=0))
```
